```python
import math
import jax
import jax.numpy as jnp
from jax import lax
import numpy as np

D_MODEL = 4096
BATCH = 4
SEQ = 2048
DEPTH = 1
DEC_BATCH = 32
DEC_SEQ = 1
PAST_LEN = 8192
PAGE_SIZE = 128

D_RNN = 2048
N_RNN_BLOCKS = 16
RNN_BLOCK = D_RNN // N_RNN_BLOCKS
CONV_W = 4
LRU_C = 8.0
N_HEADS = 16
HEAD_DIM = 128
N_KV = 4
HPG = N_HEADS // N_KV
CMP_BLOCK = 64
SEL_BLOCK = CMP_BLOCK
N_SEL = 16
WINDOW = 512
WIN_QBLK = 128
SEL_QBLK = 32
N_BUCKETS = 32
MAX_DISTANCE = 128
D_FF = -(-8 * D_MODEL // (3 * 256)) * 256
Q_W = N_HEADS * HEAD_DIM
KV_W = 2 * N_KV * HEAD_DIM
IN_SPLITS = (D_RNN, D_RNN, Q_W, KV_W, KV_W, KV_W, 3 * N_HEADS, D_MODEL, D_MODEL)
D_IN = sum(IN_SPLITS)
EPS = 1e-6
NEG_INF = -1e30
FORCE = 1e9
SCALE = HEAD_DIM ** -0.5

kernel_name = 'hybrid_rglru_nsa_decoder_step'


def _rmsnorm(x, g):
    xf = x.astype(jnp.float32)
    y = xf * lax.rsqrt(jnp.mean(xf * xf, axis=-1, keepdims=True) + EPS)
    return (y * g.astype(jnp.float32)).astype(x.dtype)


def _split_cols(z):
    parts, off = [], 0
    for w in IN_SPLITS:
        parts.append(z[..., off:off + w])
        off += w
    return parts


def _causal_conv(x, buf, w, b):
    t = x.shape[1]
    xp = jnp.concatenate([buf, x], axis=1)
    y = b
    for k in range(CONV_W):
        y = y + w[k] * xp[:, k:k + t]
    return y, xp[:, t:]


def _linear_scan(a, b, h0):
    b = b.at[:, 0].add(a[:, 0] * h0)

    def combine(l, r):
        return (l[0] * r[0], r[0] * l[1] + r[1])

    _, h = lax.associative_scan(combine, (a, b), axis=1)
    return h


def _rglru_branch(xr, gr, conv_buf, h0, conv_w, conv_b, lru_wa, lru_ba, lru_wi, lru_bi, lru_lambda):
    b, t, _ = xr.shape
    xc, conv_new = _causal_conv(xr, conv_buf, conv_w, conv_b)
    xb = xc.reshape(b, t, N_RNN_BLOCKS, RNN_BLOCK)
    r = jax.nn.sigmoid(jnp.einsum('btnc,nce->btne', xb, lru_wa).reshape(b, t, D_RNN) + lru_ba)
    i = jax.nn.sigmoid(jnp.einsum('btnc,nce->btne', xb, lru_wi).reshape(b, t, D_RNN) + lru_bi)
    log_a = -LRU_C * r.astype(jnp.float32) * jax.nn.softplus(-lru_lambda.astype(jnp.float32))
    a = jnp.exp(log_a)
    u = jnp.sqrt(-jnp.expm1(2.0 * log_a)) * (i * xc).astype(jnp.float32)
    h = _linear_scan(a, u, h0.astype(jnp.float32))
    y = h.astype(xr.dtype) * jax.nn.gelu(gr)
    return y, conv_new, h[:, -1].astype(h0.dtype)


def _bucket(dist):
    n = jnp.maximum(dist, 0)
    exact = N_BUCKETS // 2
    nf = jnp.maximum(n, 1).astype(jnp.float32)
    large = exact + (jnp.log(nf / exact) / math.log(MAX_DISTANCE / exact)
                     * (N_BUCKETS - exact)).astype(jnp.int32)
    return jnp.where(n < exact, n, jnp.minimum(large, N_BUCKETS - 1))


def _rel_bias(table, dist):
    bias = table[_bucket(dist)].astype(jnp.float32)
    return jnp.moveaxis(bias.reshape(dist.shape + (N_KV, HPG)), (-2, -1), (0, 1))


def _masked_softmax(s, valid):
    s = jnp.where(valid, s, NEG_INF)
    m = jnp.max(s, axis=-1, keepdims=True)
    e = jnp.where(valid, jnp.exp(s - m), 0.0)
    return e / jnp.maximum(jnp.sum(e, axis=-1, keepdims=True), 1e-30)


def _attn_probs(q, k, bias, valid):
    s = jnp.einsum('...qghd,...kgd->...ghqk', q, k).astype(jnp.float32) * SCALE + bias
    return _masked_softmax(s, valid)


def _attn_out(p, v):
    return jnp.einsum('...ghqk,...kgd->...qghd', p.astype(v.dtype), v)


def _to_blocks(past, new):
    parts = ([] if past is None else [past]) + [new]
    tk = sum(p.shape[1] for p in parts)
    pad = (-tk) % CMP_BLOCK
    if pad:
        parts.append(jnp.zeros((new.shape[0], pad) + new.shape[2:], new.dtype))
    kv = parts[0] if len(parts) == 1 else jnp.concatenate(parts, axis=1)
    return kv.reshape((kv.shape[0], -1, CMP_BLOCK) + kv.shape[2:])


def _nsa_cmp_sel(q, past_cmp, new_cmp, past_sel, new_sel, q_pos, rel_bias, nsa_w_cmp):
    b, tq = q.shape[:2]
    cb = _to_blocks(past_cmp, new_cmp)
    n_blk = cb.shape[1]
    kvc = jnp.einsum('bnlcgd,cgl->bncgd', cb, nsa_w_cmp.astype(cb.dtype))
    blk = jnp.arange(n_blk, dtype=jnp.int32)
    dist = q_pos[:, None] - ((blk + 1) * CMP_BLOCK - 1)[None, :]
    p_cmp = _attn_probs(q, kvc[:, :, 0], _rel_bias(rel_bias, dist), dist >= 0)
    o_cmp = _attn_out(p_cmp, kvc[:, :, 1])
    imp = jnp.sum(p_cmp, axis=2)
    cur = (q_pos // SEL_BLOCK)[:, None]
    forced = (blk == 0) | (blk == cur) | (blk == cur - 1)
    score = jnp.where(forced, FORCE, imp)
    score = jnp.where(blk <= cur, score, -FORCE)
    k_sel = min(N_SEL, n_blk)
    _, idx = lax.top_k(score, k_sel)
    sb = _to_blocks(past_sel, new_sel)
    qb = min(SEL_QBLK, tq)
    nqb = -(-tq // qb)
    pad = nqb * qb - tq
    qp = jnp.pad(q, ((0, 0), (0, pad), (0, 0), (0, 0), (0, 0)))
    qp = qp.reshape(b, nqb, qb, N_KV, HPG, HEAD_DIM).swapaxes(0, 1)
    ip = jnp.pad(idx, ((0, 0), (0, 0), (0, pad), (0, 0)))
    ip = ip.reshape(b, N_KV, nqb, qb, k_sel).transpose(2, 0, 1, 3, 4)
    pp = jnp.pad(q_pos, (0, pad)).reshape(nqb, qb)
    b_ix = jnp.arange(b)[:, None, None, None]
    g_ix = jnp.arange(N_KV)[None, :, None, None]
    table_g = rel_bias.reshape(N_BUCKETS, N_KV, HPG)
    offs = jnp.arange(SEL_BLOCK, dtype=jnp.int32)
    kl = k_sel * SEL_BLOCK

    def sel_block(args):
        qi, ii, pi = args
        kg = sb[b_ix, ii, :, 0, g_ix].reshape(b, N_KV, qb, kl, HEAD_DIM)
        vg = sb[b_ix, ii, :, 1, g_ix].reshape(b, N_KV, qb, kl, HEAD_DIM)
        kpos = (ii[..., None] * SEL_BLOCK + offs).reshape(b, N_KV, qb, kl)
        dist_s = pi[None, None, :, None] - kpos
        bias = jnp.moveaxis(table_g[_bucket(dist_s), g_ix].astype(jnp.float32), -1, 2)
        s = jnp.einsum('bqghd,bgqkd->bghqk', qi, kg).astype(jnp.float32) * SCALE + bias
        p = _masked_softmax(s, (dist_s >= 0)[:, :, None])
        return jnp.einsum('bghqk,bgqkd->bqghd', p.astype(vg.dtype), vg)

    o_sel = lax.map(sel_block, (qp, ip, pp))
    o_sel = o_sel.swapaxes(0, 1).reshape(b, nqb * qb, N_KV, HPG, HEAD_DIM)[:, :tq]
    return o_cmp, o_sel


def _window_prompt(q, kv_win, rel_bias):
    b, t = q.shape[:2]
    nb = t // WIN_QBLK
    nback = WINDOW // WIN_QBLK
    kvp = jnp.pad(kv_win, ((0, 0), (WINDOW, 0), (0, 0), (0, 0), (0, 0)))
    kvb = kvp.reshape(b, nb + nback, WIN_QBLK, 2, N_KV, HEAD_DIM)
    band = jnp.concatenate([kvb[:, j:j + nb] for j in range(nback + 1)], axis=2)
    qb = q.reshape(b, nb, WIN_QBLK, N_KV, HPG, HEAD_DIM)
    qpos = jnp.arange(nb)[:, None] * WIN_QBLK + jnp.arange(WIN_QBLK)
    kpos = (jnp.arange(nb)[:, None] - nback) * WIN_QBLK + jnp.arange((nback + 1) * WIN_QBLK)
    dist = qpos[:, :, None] - kpos[:, None, :]
    valid = (dist >= 0) & (dist < WINDOW) & (kpos[:, None, :] >= 0)
    bias = _rel_bias(rel_bias, dist[0])
    p = _attn_probs(qb, band[:, :, :, 0], bias, valid[:, None, None])
    o = _attn_out(p, band[:, :, :, 1])
    return o.reshape(b, t, N_KV, HPG, HEAD_DIM)


def _window_sample(q, kv_all, q_pos, k_pos, rel_bias):
    dist = q_pos[:, None] - k_pos[None, :]
    valid = (dist >= 0) & (dist < WINDOW)
    p = _attn_probs(q, kv_all[:, :, 0], _rel_bias(rel_bias, dist), valid)
    return _attn_out(p, kv_all[:, :, 1])


def _layer(x, q_pos, past_cmp, past_sel, win_buf, conv_buf, h0, rel_bias,
           norm_mix, w_in, conv_w, conv_b, lru_wa, lru_ba, lru_wi, lru_bi, lru_lambda,
           nsa_w_cmp, w_proj_a, w_proj_b, w_out, norm_ffn, w_gate, w_up, w_down):
    b, t, _ = x.shape
    xn = _rmsnorm(x, norm_mix)
    z = jnp.einsum('btd,de->bte', xn, w_in)
    xr, gr, q, kv_c, kv_s, kv_w, g_nsa, g_a, g_b = _split_cols(z)
    q = q.reshape(b, t, N_KV, HPG, HEAD_DIM)
    kv_c = kv_c.reshape(b, t, 2, N_KV, HEAD_DIM)
    kv_s = kv_s.reshape(b, t, 2, N_KV, HEAD_DIM)
    kv_w = kv_w.reshape(b, t, 2, N_KV, HEAD_DIM)
    ya, conv_new, h_last = _rglru_branch(xr, gr, conv_buf, h0, conv_w, conv_b,
                                         lru_wa, lru_ba, lru_wi, lru_bi, lru_lambda)
    o_cmp, o_sel = _nsa_cmp_sel(q, past_cmp, kv_c, past_sel, kv_s, q_pos, rel_bias, nsa_w_cmp)
    if win_buf is None:
        o_win = _window_prompt(q, kv_w, rel_bias)
        win_new = kv_w[:, -min(WINDOW, t):]
    else:
        wc = win_buf.shape[1]
        kv_all = jnp.concatenate([win_buf, kv_w], axis=1)
        k_pos = q_pos[0] - wc + jnp.arange(wc + t, dtype=jnp.int32)
        o_win = _window_sample(q, kv_all, q_pos, k_pos, rel_bias)
        win_new = kv_all[:, -wc:]
    gates = jax.nn.sigmoid(g_nsa.reshape(b, t, 3, N_KV, HPG))[..., None]
    o_nsa = gates[:, :, 0] * o_cmp + gates[:, :, 1] * o_sel + gates[:, :, 2] * o_win
    yb = o_nsa.reshape(b, t, Q_W)
    mix = (jax.nn.sigmoid(g_a) * jnp.einsum('btr,rd->btd', ya, w_proj_a)
           + jax.nn.sigmoid(g_b) * jnp.einsum('btr,rd->btd', yb, w_proj_b))
    h = x + jnp.einsum('btd,de->bte', mix, w_out)
    hn = _rmsnorm(h, norm_ffn)
    ff = jax.nn.silu(jnp.einsum('btd,df->btf', hn, w_gate)) * jnp.einsum('btd,df->btf', hn, w_up)
    y = h + jnp.einsum('btf,fd->btd', ff, w_down)
    return y, kv_c, kv_s, win_new, conv_new, h_last


def setup_inputs(seed: int = 0) -> dict:
    key = jax.random.key(seed)
    keys = jax.random.split(key, 32)
    f32 = jnp.float32

    def nrm(i, shape, scale):
        return scale * jax.random.normal(keys[i], shape, f32)

    n_pages = PAST_LEN // PAGE_SIZE
    n_used = DEC_BATCH * n_pages
    n_pool = (5 * n_used + 3) // 4
    wc = min(WINDOW, PAST_LEN)
    kv_row = (2, N_KV, HEAD_DIM)
    page_table = jax.random.permutation(keys[7], n_pool)[:n_used].reshape(DEC_BATCH, n_pages).astype(jnp.int32)
    u = jax.random.uniform(keys[11], (DEPTH, D_RNN), f32, 0.9, 0.999)
    a0 = u ** (1.0 / LRU_C)
    lam = jnp.log(a0) - jnp.log1p(-a0)
    return {
        'x_prompt': nrm(0, (BATCH, SEQ, D_MODEL), 1.0),
        'x_sample': nrm(1, (DEC_BATCH, DEC_SEQ, D_MODEL), 1.0),
        'cache_cmp_kv': nrm(2, (DEPTH, n_pool, PAGE_SIZE) + kv_row, 1.0),
        'cache_sel_kv': nrm(3, (DEPTH, n_pool, PAGE_SIZE) + kv_row, 1.0),
        'cache_win_kv': nrm(4, (DEPTH, DEC_BATCH, wc) + kv_row, 1.0),
        'state_rglru_h': nrm(5, (DEPTH, DEC_BATCH, D_RNN), 0.5),
        'state_conv': nrm(6, (DEPTH, DEC_BATCH, CONV_W - 1, D_RNN), 1.0),
        'page_table': page_table,
        'rel_bias': nrm(8, (N_BUCKETS, N_HEADS), 0.5),
        'norm_mix': 1.0 + nrm(9, (DEPTH, D_MODEL), 0.05),
        'w_in': nrm(10, (DEPTH, D_MODEL, D_IN), D_MODEL ** -0.5),
        'conv_w': nrm(12, (DEPTH, CONV_W, D_RNN), CONV_W ** -0.5),
        'conv_b': nrm(13, (DEPTH, D_RNN), 0.02),
        'lru_wa': nrm(14, (DEPTH, N_RNN_BLOCKS, RNN_BLOCK, RNN_BLOCK), RNN_BLOCK ** -0.5),
        'lru_ba': nrm(15, (DEPTH, D_RNN), 0.1),
        'lru_wi': nrm(16, (DEPTH, N_RNN_BLOCKS, RNN_BLOCK, RNN_BLOCK), RNN_BLOCK ** -0.5),
        'lru_bi': nrm(17, (DEPTH, D_RNN), 0.1),
        'lru_lambda': lam,
        'nsa_w_cmp': 1.0 / CMP_BLOCK + nrm(18, (DEPTH, 2, N_KV, CMP_BLOCK), CMP_BLOCK ** -0.5),
        'w_proj_a': nrm(19, (DEPTH, D_RNN, D_MODEL), D_RNN ** -0.5),
        'w_proj_b': nrm(20, (DEPTH, Q_W, D_MODEL), Q_W ** -0.5),
        'w_out': nrm(21, (DEPTH, D_MODEL, D_MODEL), D_MODEL ** -0.5),
        'norm_ffn': 1.0 + nrm(22, (DEPTH, D_MODEL), 0.05),
        'w_gate': nrm(23, (DEPTH, D_MODEL, D_FF), D_MODEL ** -0.5),
        'w_up': nrm(24, (DEPTH, D_MODEL, D_FF), D_MODEL ** -0.5),
        'w_down': nrm(25, (DEPTH, D_FF, D_MODEL), D_FF ** -0.5),
        'norm_final': 1.0 + nrm(26, (D_MODEL,), 0.05),
    }


def reference(x_prompt, x_sample, cache_cmp_kv, cache_sel_kv, cache_win_kv, state_rglru_h, state_conv,
              page_table, rel_bias, norm_mix, w_in, conv_w, conv_b, lru_wa, lru_ba, lru_wi, lru_bi,
              lru_lambda, nsa_w_cmp, w_proj_a, w_proj_b, w_out, norm_ffn, w_gate, w_up, w_down, norm_final):
    bp, tp, _ = x_prompt.shape
    bs, ts, _ = x_sample.shape
    pos_p = jnp.arange(tp, dtype=jnp.int32)
    pos_s = PAST_LEN + jnp.arange(ts, dtype=jnp.int32)
    hp, hs = x_prompt, x_sample
    cmp_p, sel_p, win_p, h_p, conv_p = [], [], [], [], []
    cmp_s, sel_s, win_s, h_s, conv_s = [], [], [], [], []
    for l in range(DEPTH):
        w_l = (norm_mix[l], w_in[l], conv_w[l], conv_b[l], lru_wa[l], lru_ba[l], lru_wi[l], lru_bi[l],
               lru_lambda[l], nsa_w_cmp[l], w_proj_a[l], w_proj_b[l], w_out[l], norm_ffn[l],
               w_gate[l], w_up[l], w_down[l])
        hp, kc, ks, kw, cn, hn = _layer(hp, pos_p, None, None, None,
                                        jnp.zeros((bp, CONV_W - 1, D_RNN), x_prompt.dtype),
                                        jnp.zeros((bp, D_RNN), x_prompt.dtype), rel_bias, *w_l)
        cmp_p.append(kc.reshape(bp, tp // PAGE_SIZE, PAGE_SIZE, 2, N_KV, HEAD_DIM))
        sel_p.append(ks.reshape(bp, tp // PAGE_SIZE, PAGE_SIZE, 2, N_KV, HEAD_DIM))
        win_p.append(kw)
        conv_p.append(cn)
        h_p.append(hn)
        past_c = cache_cmp_kv[l, page_table].reshape(bs, -1, 2, N_KV, HEAD_DIM)
        past_s = cache_sel_kv[l, page_table].reshape(bs, -1, 2, N_KV, HEAD_DIM)
        hs, kc, ks, kw, cn, hn = _layer(hs, pos_s, past_c, past_s, cache_win_kv[l], state_conv[l],
                                        state_rglru_h[l], rel_bias, *w_l)
        cmp_s.append(kc)
        sel_s.append(ks)
        win_s.append(kw)
        conv_s.append(cn)
        h_s.append(hn)
    y_prompt = _rmsnorm(hp, norm_final)
    y_sample = _rmsnorm(hs, norm_final)
    return (y_prompt, y_sample,
            jnp.stack(cmp_p), jnp.stack(sel_p), jnp.stack(win_p), jnp.stack(h_p), jnp.stack(conv_p),
            jnp.stack(cmp_s), jnp.stack(sel_s), jnp.stack(win_s), jnp.stack(h_s), jnp.stack(conv_s))
```

```python
import functools
import math

import jax
import jax.numpy as jnp
from jax import lax
from jax.experimental import pallas as pl
from jax.experimental.pallas import tpu as pltpu

F32 = jnp.float32
BF16 = jnp.bfloat16
I32 = jnp.int32

D_MODEL = 4096
D_RNN = 2048
N_RNN_BLOCKS = 16
RNN_BLOCK = D_RNN // N_RNN_BLOCKS
CONV_W = 4
LRU_C = 8.0
N_HEADS = 16
HEAD_DIM = 128
N_KV = 4
HPG = N_HEADS // N_KV
CMP_BLOCK = 64
N_SEL = 16
WINDOW = 512
N_BUCKETS = 32
MAX_DISTANCE = 128
PAGE_SIZE = 128
Q_W = N_HEADS * HEAD_DIM
KV_W = 2 * N_KV * HEAD_DIM
EPS = 1e-6
NEG_INF = -1e30
FORCE = 1e9
SCALE = HEAD_DIM ** -0.5

OFF_XR = 0
OFF_GR = D_RNN
OFF_Q = 2 * D_RNN
OFF_KVC = OFF_Q + Q_W
OFF_KVS = OFF_KVC + KV_W
OFF_KVW = OFF_KVS + KV_W
OFF_GN = OFF_KVW + KV_W
OFF_GA = OFF_GN + 3 * N_HEADS
MAIN_W = OFF_GN

ATT_T = 256
HROWS = 8
VMEM_LIMIT = 56 * 1024 * 1024


def _cparams(n_axes):
    return pltpu.CompilerParams(dimension_semantics=("arbitrary",) * n_axes,
                                vmem_limit_bytes=VMEM_LIMIT)


def _sigmoid(x):
    return 1.0 / (1.0 + jnp.exp(-x))


def _gelu_tanh(x):
    return 0.5 * x * (1.0 + jnp.tanh(math.sqrt(2.0 / math.pi) * (x + 0.044715 * (x * x * x))))


def _dot(a, b):
    return jnp.dot(a, b, preferred_element_type=F32)


def _dot_nt(a, b):
    return lax.dot_general(a, b, (((1,), (1,)), ((), ())), preferred_element_type=F32)


def _bucket(dist):
    n = jnp.maximum(dist, 0)
    exact = N_BUCKETS // 2
    nf = jnp.maximum(n, 1).astype(F32)
    large = exact + (jnp.log(nf / exact) / math.log(MAX_DISTANCE / exact)
                     * (N_BUCKETS - exact)).astype(I32)
    return jnp.where(n < exact, n, jnp.minimum(large, N_BUCKETS - 1))


def _bias_lookup(bkt, tab_ref, cols):
    outs = [jnp.zeros(bkt.shape, F32) for _ in cols]
    for b in range(N_BUCKETS):
        eq = bkt == b
        outs = [jnp.where(eq, tab_ref[b, c], o) for c, o in zip(cols, outs)]
    return outs


def _block_scores(imp, blk, cur):
    score = jnp.where(blk == 0, FORCE, jnp.where(blk == cur, FORCE,
                                                 jnp.where(blk == cur - 1, FORCE, imp)))
    return jnp.where(blk <= cur, score, -FORCE)


def _ahead(col, score, later):
    return jnp.where(later, jnp.where(col >= score, 1.0, 0.0), jnp.where(col > score, 1.0, 0.0))


def _masked_softmax(s, valid):
    s = jnp.where(valid, s, NEG_INF)
    m = jnp.max(s, axis=-1, keepdims=True)
    e = jnp.where(valid, jnp.exp(s - m), 0.0)
    return e / jnp.maximum(jnp.sum(e, axis=-1, keepdims=True), 1e-30)


def _rmsnorm_kernel(x_ref, g_ref, o_ref):
    x = x_ref[...]
    y = x * lax.rsqrt(jnp.mean(x * x, axis=-1, keepdims=True) + EPS)
    o_ref[...] = (y * g_ref[...]).astype(o_ref.dtype)


def _rmsnorm(x, g, out_dtype, tm):
    m, d = x.shape
    return pl.pallas_call(
        _rmsnorm_kernel,
        out_shape=jax.ShapeDtypeStruct((m, d), out_dtype),
        grid=(m // tm,),
        in_specs=[pl.BlockSpec((tm, d), lambda i: (i, 0)),
                  pl.BlockSpec((1, d), lambda i: (0, 0))],
        out_specs=pl.BlockSpec((tm, d), lambda i: (i, 0)),
        compiler_params=_cparams(1),
        name="rmsnorm",
    )(x, g.reshape(1, d))


def _mm_kernel(a_ref, w_ref, o_ref):
    o_ref[...] = _dot(a_ref[...], w_ref[...].astype(BF16)).astype(o_ref.dtype)


def _mm_res_kernel(a_ref, w_ref, r_ref, o_ref):
    o_ref[...] = r_ref[...] + _dot(a_ref[...], w_ref[...].astype(BF16))


def _mm(a, w, *, n_cols, tm, tn, out_dtype=F32, res=None, k_block=0, tk=None, name="mm"):
    m = a.shape[0]
    tk = a.shape[1] if tk is None else tk
    grid = (m // tm, n_cols // tn)
    in_specs = [pl.BlockSpec((tm, tk), lambda i, j: (i, k_block)),
                pl.BlockSpec((tk, tn), lambda i, j: (k_block, j))]
    args = [a, w]
    kern = _mm_kernel
    if res is not None:
        in_specs.append(pl.BlockSpec((tm, tn), lambda i, j: (i, j)))
        args.append(res)
        kern = _mm_res_kernel
    return pl.pallas_call(
        kern,
        out_shape=jax.ShapeDtypeStruct((m, n_cols), out_dtype),
        grid=grid,
        in_specs=in_specs,
        out_specs=pl.BlockSpec((tm, tn), lambda i, j: (i, j)),
        compiler_params=_cparams(2),
        name=name,
    )(*args)


def _swiglu_kernel(a_ref, wg_ref, wu_ref, o_ref):
    a = a_ref[...]
    g = _dot(a, wg_ref[...].astype(BF16))
    u = _dot(a, wu_ref[...].astype(BF16))
    o_ref[...] = ((g * _sigmoid(g)) * u).astype(o_ref.dtype)


def _swiglu(a, wg, wu, *, tm, tn):
    m, k = a.shape
    n = wg.shape[1]
    return pl.pallas_call(
        _swiglu_kernel,
        out_shape=jax.ShapeDtypeStruct((m, n), BF16),
        grid=(m // tm, n // tn),
        in_specs=[pl.BlockSpec((tm, k), lambda i, j: (i, 0)),
                  pl.BlockSpec((k, tn), lambda i, j: (0, j)),
                  pl.BlockSpec((k, tn), lambda i, j: (0, j))],
        out_specs=pl.BlockSpec((tm, tn), lambda i, j: (i, j)),
        compiler_params=_cparams(2),
        name="ffn_gate_up",
    )(a, wg, wu)


def _mix_kernel(ya_ref, yb_ref, wa_ref, wb_ref, ga_ref, gb_ref, o_ref):
    pa = _dot(ya_ref[...], wa_ref[...].astype(BF16))
    pb = _dot(yb_ref[...], wb_ref[...].astype(BF16))
    o_ref[...] = (_sigmoid(ga_ref[...]) * pa + _sigmoid(gb_ref[...]) * pb).astype(o_ref.dtype)


def _mix(ya, yb, wa, wb, gab, *, tm, tn):
    m, k = ya.shape
    n = wa.shape[1]
    gb_off = n // tn
    return pl.pallas_call(
        _mix_kernel,
        out_shape=jax.ShapeDtypeStruct((m, n), BF16),
        grid=(m // tm, n // tn),
        in_specs=[pl.BlockSpec((tm, k), lambda i, j: (i, 0)),
                  pl.BlockSpec((tm, k), lambda i, j: (i, 0)),
                  pl.BlockSpec((k, tn), lambda i, j: (0, j)),
                  pl.BlockSpec((k, tn), lambda i, j: (0, j)),
                  pl.BlockSpec((tm, tn), lambda i, j: (i, j)),
                  pl.BlockSpec((tm, tn), lambda i, j: (i, j + gb_off))],
        out_specs=pl.BlockSpec((tm, tn), lambda i, j: (i, j)),
        compiler_params=_cparams(2),
        name="branch_merge",
    )(ya, yb, wa, wb, gab, gab)


def _lru_gates(xc, wa_ref, ba_ref, wi_ref, bi_ref, lam_ref, a_ref, u_ref):
    for n in range(N_RNN_BLOCKS):
        sl = slice(n * RNN_BLOCK, (n + 1) * RNN_BLOCK)
        xn = xc[:, sl]
        xb = xn.astype(BF16)
        r = _sigmoid(_dot(xb, wa_ref[n].astype(BF16)) + ba_ref[:, sl])
        i = _sigmoid(_dot(xb, wi_ref[n].astype(BF16)) + bi_ref[:, sl])
        neg_lam = -lam_ref[:, sl]
        softplus = jnp.maximum(neg_lam, 0.0) + jnp.log1p(jnp.exp(-jnp.abs(neg_lam)))
        log_a = (-LRU_C * r) * softplus
        a = jnp.exp(log_a)
        a_ref[:, sl] = a
        u_ref[:, sl] = jnp.sqrt(-jnp.tanh(log_a) * (a * a + 1.0)) * (i * xn)


def _rglru_prompt_kernel(xr_ref, gr_ref, cw_ref, cb_ref, wa_ref, ba_ref, wi_ref, bi_ref, lam_ref,
                         ya_ref, conv_ref, hl_ref, xp_ref, a_ref, u_ref, h_ref, carry_ref):
    ti = pl.program_id(1)
    tt, c = xr_ref.shape

    @pl.when(ti == 0)
    def _():
        xp_ref[0:8, :] = jnp.zeros((8, c), F32)
        carry_ref[...] = jnp.zeros((1, c), F32)

    x = xr_ref[...]
    xp_ref[8:8 + tt, :] = x
    xc = cb_ref[...]
    for k in range(CONV_W):
        xc = xc + cw_ref[k:k + 1, :] * xp_ref[5 + k:5 + k + tt, :]
    xp_ref[0:8, :] = xp_ref[tt:tt + 8, :]

    _lru_gates(xc, wa_ref, ba_ref, wi_ref, bi_ref, lam_ref, a_ref, u_ref)

    row = lax.broadcasted_iota(I32, (8, c), 0)

    def chunk(ci, carry):
        off = pl.multiple_of(ci * 8, 8)
        a = a_ref[pl.ds(off, 8), :]
        b = u_ref[pl.ds(off, 8), :]
        for s in (1, 2, 4):
            keep = row >= s
            b = jnp.where(keep, a * pltpu.roll(b, s, 0) + b, b)
            a = jnp.where(keep, a * pltpu.roll(a, s, 0), a)
        h = a * carry + b
        h_ref[pl.ds(off, 8), :] = h
        return h[7:8, :]

    carry = lax.fori_loop(0, tt // 8, chunk, carry_ref[...])
    carry_ref[...] = carry
    ya_ref[...] = (h_ref[...] * _gelu_tanh(gr_ref[...])).astype(ya_ref.dtype)

    @pl.when(ti == pl.num_programs(1) - 1)
    def _():
        conv_ref[0] = x[tt - (CONV_W - 1):, :]
        hl_ref[0] = carry


def _rglru_prompt(z, batch, seq, conv_w, conv_b, lru_wa, lru_ba, lru_wi, lru_bi, lam, *, tt):
    nt = seq // tt
    c = D_RNN
    row = lambda b, t: (b * nt + t, 0)
    full2 = lambda b, t: (0, 0)
    full3 = lambda b, t: (0, 0, 0)
    return pl.pallas_call(
        _rglru_prompt_kernel,
        out_shape=(jax.ShapeDtypeStruct((batch * seq, c), BF16),
                   jax.ShapeDtypeStruct((batch, CONV_W - 1, c), F32),
                   jax.ShapeDtypeStruct((batch, 1, c), F32)),
        grid=(batch, nt),
        in_specs=[pl.BlockSpec((tt, c), row),
                  pl.BlockSpec((tt, c), lambda b, t: (b * nt + t, 1)),
                  pl.BlockSpec((CONV_W, c), full2),
                  pl.BlockSpec((1, c), full2),
                  pl.BlockSpec((N_RNN_BLOCKS, RNN_BLOCK, RNN_BLOCK), full3),
                  pl.BlockSpec((1, c), full2),
                  pl.BlockSpec((N_RNN_BLOCKS, RNN_BLOCK, RNN_BLOCK), full3),
                  pl.BlockSpec((1, c), full2),
                  pl.BlockSpec((1, c), full2)],
        out_specs=(pl.BlockSpec((tt, c), row),
                   pl.BlockSpec((1, CONV_W - 1, c), lambda b, t: (b, 0, 0)),
                   pl.BlockSpec((1, 1, c), lambda b, t: (b, 0, 0))),
        scratch_shapes=[pltpu.VMEM((tt + 8, c), F32), pltpu.VMEM((tt, c), F32),
                        pltpu.VMEM((tt, c), F32), pltpu.VMEM((tt, c), F32),
                        pltpu.VMEM((1, c), F32)],
        compiler_params=_cparams(2),
        name="rglru_prompt",
    )(z, z, conv_w, conv_b.reshape(1, c), lru_wa, lru_ba.reshape(1, c), lru_wi,
      lru_bi.reshape(1, c), lam.reshape(1, c))


def _rglru_step_kernel(xr_ref, gr_ref, buf_ref, h0_ref, cw_ref, cb_ref, wa_ref, ba_ref, wi_ref,
                       bi_ref, lam_ref, ya_ref, hn_ref, a_ref, u_ref):
    x = xr_ref[...]
    xc = cb_ref[...]
    for k in range(CONV_W - 1):
        xc = xc + cw_ref[k:k + 1, :] * buf_ref[k]
    xc = xc + cw_ref[CONV_W - 1:CONV_W, :] * x
    _lru_gates(xc, wa_ref, ba_ref, wi_ref, bi_ref, lam_ref, a_ref, u_ref)
    h = a_ref[...] * h0_ref[...] + u_ref[...]
    hn_ref[...] = h
    ya_ref[...] = (h * _gelu_tanh(gr_ref[...])).astype(ya_ref.dtype)


def _rglru_step(z, buf_t, h0, conv_w, conv_b, lru_wa, lru_ba, lru_wi, lru_bi, lam):
    b, c = h0.shape
    full2 = lambda i: (0, 0)
    full3 = lambda i: (0, 0, 0)
    return pl.pallas_call(
        _rglru_step_kernel,
        out_shape=(jax.ShapeDtypeStruct((b, c), BF16), jax.ShapeDtypeStruct((b, c), F32)),
        grid=(1,),
        in_specs=[pl.BlockSpec((b, c), full2),
                  pl.BlockSpec((b, c), lambda i: (0, 1)),
                  pl.BlockSpec((CONV_W - 1, b, c), full3),
                  pl.BlockSpec((b, c), full2),
                  pl.BlockSpec((CONV_W, c), full2),
                  pl.BlockSpec((1, c), full2),
                  pl.BlockSpec((N_RNN_BLOCKS, RNN_BLOCK, RNN_BLOCK), full3),
                  pl.BlockSpec((1, c), full2),
                  pl.BlockSpec((N_RNN_BLOCKS, RNN_BLOCK, RNN_BLOCK), full3),
                  pl.BlockSpec((1, c), full2),
                  pl.BlockSpec((1, c), full2)],
        out_specs=(pl.BlockSpec((b, c), full2), pl.BlockSpec((b, c), full2)),
        scratch_shapes=[pltpu.VMEM((b, c), F32), pltpu.VMEM((b, c), F32)],
        compiler_params=_cparams(1),
        name="rglru_step",
    )(z, z, buf_t, h0, conv_w, conv_b.reshape(1, c), lru_wa, lru_ba.reshape(1, c), lru_wi,
      lru_bi.reshape(1, c), lam.reshape(1, c))


def _bias_tiles_kernel(tab_ref, bd_ref, bo_ref):
    h = pl.program_id(0)
    t = bd_ref.shape[1]
    d0 = lax.broadcasted_iota(I32, (t, t), 0) - lax.broadcasted_iota(I32, (t, t), 1)
    bd_ref[0] = _bias_lookup(_bucket(d0), tab_ref, [h])[0]
    bo_ref[0] = _bias_lookup(_bucket(d0 + t), tab_ref, [h])[0]


def _bias_tiles(rel_bias, t):
    shp = jax.ShapeDtypeStruct((N_HEADS, t, t), F32)
    spec = pl.BlockSpec((1, t, t), lambda h: (h, 0, 0))
    return pl.pallas_call(
        _bias_tiles_kernel,
        out_shape=(shp, shp),
        grid=(N_HEADS,),
        in_specs=[pl.BlockSpec(memory_space=pltpu.SMEM)],
        out_specs=(spec, spec),
        compiler_params=_cparams(1),
        name="bias_tiles",
    )(rel_bias)


def _cmp_prompt_kernel(tab_ref, q_ref, kc_ref, vc_ref, wk_ref, wv_ref, gate_ref,
                       oc_ref, sel_ref, kcmp_ref, vcmp_ref):
    g = pl.program_id(1)
    qi = pl.program_id(2)
    tq = q_ref.shape[0]
    seq = kc_ref.shape[0]
    nblk = seq // CMP_BLOCK

    @pl.when(qi == 0)
    def _():
        k3 = kc_ref[...].reshape(nblk, CMP_BLOCK, HEAD_DIM)
        v3 = vc_ref[...].reshape(nblk, CMP_BLOCK, HEAD_DIM)
        kcmp_ref[...] = jnp.sum(k3 * wk_ref[0, 0][None], axis=1)
        vcmp_ref[...] = jnp.sum(v3 * wv_ref[0, 0][None], axis=1)

    tpos = qi * tq + lax.broadcasted_iota(I32, (tq, nblk), 0)
    blk = lax.broadcasted_iota(I32, (tq, nblk), 1)
    dist = tpos - (blk * CMP_BLOCK + (CMP_BLOCK - 1))
    valid = dist >= 0
    biases = _bias_lookup(_bucket(dist), tab_ref, [g * HPG + h for h in range(HPG)])
    kb = kcmp_ref[...].astype(BF16)
    vb = vcmp_ref[...].astype(BF16)
    gates = _sigmoid(gate_ref[0, 0])
    imp = jnp.zeros((tq, nblk), F32)
    for h in range(HPG):
        sl = slice(h * HEAD_DIM, (h + 1) * HEAD_DIM)
        s = _dot_nt(q_ref[:, sl].astype(BF16), kb) * SCALE + biases[h]
        p = _masked_softmax(s, valid)
        imp = imp + p
        oc_ref[:, sl] = gates[:, h:h + 1] * _dot(p.astype(BF16), vb)

    cur = jnp.right_shift(tpos, 6)
    score = _block_scores(imp, blk, cur)
    rank = jnp.zeros((tq, nblk), F32)
    for j in range(nblk):
        col = score[:, j:j + 1]
        rank = rank + _ahead(col, score, blk > j)
    sel_ref[0, 0] = jnp.where(rank < float(min(N_SEL, nblk)), 1.0, 0.0).astype(sel_ref.dtype)


def _cmp_prompt(z, rel_bias, wexp, gn_t, batch, seq):
    tq = ATT_T
    nq = seq // tq
    nblk = seq // CMP_BLOCK
    m = batch * seq
    qcol = OFF_Q // (HPG * HEAD_DIM)
    kcol = OFF_KVC // HEAD_DIM
    return pl.pallas_call(
        _cmp_prompt_kernel,
        out_shape=(jax.ShapeDtypeStruct((m, Q_W), F32),
                   jax.ShapeDtypeStruct((batch, N_KV, seq, nblk), BF16)),
        grid=(batch, N_KV, nq),
        in_specs=[pl.BlockSpec(memory_space=pltpu.SMEM),
                  pl.BlockSpec((tq, HPG * HEAD_DIM), lambda b, g, i: (b * nq + i, qcol + g)),
                  pl.BlockSpec((seq, HEAD_DIM), lambda b, g, i: (b, kcol + g)),
                  pl.BlockSpec((seq, HEAD_DIM), lambda b, g, i: (b, kcol + N_KV + g)),
                  pl.BlockSpec((1, 1, CMP_BLOCK, HEAD_DIM), lambda b, g, i: (0, g, 0, 0)),
                  pl.BlockSpec((1, 1, CMP_BLOCK, HEAD_DIM), lambda b, g, i: (1, g, 0, 0)),
                  pl.BlockSpec((1, 1, tq, HPG), lambda b, g, i: (0, g, b * nq + i, 0))],
        out_specs=(pl.BlockSpec((tq, HPG * HEAD_DIM), lambda b, g, i: (b * nq + i, g)),
                   pl.BlockSpec((1, 1, tq, nblk), lambda b, g, i: (b, g, i, 0))),
        scratch_shapes=[pltpu.VMEM((nblk, HEAD_DIM), F32), pltpu.VMEM((nblk, HEAD_DIM), F32)],
        compiler_params=_cparams(3),
        name="cmp_prompt",
    )(rel_bias, z, z, z, wexp, wexp, gn_t)


def _attn_init(m_ref, l_ref, acc_ref):
    m_ref[...] = jnp.full(m_ref.shape, NEG_INF, F32)
    l_ref[...] = jnp.zeros(l_ref.shape, F32)
    acc_ref[...] = jnp.zeros(acc_ref.shape, F32)


def _attn_update(h, qh, k, v, bias, valid, m_ref, l_ref, acc_ref):
    s = _dot_nt(qh, k) * SCALE + bias
    if valid is not None:
        s = jnp.where(valid, s, NEG_INF)
    m_old = m_ref[h]
    m_new = jnp.maximum(m_old, jnp.max(s, axis=-1, keepdims=True))
    alpha = jnp.exp(m_old - m_new)
    e = jnp.exp(s - m_new)
    if valid is not None:
        e = jnp.where(valid, e, 0.0)
    l_ref[h] = alpha * l_ref[h] + jnp.sum(e, axis=-1, keepdims=True)
    acc_ref[h] = alpha * acc_ref[h] + _dot(e.astype(BF16), v)
    m_ref[h] = m_new


def _load_kv_tiles(qi, k_ref, v_ref, kb_ref, vb_ref):
    @pl.when(qi == 0)
    def _():
        kb_ref[...] = k_ref[...].reshape(kb_ref.shape).astype(BF16)
        vb_ref[...] = v_ref[...].reshape(vb_ref.shape).astype(BF16)


def _sel_prompt_kernel(tab_ref, q_ref, k_ref, v_ref, bd_ref, bo_ref, gate_ref, sel_ref, ex_ref,
                       os_ref, kb_ref, vb_ref, qb_ref, m_ref, l_ref, acc_ref):
    g = pl.program_id(1)
    qi = pl.program_id(2)
    t = q_ref.shape[0]
    _load_kv_tiles(qi, k_ref, v_ref, kb_ref, vb_ref)
    for h in range(HPG):
        qb_ref[h] = q_ref[:, h * HEAD_DIM:(h + 1) * HEAD_DIM].astype(BF16)
    _attn_init(m_ref, l_ref, acc_ref)
    sel = sel_ref[0, 0]
    causal = lax.broadcasted_iota(I32, (t, t), 0) >= lax.broadcasted_iota(I32, (t, t), 1)

    def tile(kj, kind):
        self = _dot(sel, ex_ref[kj])
        if kind == "diag":
            self = jnp.where(causal, self, 0.0)
        valid = self > 0.5
        k = kb_ref[kj]
        v = vb_ref[kj]
        for h in range(HPG):
            if kind == "far":
                bias = tab_ref[N_BUCKETS - 1, g * HPG + h]
            elif kind == "off":
                bias = bo_ref[h]
            else:
                bias = bd_ref[h]
            _attn_update(h, qb_ref[h], k, v, bias, valid, m_ref, l_ref, acc_ref)

    def far_body(kj, carry):
        tile(kj, "far")
        return carry

    lax.fori_loop(0, jnp.maximum(qi - 1, 0), far_body, 0)

    @pl.when(qi >= 1)
    def _():
        tile(qi - 1, "off")

    tile(qi, "diag")
    gates = _sigmoid(gate_ref[0, 0])
    for h in range(HPG):
        o = acc_ref[h] / jnp.maximum(l_ref[h], 1e-30)
        os_ref[:, h * HEAD_DIM:(h + 1) * HEAD_DIM] = gates[:, h:h + 1] * o


def _win_prompt_kernel(tab_ref, q_ref, k_ref, v_ref, bd_ref, bo_ref, gate_ref, oc_ref, os_ref,
                       yb_ref, kb_ref, vb_ref, qb_ref, m_ref, l_ref, acc_ref):
    g = pl.program_id(1)
    qi = pl.program_id(2)
    t = q_ref.shape[0]
    _load_kv_tiles(qi, k_ref, v_ref, kb_ref, vb_ref)
    for h in range(HPG):
        qb_ref[h] = q_ref[:, h * HEAD_DIM:(h + 1) * HEAD_DIM].astype(BF16)
    _attn_init(m_ref, l_ref, acc_ref)
    row = lax.broadcasted_iota(I32, (t, t), 0)
    col = lax.broadcasted_iota(I32, (t, t), 1)

    def tile(kj, kind):
        k = kb_ref[kj]
        v = vb_ref[kj]
        for h in range(HPG):
            if kind == "far":
                bias, valid = tab_ref[N_BUCKETS - 1, g * HPG + h], row < col
            elif kind == "off":
                bias, valid = bo_ref[h], None
            else:
                bias, valid = bd_ref[h], row >= col
            _attn_update(h, qb_ref[h], k, v, bias, valid, m_ref, l_ref, acc_ref)

    @pl.when(qi >= 2)
    def _():
        tile(qi - 2, "far")

    @pl.when(qi >= 1)
    def _():
        tile(qi - 1, "off")

    tile(qi, "diag")
    gates = _sigmoid(gate_ref[0, 0])
    for h in range(HPG):
        sl = slice(h * HEAD_DIM, (h + 1) * HEAD_DIM)
        o = acc_ref[h] / jnp.maximum(l_ref[h], 1e-30)
        yb_ref[:, sl] = (oc_ref[:, sl] + os_ref[:, sl] + gates[:, h:h + 1] * o).astype(yb_ref.dtype)


def _attn_prompt_specs(batch, seq, kv_off, gate_idx):
    t = ATT_T
    nq = seq // t
    qcol = OFF_Q // (HPG * HEAD_DIM)
    kcol = kv_off // HEAD_DIM
    rowblk = lambda b, g, i: (b * nq + i, g)
    specs = [pl.BlockSpec(memory_space=pltpu.SMEM),
             pl.BlockSpec((t, HPG * HEAD_DIM), lambda b, g, i: (b * nq + i, qcol + g)),
             pl.BlockSpec((seq, HEAD_DIM), lambda b, g, i: (b, kcol + g)),
             pl.BlockSpec((seq, HEAD_DIM), lambda b, g, i: (b, kcol + N_KV + g)),
             pl.BlockSpec((HPG, t, t), lambda b, g, i: (g, 0, 0)),
             pl.BlockSpec((HPG, t, t), lambda b, g, i: (g, 0, 0)),
             pl.BlockSpec((1, 1, t, HPG), lambda b, g, i: (gate_idx, g, b * nq + i, 0))]
    scratch = [pltpu.VMEM((seq // t, t, HEAD_DIM), BF16), pltpu.VMEM((seq // t, t, HEAD_DIM), BF16),
               pltpu.VMEM((HPG, t, HEAD_DIM), BF16),
               pltpu.VMEM((HPG, t, 1), F32), pltpu.VMEM((HPG, t, 1), F32),
               pltpu.VMEM((HPG, t, HEAD_DIM), F32)]
    return specs, scratch, pl.BlockSpec((t, HPG * HEAD_DIM), rowblk), (batch, N_KV, nq)


def _sel_prompt(z, rel_bias, bd, bo, gn_t, sel, expand, batch, seq):
    specs, scratch, out_spec, grid = _attn_prompt_specs(batch, seq, OFF_KVS, 1)
    t = ATT_T
    nq = seq // t
    nblk = seq // CMP_BLOCK
    specs += [pl.BlockSpec((1, 1, t, nblk), lambda b, g, i: (b, g, i, 0)),
              pl.BlockSpec((nq, nblk, t), lambda b, g, i: (0, 0, 0))]
    return pl.pallas_call(
        _sel_prompt_kernel,
        out_shape=jax.ShapeDtypeStruct((batch * seq, Q_W), F32),
        grid=grid, in_specs=specs, out_specs=out_spec, scratch_shapes=scratch,
        compiler_params=_cparams(3), name="sel_prompt",
    )(rel_bias, z, z, z, bd, bo, gn_t, sel, expand)


def _win_prompt(z, rel_bias, bd, bo, gn_t, oc, osel, batch, seq):
    specs, scratch, out_spec, grid = _attn_prompt_specs(batch, seq, OFF_KVW, 2)
    specs += [out_spec, out_spec]
    return pl.pallas_call(
        _win_prompt_kernel,
        out_shape=jax.ShapeDtypeStruct((batch * seq, Q_W), BF16),
        grid=grid, in_specs=specs, out_specs=out_spec, scratch_shapes=scratch,
        compiler_params=_cparams(3), name="win_prompt",
    )(rel_bias, z, z, z, bd, bo, gn_t, oc, osel)


def _cmp_pages_kernel(pt_ref, page_ref, w_ref, o_ref):
    x = page_ref[0].reshape(PAGE_SIZE // CMP_BLOCK, CMP_BLOCK, KV_W)
    o_ref[0, 0] = jnp.sum(x * w_ref[...][None], axis=1)


def _cmp_pages(cache, page_table, wrow):
    b, n_pages = page_table.shape
    per_page = PAGE_SIZE // CMP_BLOCK
    grid_spec = pltpu.PrefetchScalarGridSpec(
        num_scalar_prefetch=1,
        grid=(b, n_pages),
        in_specs=[pl.BlockSpec((1, PAGE_SIZE, KV_W), lambda i, p, pt: (pt[i, p], 0, 0)),
                  pl.BlockSpec((CMP_BLOCK, KV_W), lambda i, p, pt: (0, 0))],
        out_specs=pl.BlockSpec((1, 1, per_page, KV_W), lambda i, p, pt: (i, p, 0, 0)),
    )
    out = pl.pallas_call(
        _cmp_pages_kernel,
        out_shape=jax.ShapeDtypeStruct((b, n_pages, per_page, KV_W), F32),
        grid_spec=grid_spec,
        compiler_params=_cparams(2),
        name="cmp_pages",
    )(page_table, cache, wrow)
    return out.reshape(b, n_pages * per_page, KV_W)


def _cmp_sample_kernel(tab_ref, q_ref, kvc_ref, new_ref, w_ref, gate_ref,
                       oc_ref, idx_ref, kall_ref, vall_ref, imp_ref, *, q_pos):
    nblk = kvc_ref.shape[1]
    nk = kall_ref.shape[0]
    blk = lax.broadcasted_iota(I32, (1, nk), 1)
    dist = q_pos - (blk * CMP_BLOCK + (CMP_BLOCK - 1))
    valid = dist >= 0
    bkt = _bucket(dist)
    new_c = new_ref[0] * w_ref[0:1, :]
    kall_ref[...] = jnp.zeros(kall_ref.shape, F32)
    vall_ref[...] = jnp.zeros(vall_ref.shape, F32)
    for g in range(N_KV):
        ks = slice(g * HEAD_DIM, (g + 1) * HEAD_DIM)
        vs = slice((N_KV + g) * HEAD_DIM, (N_KV + g + 1) * HEAD_DIM)
        kall_ref[0:nblk, :] = kvc_ref[0][:, ks]
        vall_ref[0:nblk, :] = kvc_ref[0][:, vs]
        kall_ref[nblk:nblk + 1, :] = new_c[:, ks]
        vall_ref[nblk:nblk + 1, :] = new_c[:, vs]
        kb = kall_ref[...].astype(BF16)
        vb = vall_ref[...].astype(BF16)
        biases = _bias_lookup(bkt, tab_ref, [g * HPG + h for h in range(HPG)])
        rows = lax.broadcasted_iota(I32, (HROWS, nk), 0)
        bias = jnp.zeros((HROWS, nk), F32)
        for h in range(HPG):
            bias = jnp.where(rows == h, biases[h], bias)
        s = _dot_nt(q_ref[0, g].astype(BF16), kb) * SCALE + bias
        p = _masked_softmax(s, valid)
        gate = _sigmoid(gate_ref[0, 0, g])
        oc_ref[0, g] = gate * _dot(p.astype(BF16), vb)
        head_row = jnp.where(rows < HPG, p, 0.0)
        imp_ref[g:g + 1, :] = jnp.sum(head_row, axis=0, keepdims=True)

    imp = imp_ref[0:N_KV, :]
    blk4 = lax.broadcasted_iota(I32, (N_KV, nk), 1)
    cur = q_pos // CMP_BLOCK
    score = _block_scores(imp, blk4, cur)
    rank = jnp.zeros((N_KV, nk), F32)
    for j in range(nblk + 1):
        col = score[:, j:j + 1]
        rank = rank + _ahead(col, score, blk4 > j)
    rank = jnp.where(blk4 <= nblk, rank, float(nk))
    blkf = blk4.astype(F32)
    slot = lax.broadcasted_iota(I32, (N_KV, N_SEL), 1)
    idx = jnp.zeros((N_KV, N_SEL), F32)
    for r in range(N_SEL):
        pick = jnp.sum(jnp.where(rank == float(r), blkf, 0.0), axis=-1, keepdims=True)
        idx = jnp.where(slot == r, pick, idx)
    idx_ref[0] = idx.astype(I32)


def _cmp_sample(rel_bias, q8, kvc, kvc_new, wrow, gates8, q_pos):
    b, nblk, _ = kvc.shape
    nk = 2 * HEAD_DIM
    assert nblk + 1 <= nk and q_pos // CMP_BLOCK == nblk
    return pl.pallas_call(
        functools.partial(_cmp_sample_kernel, q_pos=q_pos),
        out_shape=(jax.ShapeDtypeStruct((b, N_KV, HROWS, HEAD_DIM), F32),
                   jax.ShapeDtypeStruct((b, N_KV, N_SEL), I32)),
        grid=(b,),
        in_specs=[pl.BlockSpec(memory_space=pltpu.SMEM),
                  pl.BlockSpec((1, N_KV, HROWS, HEAD_DIM), lambda i: (i, 0, 0, 0)),
                  pl.BlockSpec((1, nblk, KV_W), lambda i: (i, 0, 0)),
                  pl.BlockSpec((1, 1, KV_W), lambda i: (i, 0, 0)),
                  pl.BlockSpec((CMP_BLOCK, KV_W), lambda i: (0, 0)),
                  pl.BlockSpec((1, 1, N_KV, HROWS, 1), lambda i: (i, 0, 0, 0, 0))],
        out_specs=(pl.BlockSpec((1, N_KV, HROWS, HEAD_DIM), lambda i: (i, 0, 0, 0)),
                   pl.BlockSpec((1, N_KV, N_SEL), lambda i: (i, 0, 0))),
        scratch_shapes=[pltpu.VMEM((nk, HEAD_DIM), F32), pltpu.VMEM((nk, HEAD_DIM), F32),
                        pltpu.VMEM((HROWS, nk), F32)],
        compiler_params=_cparams(1),
        name="cmp_sample",
    )(rel_bias, q8, kvc, kvc_new, wrow, gates8)


def _sel_sample_kernel(pt_ref, idx_ref, tab_ref, q_ref, k_ref, v_ref, kn_ref, vn_ref, gate_ref,
                       os_ref, m_ref, l_ref, acc_ref, *, q_pos, n_past):
    b = pl.program_id(0)
    g = pl.program_id(1)
    j = pl.program_id(2)

    @pl.when(j == 0)
    def _():
        _attn_init(m_ref, l_ref, acc_ref)

    n = idx_ref[b, g, j]
    is_new = n >= n_past
    kpos = n * CMP_BLOCK + lax.broadcasted_iota(I32, (1, CMP_BLOCK), 1)
    dist = q_pos - kpos
    valid = dist >= 0
    biases = _bias_lookup(_bucket(dist), tab_ref, [g * HPG + h for h in range(HPG)])
    rows = lax.broadcasted_iota(I32, (HROWS, CMP_BLOCK), 0)
    bias = jnp.zeros((HROWS, CMP_BLOCK), F32)
    for h in range(HPG):
        bias = jnp.where(rows == h, biases[h], bias)
    first = lax.broadcasted_iota(I32, (CMP_BLOCK, HEAD_DIM), 0) == 0
    k = jnp.where(is_new, jnp.where(first, kn_ref[0, 0], 0.0), k_ref[0])
    v = jnp.where(is_new, jnp.where(first, vn_ref[0, 0], 0.0), v_ref[0])
    _attn_update(0, q_ref[0, 0].astype(BF16), k.astype(BF16), v.astype(BF16), bias, valid,
                 m_ref, l_ref, acc_ref)

    @pl.when(j == pl.num_programs(2) - 1)
    def _():
        o = acc_ref[0] / jnp.maximum(l_ref[0], 1e-30)
        os_ref[0, 0] = _sigmoid(gate_ref[0, 0, 0]) * o


def _sel_sample(page_table, idx, rel_bias, q8, cache_half, kvs_new, gates8, q_pos):
    b = q8.shape[0]
    n_past = q_pos // CMP_BLOCK
    per_page = PAGE_SIZE // CMP_BLOCK

    def half(i, g, j, pt, ix):
        n = jnp.minimum(ix[i, g, j], n_past - 1)
        return pt[i, n // per_page] * per_page + n % per_page

    grid_spec = pltpu.PrefetchScalarGridSpec(
        num_scalar_prefetch=2,
        grid=(b, N_KV, N_SEL),
        in_specs=[pl.BlockSpec(memory_space=pltpu.SMEM),
                  pl.BlockSpec((1, 1, HROWS, HEAD_DIM), lambda i, g, j, pt, ix: (i, g, 0, 0)),
                  pl.BlockSpec((1, CMP_BLOCK, HEAD_DIM),
                               lambda i, g, j, pt, ix: (half(i, g, j, pt, ix), 0, g)),
                  pl.BlockSpec((1, CMP_BLOCK, HEAD_DIM),
                               lambda i, g, j, pt, ix: (half(i, g, j, pt, ix), 0, N_KV + g)),
                  pl.BlockSpec((1, 1, 1, HEAD_DIM), lambda i, g, j, pt, ix: (i, g, 0, 0)),
                  pl.BlockSpec((1, 1, 1, HEAD_DIM), lambda i, g, j, pt, ix: (i, N_KV + g, 0, 0)),
                  pl.BlockSpec((1, 1, 1, HROWS, 1), lambda i, g, j, pt, ix: (i, 1, g, 0, 0))],
        out_specs=pl.BlockSpec((1, 1, HROWS, HEAD_DIM), lambda i, g, j, pt, ix: (i, g, 0, 0)),
        scratch_shapes=[pltpu.VMEM((1, HROWS, 1), F32), pltpu.VMEM((1, HROWS, 1), F32),
                        pltpu.VMEM((1, HROWS, HEAD_DIM), F32)],
    )
    return pl.pallas_call(
        functools.partial(_sel_sample_kernel, q_pos=q_pos, n_past=n_past),
        out_shape=jax.ShapeDtypeStruct((b, N_KV, HROWS, HEAD_DIM), F32),
        grid_spec=grid_spec,
        compiler_params=_cparams(3),
        name="sel_sample",
    )(page_table, idx, rel_bias, q8, cache_half, cache_half, kvs_new, kvs_new, gates8)


def _win_sample_kernel(tab_ref, q_ref, win_ref, new_ref, gate_ref, oc_ref, os_ref,
                       yb_ref, kall_ref, vall_ref):
    wc = win_ref.shape[1]
    nk = kall_ref.shape[0]
    kidx = lax.broadcasted_iota(I32, (1, nk), 1)
    dist = wc - kidx
    valid = jnp.where(dist >= 0, dist, WINDOW) < WINDOW
    bkt = _bucket(dist)
    kall_ref[...] = jnp.zeros(kall_ref.shape, F32)
    vall_ref[...] = jnp.zeros(vall_ref.shape, F32)
    for g in range(N_KV):
        ks = slice(g * HEAD_DIM, (g + 1) * HEAD_DIM)
        vs = slice((N_KV + g) * HEAD_DIM, (N_KV + g + 1) * HEAD_DIM)
        kall_ref[0:wc, :] = win_ref[0][:, ks]
        vall_ref[0:wc, :] = win_ref[0][:, vs]
        kall_ref[wc:wc + 1, :] = new_ref[0][:, ks]
        vall_ref[wc:wc + 1, :] = new_ref[0][:, vs]
        biases = _bias_lookup(bkt, tab_ref, [g * HPG + h for h in range(HPG)])
        rows = lax.broadcasted_iota(I32, (HROWS, nk), 0)
        bias = jnp.zeros((HROWS, nk), F32)
        for h in range(HPG):
            bias = jnp.where(rows == h, biases[h], bias)
        s = _dot_nt(q_ref[0, g].astype(BF16), kall_ref[...].astype(BF16)) * SCALE + bias
        p = _masked_softmax(s, valid)
        o = _dot(p.astype(BF16), vall_ref[...].astype(BF16))
        yb_ref[0, g] = oc_ref[0, g] + os_ref[0, g] + _sigmoid(gate_ref[0, 0, g]) * o


def _win_sample(rel_bias, q8, win, kvw_new, gates8, oc, osel):
    b, wc, _ = win.shape
    nk = wc + HEAD_DIM
    hspec = pl.BlockSpec((1, N_KV, HROWS, HEAD_DIM), lambda i: (i, 0, 0, 0))
    return pl.pallas_call(
        _win_sample_kernel,
        out_shape=jax.ShapeDtypeStruct((b, N_KV, HROWS, HEAD_DIM), F32),
        grid=(b,),
        in_specs=[pl.BlockSpec(memory_space=pltpu.SMEM),
                  hspec,
                  pl.BlockSpec((1, wc, KV_W), lambda i: (i, 0, 0)),
                  pl.BlockSpec((1, 1, KV_W), lambda i: (i, 0, 0)),
                  pl.BlockSpec((1, 1, N_KV, HROWS, 1), lambda i: (i, 2, 0, 0, 0)),
                  hspec, hspec],
        out_specs=hspec,
        scratch_shapes=[pltpu.VMEM((nk, HEAD_DIM), F32), pltpu.VMEM((nk, HEAD_DIM), F32)],
        compiler_params=_cparams(1),
        name="win_sample",
    )(rel_bias, q8, win, kvw_new, gates8, oc, osel)


def _in_proj(x, norm_w, w_in, w_gab, w_gn, *, tm, tm_norm):
    xn = _rmsnorm(x, norm_w, BF16, tm_norm)
    z = _mm(xn, w_in, n_cols=MAIN_W, tm=tm, tn=256, name="in_proj")
    gab = _mm(xn, w_gab, n_cols=2 * D_MODEL, tm=tm, tn=256, name="in_proj_gates")
    gn = _mm(xn, w_gn, n_cols=w_gn.shape[1], tm=tm, tn=128, name="in_proj_nsa_gates")
    return z, gab, gn


def _out_and_ffn(x, ya, yb, gab, w_proj_a, w_proj_b, w_out, norm_ffn, w_gate, w_up, w_down,
                 norm_final, *, tm, tm_norm):
    d_ff = w_gate.shape[1]
    mix = _mix(ya, yb, w_proj_a, w_proj_b, gab, tm=tm, tn=256)
    h = _mm(mix, w_out, n_cols=D_MODEL, tm=tm, tn=256, res=x, name="out_proj")
    hn = _rmsnorm(h, norm_ffn, BF16, tm_norm)
    ff = _swiglu(hn, w_gate, w_up, tm=tm, tn=256)
    half = d_ff // 2
    y = _mm(ff, w_down, n_cols=D_MODEL, tm=tm, tn=256, res=h, k_block=0, tk=half, name="ffn_down0")
    y = _mm(ff, w_down, n_cols=D_MODEL, tm=tm, tn=256, res=y, k_block=1, tk=half, name="ffn_down1")
    return _rmsnorm(y, norm_final, F32, tm_norm)


def kernel(x_prompt, x_sample, cache_cmp_kv, cache_sel_kv, cache_win_kv, state_rglru_h, state_conv,
           page_table, rel_bias, norm_mix, w_in, conv_w, conv_b, lru_wa, lru_ba, lru_wi, lru_bi,
           lru_lambda, nsa_w_cmp, w_proj_a, w_proj_b, w_out, norm_ffn, w_gate, w_up, w_down,
           norm_final):
    depth = w_in.shape[0]
    assert depth == 1, "single-layer trunk"
    bp, tp, _ = x_prompt.shape
    bs, ts, _ = x_sample.shape
    assert ts == 1
    n_pool = cache_cmp_kv.shape[1]
    past_len = page_table.shape[1] * PAGE_SIZE
    wc = cache_win_kv.shape[2]

    w_in0 = w_in[0]
    w_gab = w_in0[:, OFF_GA:]
    w_gn = jnp.pad(w_in0[:, OFF_GN:OFF_GA], ((0, 0), (0, HEAD_DIM - 3 * N_HEADS)))
    wrow = jnp.repeat(nsa_w_cmp[0].reshape(2 * N_KV, CMP_BLOCK).T, HEAD_DIM, axis=1)
    wexp = wrow.reshape(CMP_BLOCK, 2, N_KV, HEAD_DIM).transpose(1, 2, 0, 3)
    layer_w = (conv_w[0], conv_b[0], lru_wa[0], lru_ba[0], lru_wi[0], lru_bi[0], lru_lambda[0])
    tail_w = (w_proj_a[0], w_proj_b[0], w_out[0], norm_ffn[0], w_gate[0], w_up[0], w_down[0],
              norm_final)

    mp = bp * tp
    xp = x_prompt.reshape(mp, D_MODEL)
    z, gab, gn = _in_proj(xp, norm_mix[0], w_in0, w_gab, w_gn, tm=1024, tm_norm=256)
    ya, conv_p, h_p = _rglru_prompt(z, bp, tp, *layer_w, tt=128)
    gn_t = gn[:, :3 * N_HEADS].reshape(mp, 3, N_KV, HPG).transpose(1, 2, 0, 3)
    bd, bo = _bias_tiles(rel_bias, ATT_T)
    oc, sel = _cmp_prompt(z, rel_bias, wexp, gn_t, bp, tp)
    nblk = tp // CMP_BLOCK
    key_blk = (jnp.arange(tp, dtype=I32) // CMP_BLOCK).reshape(tp // ATT_T, 1, ATT_T)
    expand = (key_blk == jnp.arange(nblk, dtype=I32)[None, :, None]).astype(BF16)
    osel = _sel_prompt(z, rel_bias, bd, bo, gn_t, sel, expand, bp, tp)
    yb = _win_prompt(z, rel_bias, bd, bo, gn_t, oc, osel, bp, tp)
    y_prompt = _out_and_ffn(xp, ya, yb, gab, *tail_w, tm=1024, tm_norm=256)

    kv_shape = (1, bp, tp // PAGE_SIZE, PAGE_SIZE, 2, N_KV, HEAD_DIM)
    cmp_p = z[:, OFF_KVC:OFF_KVS].reshape(kv_shape)
    sel_p = z[:, OFF_KVS:OFF_KVW].reshape(kv_shape)
    wlen = min(WINDOW, tp)
    win_p = z[:, OFF_KVW:MAIN_W].reshape(bp, tp, 2, N_KV, HEAD_DIM)[None, :, tp - wlen:]

    xs = x_sample.reshape(bs, D_MODEL)
    zs, gab_s, gn_s = _in_proj(xs, norm_mix[0], w_in0, w_gab, w_gn, tm=bs, tm_norm=bs)
    buf_t = state_conv[0].transpose(1, 0, 2)
    ya_s, h_s = _rglru_step(zs, buf_t, state_rglru_h[0], *layer_w)

    q8 = jnp.pad(zs[:, OFF_Q:OFF_KVC].reshape(bs, N_KV, HPG, HEAD_DIM),
                 ((0, 0), (0, 0), (0, HROWS - HPG), (0, 0)))
    gates8 = jnp.pad(gn_s[:, :3 * N_HEADS].reshape(bs, 3, N_KV, HPG),
                     ((0, 0), (0, 0), (0, 0), (0, HROWS - HPG)))[..., None]
    kvc_new = zs[:, OFF_KVC:OFF_KVS].reshape(bs, 1, KV_W)
    kvs_new = zs[:, OFF_KVS:OFF_KVW].reshape(bs, 2 * N_KV, 1, HEAD_DIM)
    kvw_new = zs[:, OFF_KVW:MAIN_W].reshape(bs, 1, KV_W)

    kvc = _cmp_pages(cache_cmp_kv.reshape(n_pool, PAGE_SIZE, KV_W), page_table, wrow)
    oc_s, idx = _cmp_sample(rel_bias, q8, kvc, kvc_new, wrow, gates8, past_len)
    cache_half = cache_sel_kv.reshape(n_pool * (PAGE_SIZE // CMP_BLOCK), CMP_BLOCK, KV_W)
    os_s = _sel_sample(page_table, idx, rel_bias, q8, cache_half, kvs_new, gates8, past_len)
    win = cache_win_kv[0].reshape(bs, wc, KV_W)
    yb8 = _win_sample(rel_bias, q8, win, kvw_new, gates8, oc_s, os_s)
    yb_s = yb8[:, :, :HPG].reshape(bs, Q_W).astype(BF16)
    y_sample = _out_and_ffn(xs, ya_s, yb_s, gab_s, *tail_w, tm=bs, tm_norm=bs)

    row_shape = (1, bs, 1, 2, N_KV, HEAD_DIM)
    cmp_s = kvc_new.reshape(row_shape)
    sel_s = zs[:, OFF_KVS:OFF_KVW].reshape(row_shape)
    win_s = jnp.concatenate([win, kvw_new], axis=1)[:, 1:].reshape(1, bs, wc, 2, N_KV, HEAD_DIM)
    conv_s = jnp.concatenate([state_conv[0], zs[:, None, OFF_XR:OFF_GR]], axis=1)[:, 1:][None]

    return (y_prompt.reshape(bp, tp, D_MODEL), y_sample.reshape(bs, ts, D_MODEL),
            cmp_p, sel_p, win_p, h_p.reshape(1, bp, D_RNN), conv_p[None],
            cmp_s, sel_s, win_s, h_s[None], conv_s)
```

```python
import functools
import math

import jax
import jax.numpy as jnp
from jax import lax
from jax.experimental import pallas as pl
from jax.experimental.pallas import tpu as pltpu

F32 = jnp.float32
BF16 = jnp.bfloat16
I32 = jnp.int32

D_MODEL = 4096
D_RNN = 2048
N_RNN_BLOCKS = 16
RNN_BLOCK = D_RNN // N_RNN_BLOCKS
CONV_W = 4
LRU_C = 8.0
N_HEADS = 16
HEAD_DIM = 128
N_KV = 4
HPG = N_HEADS // N_KV
CMP_BLOCK = 64
N_SEL = 16
WINDOW = 512
N_BUCKETS = 32
MAX_DISTANCE = 128
PAGE_SIZE = 128
Q_W = N_HEADS * HEAD_DIM
KV_W = 2 * N_KV * HEAD_DIM
EPS = 1e-6
NEG_INF = -1e30
FORCE = 1e9
SCALE = HEAD_DIM ** -0.5

OFF_XR = 0
OFF_GR = D_RNN
OFF_Q = 2 * D_RNN
OFF_KVC = OFF_Q + Q_W
OFF_KVS = OFF_KVC + KV_W
OFF_KVW = OFF_KVS + KV_W
OFF_GN = OFF_KVW + KV_W
OFF_GA = OFF_GN + 3 * N_HEADS
MAIN_W = OFF_GN
IN_TN = 256
Z_GN = MAIN_W
Z_GA = MAIN_W + IN_TN
Z_GB = Z_GA + D_MODEL

ATT_T = 256
HROWS = 8
KV_ROWS = 2 * N_KV
PAGES_PER_STEP = 8
VMEM_LIMIT = 56 * 1024 * 1024


def _cparams(n_axes):
    return pltpu.CompilerParams(dimension_semantics=("arbitrary",) * n_axes,
                                vmem_limit_bytes=VMEM_LIMIT)


def _sigmoid(x):
    return 1.0 / (1.0 + jnp.exp(-x))


def _gelu_tanh(x):
    return 0.5 * x * (1.0 + jnp.tanh(math.sqrt(2.0 / math.pi) * (x + 0.044715 * (x * x * x))))


def _dot(a, b):
    return jnp.dot(a, b, preferred_element_type=F32)


def _dot_nt(a, b):
    return lax.dot_general(a, b, (((1,), (1,)), ((), ())), preferred_element_type=F32)


def _bucket(dist):
    n = jnp.maximum(dist, 0)
    exact = N_BUCKETS // 2
    nf = jnp.maximum(n, 1).astype(F32)
    large = exact + (jnp.log(nf / exact) / math.log(MAX_DISTANCE / exact)
                     * (N_BUCKETS - exact)).astype(I32)
    return jnp.where(n < exact, n, jnp.minimum(large, N_BUCKETS - 1))


def _bias_lookup(bkt, tab_ref, cols):
    outs = [jnp.zeros(bkt.shape, F32) for _ in cols]
    for b in range(N_BUCKETS):
        eq = bkt == b
        outs = [jnp.where(eq, tab_ref[b, c], o) for c, o in zip(cols, outs)]
    return outs


def _block_scores(imp, blk, cur):
    score = jnp.where(blk == 0, FORCE, jnp.where(blk == cur, FORCE,
                                                 jnp.where(blk == cur - 1, FORCE, imp)))
    return jnp.where(blk <= cur, score, -FORCE)


def _ahead(col, score, later):
    return jnp.where(later, jnp.where(col >= score, 1.0, 0.0), jnp.where(col > score, 1.0, 0.0))


def _masked_softmax(s, valid):
    s = jnp.where(valid, s, NEG_INF)
    m = jnp.max(s, axis=-1, keepdims=True)
    e = jnp.where(valid, jnp.exp(s - m), 0.0)
    return e / jnp.maximum(jnp.sum(e, axis=-1, keepdims=True), 1e-30)


def _rmsnorm_kernel(x_ref, g_ref, o_ref):
    x = x_ref[...]
    y = x * lax.rsqrt(jnp.mean(x * x, axis=-1, keepdims=True) + EPS)
    o_ref[...] = (y * g_ref[...]).astype(o_ref.dtype)


def _rmsnorm(x, g, out_dtype, tm):
    m, d = x.shape
    return pl.pallas_call(
        _rmsnorm_kernel,
        out_shape=jax.ShapeDtypeStruct((m, d), out_dtype),
        grid=(m // tm,),
        in_specs=[pl.BlockSpec((tm, d), lambda i: (i, 0)),
                  pl.BlockSpec((1, d), lambda i: (0, 0))],
        out_specs=pl.BlockSpec((tm, d), lambda i: (i, 0)),
        compiler_params=_cparams(1),
        name="rmsnorm",
    )(x, g.reshape(1, d))


def _mm_kernel(a_ref, w_ref, o_ref):
    o_ref[...] = _dot(a_ref[...], w_ref[...].astype(BF16)).astype(o_ref.dtype)


def _mm_res_kernel(a_ref, w_ref, r_ref, o_ref):
    o_ref[...] = r_ref[...] + _dot(a_ref[...], w_ref[...].astype(BF16))


def _mm(a, w, *, n_cols, tm, tn, out_dtype=F32, res=None, k_block=0, tk=None, name="mm"):
    m = a.shape[0]
    tk = a.shape[1] if tk is None else tk
    grid = (m // tm, n_cols // tn)
    in_specs = [pl.BlockSpec((tm, tk), lambda i, j: (i, k_block)),
                pl.BlockSpec((tk, tn), lambda i, j: (k_block, j))]
    args = [a, w]
    kern = _mm_kernel
    if res is not None:
        in_specs.append(pl.BlockSpec((tm, tn), lambda i, j: (i, j)))
        args.append(res)
        kern = _mm_res_kernel
    return pl.pallas_call(
        kern,
        out_shape=jax.ShapeDtypeStruct((m, n_cols), out_dtype),
        grid=grid,
        in_specs=in_specs,
        out_specs=pl.BlockSpec((tm, tn), lambda i, j: (i, j)),
        compiler_params=_cparams(2),
        name=name,
    )(*args)


def _in_proj_kernel(a_ref, wt_ref, o_ref):
    o_ref[...] = _dot_nt(a_ref[...], wt_ref[...].astype(BF16))


def _in_proj_all(a, wt, *, tm, tn):
    m, k = a.shape
    assert MAIN_W % tn == 0 and Z_GN == MAIN_W and Z_GA == MAIN_W + tn and 3 * N_HEADS <= tn
    n_gate_blocks = 2 * D_MODEL // tn
    nb = MAIN_W // tn + 1 + n_gate_blocks
    assert OFF_GA + n_gate_blocks * tn == wt.shape[0]

    def w_index(i, j):
        row = jnp.where(j <= MAIN_W // tn, j * tn, OFF_GA + (j - MAIN_W // tn - 1) * tn)
        return (pl.multiple_of(row, 8), 0)

    return pl.pallas_call(
        _in_proj_kernel,
        out_shape=jax.ShapeDtypeStruct((m, nb * tn), F32),
        grid=(m // tm, nb),
        in_specs=[pl.BlockSpec((tm, k), lambda i, j: (i, 0)),
                  pl.BlockSpec((pl.Element(tn), pl.Element(k)), w_index)],
        out_specs=pl.BlockSpec((tm, tn), lambda i, j: (i, j)),
        compiler_params=_cparams(2),
        name="in_proj",
    )(a, wt)


def _swiglu_kernel(a_ref, wg_ref, wu_ref, o_ref):
    a = a_ref[...]
    g = _dot(a, wg_ref[...].astype(BF16))
    u = _dot(a, wu_ref[...].astype(BF16))
    o_ref[...] = ((g * _sigmoid(g)) * u).astype(o_ref.dtype)


def _swiglu(a, wg, wu, *, tm, tn):
    m, k = a.shape
    n = wg.shape[1]
    return pl.pallas_call(
        _swiglu_kernel,
        out_shape=jax.ShapeDtypeStruct((m, n), BF16),
        grid=(m // tm, n // tn),
        in_specs=[pl.BlockSpec((tm, k), lambda i, j: (i, 0)),
                  pl.BlockSpec((k, tn), lambda i, j: (0, j)),
                  pl.BlockSpec((k, tn), lambda i, j: (0, j))],
        out_specs=pl.BlockSpec((tm, tn), lambda i, j: (i, j)),
        compiler_params=_cparams(2),
        name="ffn_gate_up",
    )(a, wg, wu)


def _mix_kernel(ya_ref, yb_ref, wa_ref, wb_ref, ga_ref, gb_ref, o_ref):
    pa = _dot(ya_ref[...], wa_ref[...].astype(BF16))
    pb = _dot(yb_ref[...], wb_ref[...].astype(BF16))
    o_ref[...] = (_sigmoid(ga_ref[...]) * pa + _sigmoid(gb_ref[...]) * pb).astype(o_ref.dtype)


def _mix(ya, yb, wa, wb, z, *, tm, tn):
    m, k = ya.shape
    n = wa.shape[1]
    assert Z_GA % tn == 0 and Z_GB % tn == 0
    gate = lambda off: pl.BlockSpec((tm, tn), lambda i, j: (i, j + off // tn))
    return pl.pallas_call(
        _mix_kernel,
        out_shape=jax.ShapeDtypeStruct((m, n), BF16),
        grid=(m // tm, n // tn),
        in_specs=[pl.BlockSpec((tm, k), lambda i, j: (i, 0)),
                  pl.BlockSpec((tm, k), lambda i, j: (i, 0)),
                  pl.BlockSpec((k, tn), lambda i, j: (0, j)),
                  pl.BlockSpec((k, tn), lambda i, j: (0, j)),
                  gate(Z_GA), gate(Z_GB)],
        out_specs=pl.BlockSpec((tm, tn), lambda i, j: (i, j)),
        compiler_params=_cparams(2),
        name="branch_merge",
    )(ya, yb, wa, wb, z, z)


def _lru_gates(xc, wa_ref, ba_ref, wi_ref, bi_ref, lam_ref, a_ref, u_ref):
    for n in range(N_RNN_BLOCKS):
        sl = slice(n * RNN_BLOCK, (n + 1) * RNN_BLOCK)
        xn = xc[:, sl]
        xb = xn.astype(BF16)
        r = _sigmoid(_dot(xb, wa_ref[n].astype(BF16)) + ba_ref[:, sl])
        i = _sigmoid(_dot(xb, wi_ref[n].astype(BF16)) + bi_ref[:, sl])
        neg_lam = -lam_ref[:, sl]
        softplus = jnp.maximum(neg_lam, 0.0) + jnp.log1p(jnp.exp(-jnp.abs(neg_lam)))
        log_a = (-LRU_C * r) * softplus
        a = jnp.exp(log_a)
        a_ref[:, sl] = a
        u_ref[:, sl] = jnp.sqrt(-jnp.tanh(log_a) * (a * a + 1.0)) * (i * xn)


def _rglru_prompt_kernel(xr_ref, gr_ref, cw_ref, cb_ref, wa_ref, ba_ref, wi_ref, bi_ref, lam_ref,
                         ya_ref, conv_ref, hl_ref, xp_ref, a_ref, u_ref, h_ref, carry_ref):
    ti = pl.program_id(1)
    tt, c = xr_ref.shape

    @pl.when(ti == 0)
    def _():
        xp_ref[0:8, :] = jnp.zeros((8, c), F32)
        carry_ref[...] = jnp.zeros((1, c), F32)

    x = xr_ref[...]
    xp_ref[8:8 + tt, :] = x
    xc = cb_ref[...]
    for k in range(CONV_W):
        xc = xc + cw_ref[k:k + 1, :] * xp_ref[5 + k:5 + k + tt, :]
    xp_ref[0:8, :] = xp_ref[tt:tt + 8, :]

    _lru_gates(xc, wa_ref, ba_ref, wi_ref, bi_ref, lam_ref, a_ref, u_ref)

    row = lax.broadcasted_iota(I32, (8, c), 0)

    def chunk(ci, carry):
        off = pl.multiple_of(ci * 8, 8)
        a = a_ref[pl.ds(off, 8), :]
        b = u_ref[pl.ds(off, 8), :]
        for s in (1, 2, 4):
            keep = row >= s
            b = jnp.where(keep, a * pltpu.roll(b, s, 0) + b, b)
            a = jnp.where(keep, a * pltpu.roll(a, s, 0), a)
        h = a * carry + b
        h_ref[pl.ds(off, 8), :] = h
        return h[7:8, :]

    carry = lax.fori_loop(0, tt // 8, chunk, carry_ref[...])
    carry_ref[...] = carry
    ya_ref[...] = (h_ref[...] * _gelu_tanh(gr_ref[...])).astype(ya_ref.dtype)

    @pl.when(ti == pl.num_programs(1) - 1)
    def _():
        conv_ref[0] = x[tt - (CONV_W - 1):, :]
        hl_ref[0] = carry


def _rglru_prompt(z, batch, seq, conv_w, conv_b, lru_wa, lru_ba, lru_wi, lru_bi, lam, *, tt):
    nt = seq // tt
    c = D_RNN
    row = lambda b, t: (b * nt + t, 0)
    full2 = lambda b, t: (0, 0)
    full3 = lambda b, t: (0, 0, 0)
    return pl.pallas_call(
        _rglru_prompt_kernel,
        out_shape=(jax.ShapeDtypeStruct((batch * seq, c), BF16),
                   jax.ShapeDtypeStruct((batch, CONV_W - 1, c), F32),
                   jax.ShapeDtypeStruct((batch, 1, c), F32)),
        grid=(batch, nt),
        in_specs=[pl.BlockSpec((tt, c), row),
                  pl.BlockSpec((tt, c), lambda b, t: (b * nt + t, 1)),
                  pl.BlockSpec((CONV_W, c), full2),
                  pl.BlockSpec((1, c), full2),
                  pl.BlockSpec((N_RNN_BLOCKS, RNN_BLOCK, RNN_BLOCK), full3),
                  pl.BlockSpec((1, c), full2),
                  pl.BlockSpec((N_RNN_BLOCKS, RNN_BLOCK, RNN_BLOCK), full3),
                  pl.BlockSpec((1, c), full2),
                  pl.BlockSpec((1, c), full2)],
        out_specs=(pl.BlockSpec((tt, c), row),
                   pl.BlockSpec((1, CONV_W - 1, c), lambda b, t: (b, 0, 0)),
                   pl.BlockSpec((1, 1, c), lambda b, t: (b, 0, 0))),
        scratch_shapes=[pltpu.VMEM((tt + 8, c), F32), pltpu.VMEM((tt, c), F32),
                        pltpu.VMEM((tt, c), F32), pltpu.VMEM((tt, c), F32),
                        pltpu.VMEM((1, c), F32)],
        compiler_params=_cparams(2),
        name="rglru_prompt",
    )(z, z, conv_w, conv_b.reshape(1, c), lru_wa, lru_ba.reshape(1, c), lru_wi,
      lru_bi.reshape(1, c), lam.reshape(1, c))


def _rglru_step_kernel(xr_ref, gr_ref, buf_ref, h0_ref, cw_ref, cb_ref, wa_ref, ba_ref, wi_ref,
                       bi_ref, lam_ref, ya_ref, hn_ref, a_ref, u_ref):
    x = xr_ref[...]
    xc = cb_ref[...]
    for k in range(CONV_W - 1):
        xc = xc + cw_ref[k:k + 1, :] * buf_ref[k]
    xc = xc + cw_ref[CONV_W - 1:CONV_W, :] * x
    _lru_gates(xc, wa_ref, ba_ref, wi_ref, bi_ref, lam_ref, a_ref, u_ref)
    h = a_ref[...] * h0_ref[...] + u_ref[...]
    hn_ref[...] = h
    ya_ref[...] = (h * _gelu_tanh(gr_ref[...])).astype(ya_ref.dtype)


def _rglru_step(z, buf_t, h0, conv_w, conv_b, lru_wa, lru_ba, lru_wi, lru_bi, lam):
    b, c = h0.shape
    full2 = lambda i: (0, 0)
    full3 = lambda i: (0, 0, 0)
    return pl.pallas_call(
        _rglru_step_kernel,
        out_shape=(jax.ShapeDtypeStruct((b, c), BF16), jax.ShapeDtypeStruct((b, c), F32)),
        grid=(1,),
        in_specs=[pl.BlockSpec((b, c), full2),
                  pl.BlockSpec((b, c), lambda i: (0, 1)),
                  pl.BlockSpec((CONV_W - 1, b, c), full3),
                  pl.BlockSpec((b, c), full2),
                  pl.BlockSpec((CONV_W, c), full2),
                  pl.BlockSpec((1, c), full2),
                  pl.BlockSpec((N_RNN_BLOCKS, RNN_BLOCK, RNN_BLOCK), full3),
                  pl.BlockSpec((1, c), full2),
                  pl.BlockSpec((N_RNN_BLOCKS, RNN_BLOCK, RNN_BLOCK), full3),
                  pl.BlockSpec((1, c), full2),
                  pl.BlockSpec((1, c), full2)],
        out_specs=(pl.BlockSpec((b, c), full2), pl.BlockSpec((b, c), full2)),
        scratch_shapes=[pltpu.VMEM((b, c), F32), pltpu.VMEM((b, c), F32)],
        compiler_params=_cparams(1),
        name="rglru_step",
    )(z, z, buf_t, h0, conv_w, conv_b.reshape(1, c), lru_wa, lru_ba.reshape(1, c), lru_wi,
      lru_bi.reshape(1, c), lam.reshape(1, c))


def _bias_tiles_kernel(tab_ref, bd_ref, bo_ref):
    h = pl.program_id(0)
    t = bd_ref.shape[1]
    d0 = lax.broadcasted_iota(I32, (t, t), 0) - lax.broadcasted_iota(I32, (t, t), 1)
    bd_ref[0] = _bias_lookup(_bucket(d0), tab_ref, [h])[0]
    bo_ref[0] = _bias_lookup(_bucket(d0 + t), tab_ref, [h])[0]


def _bias_tiles(rel_bias, t):
    shp = jax.ShapeDtypeStruct((N_HEADS, t, t), F32)
    spec = pl.BlockSpec((1, t, t), lambda h: (h, 0, 0))
    return pl.pallas_call(
        _bias_tiles_kernel,
        out_shape=(shp, shp),
        grid=(N_HEADS,),
        in_specs=[pl.BlockSpec(memory_space=pltpu.SMEM)],
        out_specs=(spec, spec),
        compiler_params=_cparams(1),
        name="bias_tiles",
    )(rel_bias)


def _cmp_prompt_kernel(tab_ref, q_ref, kc_ref, vc_ref, wk_ref, wv_ref, gate_ref,
                       oc_ref, sel_ref, kcmp_ref, vcmp_ref):
    g = pl.program_id(1)
    qi = pl.program_id(2)
    tq = q_ref.shape[0]
    seq = kc_ref.shape[0]
    nblk = seq // CMP_BLOCK

    @pl.when(qi == 0)
    def _():
        k3 = kc_ref[...].reshape(nblk, CMP_BLOCK, HEAD_DIM)
        v3 = vc_ref[...].reshape(nblk, CMP_BLOCK, HEAD_DIM)
        kcmp_ref[...] = jnp.sum(k3 * wk_ref[0, 0][None], axis=1)
        vcmp_ref[...] = jnp.sum(v3 * wv_ref[0, 0][None], axis=1)

    tpos = qi * tq + lax.broadcasted_iota(I32, (tq, nblk), 0)
    blk = lax.broadcasted_iota(I32, (tq, nblk), 1)
    dist = tpos - (blk * CMP_BLOCK + (CMP_BLOCK - 1))
    valid = dist >= 0
    biases = _bias_lookup(_bucket(dist), tab_ref, [g * HPG + h for h in range(HPG)])
    kb = kcmp_ref[...].astype(BF16)
    vb = vcmp_ref[...].astype(BF16)
    gates = _sigmoid(gate_ref[0, 0])
    imp = jnp.zeros((tq, nblk), F32)
    for h in range(HPG):
        sl = slice(h * HEAD_DIM, (h + 1) * HEAD_DIM)
        s = _dot_nt(q_ref[:, sl].astype(BF16), kb) * SCALE + biases[h]
        p = _masked_softmax(s, valid)
        imp = imp + p
        oc_ref[:, sl] = gates[:, h:h + 1] * _dot(p.astype(BF16), vb)

    cur = jnp.right_shift(tpos, 6)
    score = _block_scores(imp, blk, cur)
    rank = jnp.zeros((tq, nblk), F32)
    for j in range(nblk):
        col = score[:, j:j + 1]
        rank = rank + _ahead(col, score, blk > j)
    sel_ref[0, 0] = jnp.where(rank < float(min(N_SEL, nblk)), 1.0, 0.0).astype(sel_ref.dtype)


def _cmp_prompt(z, rel_bias, wexp, gn_t, batch, seq):
    tq = ATT_T
    nq = seq // tq
    nblk = seq // CMP_BLOCK
    m = batch * seq
    qcol = OFF_Q // (HPG * HEAD_DIM)
    kcol = OFF_KVC // HEAD_DIM
    return pl.pallas_call(
        _cmp_prompt_kernel,
        out_shape=(jax.ShapeDtypeStruct((m, Q_W), F32),
                   jax.ShapeDtypeStruct((batch, N_KV, seq, nblk), BF16)),
        grid=(batch, N_KV, nq),
        in_specs=[pl.BlockSpec(memory_space=pltpu.SMEM),
                  pl.BlockSpec((tq, HPG * HEAD_DIM), lambda b, g, i: (b * nq + i, qcol + g)),
                  pl.BlockSpec((seq, HEAD_DIM), lambda b, g, i: (b, kcol + g)),
                  pl.BlockSpec((seq, HEAD_DIM), lambda b, g, i: (b, kcol + N_KV + g)),
                  pl.BlockSpec((1, 1, CMP_BLOCK, HEAD_DIM), lambda b, g, i: (0, g, 0, 0)),
                  pl.BlockSpec((1, 1, CMP_BLOCK, HEAD_DIM), lambda b, g, i: (1, g, 0, 0)),
                  pl.BlockSpec((1, 1, tq, HPG), lambda b, g, i: (0, g, b * nq + i, 0))],
        out_specs=(pl.BlockSpec((tq, HPG * HEAD_DIM), lambda b, g, i: (b * nq + i, g)),
                   pl.BlockSpec((1, 1, tq, nblk), lambda b, g, i: (b, g, i, 0))),
        scratch_shapes=[pltpu.VMEM((nblk, HEAD_DIM), F32), pltpu.VMEM((nblk, HEAD_DIM), F32)],
        compiler_params=_cparams(3),
        name="cmp_prompt",
    )(rel_bias, z, z, z, wexp, wexp, gn_t)


def _attn_init(m_ref, l_ref, acc_ref):
    m_ref[...] = jnp.full(m_ref.shape, NEG_INF, F32)
    l_ref[...] = jnp.zeros(l_ref.shape, F32)
    acc_ref[...] = jnp.zeros(acc_ref.shape, F32)


def _attn_update(h, qh, k, v, bias, valid, m_ref, l_ref, acc_ref):
    s = _dot_nt(qh, k) * SCALE + bias
    if valid is not None:
        s = jnp.where(valid, s, NEG_INF)
    m_old = m_ref[h]
    m_new = jnp.maximum(m_old, jnp.max(s, axis=-1, keepdims=True))
    alpha = jnp.exp(m_old - m_new)
    e = jnp.exp(s - m_new)
    if valid is not None:
        e = jnp.where(valid, e, 0.0)
    l_ref[h] = alpha * l_ref[h] + jnp.sum(e, axis=-1, keepdims=True)
    acc_ref[h] = alpha * acc_ref[h] + _dot(e.astype(BF16), v)
    m_ref[h] = m_new


def _load_kv_tiles(qi, k_ref, v_ref, kb_ref, vb_ref):
    @pl.when(qi == 0)
    def _():
        kb_ref[...] = k_ref[...].reshape(kb_ref.shape).astype(BF16)
        vb_ref[...] = v_ref[...].reshape(vb_ref.shape).astype(BF16)


def _sel_prompt_kernel(tab_ref, q_ref, k_ref, v_ref, bd_ref, bo_ref, gate_ref, sel_ref, ex_ref,
                       os_ref, kb_ref, vb_ref, qb_ref, m_ref, l_ref, acc_ref):
    g = pl.program_id(1)
    qi = pl.program_id(2)
    t = q_ref.shape[0]
    _load_kv_tiles(qi, k_ref, v_ref, kb_ref, vb_ref)
    for h in range(HPG):
        qb_ref[h] = q_ref[:, h * HEAD_DIM:(h + 1) * HEAD_DIM].astype(BF16)
    _attn_init(m_ref, l_ref, acc_ref)
    sel = sel_ref[0, 0]
    causal = lax.broadcasted_iota(I32, (t, t), 0) >= lax.broadcasted_iota(I32, (t, t), 1)

    def tile(kj, kind):
        self = _dot(sel, ex_ref[kj])
        if kind == "diag":
            self = jnp.where(causal, self, 0.0)
        valid = self > 0.5
        k = kb_ref[kj]
        v = vb_ref[kj]
        for h in range(HPG):
            if kind == "far":
                bias = tab_ref[N_BUCKETS - 1, g * HPG + h]
            elif kind == "off":
                bias = bo_ref[h]
            else:
                bias = bd_ref[h]
            _attn_update(h, qb_ref[h], k, v, bias, valid, m_ref, l_ref, acc_ref)

    def far_body(kj, carry):
        tile(kj, "far")
        return carry

    lax.fori_loop(0, jnp.maximum(qi - 1, 0), far_body, 0)

    @pl.when(qi >= 1)
    def _():
        tile(qi - 1, "off")

    tile(qi, "diag")
    gates = _sigmoid(gate_ref[0, 0])
    for h in range(HPG):
        o = acc_ref[h] / jnp.maximum(l_ref[h], 1e-30)
        os_ref[:, h * HEAD_DIM:(h + 1) * HEAD_DIM] = gates[:, h:h + 1] * o


def _win_prompt_kernel(tab_ref, q_ref, k_ref, v_ref, bd_ref, bo_ref, gate_ref, oc_ref, os_ref,
                       yb_ref, kb_ref, vb_ref, qb_ref, m_ref, l_ref, acc_ref):
    g = pl.program_id(1)
    qi = pl.program_id(2)
    t = q_ref.shape[0]
    _load_kv_tiles(qi, k_ref, v_ref, kb_ref, vb_ref)
    for h in range(HPG):
        qb_ref[h] = q_ref[:, h * HEAD_DIM:(h + 1) * HEAD_DIM].astype(BF16)
    _attn_init(m_ref, l_ref, acc_ref)
    row = lax.broadcasted_iota(I32, (t, t), 0)
    col = lax.broadcasted_iota(I32, (t, t), 1)

    def tile(kj, kind):
        k = kb_ref[kj]
        v = vb_ref[kj]
        for h in range(HPG):
            if kind == "far":
                bias, valid = tab_ref[N_BUCKETS - 1, g * HPG + h], row < col
            elif kind == "off":
                bias, valid = bo_ref[h], None
            else:
                bias, valid = bd_ref[h], row >= col
            _attn_update(h, qb_ref[h], k, v, bias, valid, m_ref, l_ref, acc_ref)

    @pl.when(qi >= 2)
    def _():
        tile(qi - 2, "far")

    @pl.when(qi >= 1)
    def _():
        tile(qi - 1, "off")

    tile(qi, "diag")
    gates = _sigmoid(gate_ref[0, 0])
    for h in range(HPG):
        sl = slice(h * HEAD_DIM, (h + 1) * HEAD_DIM)
        o = acc_ref[h] / jnp.maximum(l_ref[h], 1e-30)
        yb_ref[:, sl] = (oc_ref[:, sl] + os_ref[:, sl] + gates[:, h:h + 1] * o).astype(yb_ref.dtype)


def _attn_prompt_specs(batch, seq, kv_off, gate_idx):
    t = ATT_T
    nq = seq // t
    qcol = OFF_Q // (HPG * HEAD_DIM)
    kcol = kv_off // HEAD_DIM
    rowblk = lambda b, g, i: (b * nq + i, g)
    specs = [pl.BlockSpec(memory_space=pltpu.SMEM),
             pl.BlockSpec((t, HPG * HEAD_DIM), lambda b, g, i: (b * nq + i, qcol + g)),
             pl.BlockSpec((seq, HEAD_DIM), lambda b, g, i: (b, kcol + g)),
             pl.BlockSpec((seq, HEAD_DIM), lambda b, g, i: (b, kcol + N_KV + g)),
             pl.BlockSpec((HPG, t, t), lambda b, g, i: (g, 0, 0)),
             pl.BlockSpec((HPG, t, t), lambda b, g, i: (g, 0, 0)),
             pl.BlockSpec((1, 1, t, HPG), lambda b, g, i: (gate_idx, g, b * nq + i, 0))]
    scratch = [pltpu.VMEM((seq // t, t, HEAD_DIM), BF16), pltpu.VMEM((seq // t, t, HEAD_DIM), BF16),
               pltpu.VMEM((HPG, t, HEAD_DIM), BF16),
               pltpu.VMEM((HPG, t, 1), F32), pltpu.VMEM((HPG, t, 1), F32),
               pltpu.VMEM((HPG, t, HEAD_DIM), F32)]
    return specs, scratch, pl.BlockSpec((t, HPG * HEAD_DIM), rowblk), (batch, N_KV, nq)


def _sel_prompt(z, rel_bias, bd, bo, gn_t, sel, expand, batch, seq):
    specs, scratch, out_spec, grid = _attn_prompt_specs(batch, seq, OFF_KVS, 1)
    t = ATT_T
    nq = seq // t
    nblk = seq // CMP_BLOCK
    specs += [pl.BlockSpec((1, 1, t, nblk), lambda b, g, i: (b, g, i, 0)),
              pl.BlockSpec((nq, nblk, t), lambda b, g, i: (0, 0, 0))]
    return pl.pallas_call(
        _sel_prompt_kernel,
        out_shape=jax.ShapeDtypeStruct((batch * seq, Q_W), F32),
        grid=grid, in_specs=specs, out_specs=out_spec, scratch_shapes=scratch,
        compiler_params=_cparams(3), name="sel_prompt",
    )(rel_bias, z, z, z, bd, bo, gn_t, sel, expand)


def _win_prompt(z, rel_bias, bd, bo, gn_t, oc, osel, batch, seq):
    specs, scratch, out_spec, grid = _attn_prompt_specs(batch, seq, OFF_KVW, 2)
    specs += [out_spec, out_spec]
    return pl.pallas_call(
        _win_prompt_kernel,
        out_shape=jax.ShapeDtypeStruct((batch * seq, Q_W), BF16),
        grid=grid, in_specs=specs, out_specs=out_spec, scratch_shapes=scratch,
        compiler_params=_cparams(3), name="win_prompt",
    )(rel_bias, z, z, z, bd, bo, gn_t, oc, osel)


def _cmp_pages_kernel(pt_ref, *refs):
    page_refs, w_ref, o_ref = refs[:-2], refs[-2], refs[-1]
    per_page = PAGE_SIZE // CMP_BLOCK
    w = w_ref[...]
    for k, page_ref in enumerate(page_refs):
        x = page_ref[0].reshape(per_page, CMP_BLOCK, KV_ROWS, HEAD_DIM)
        o_ref[0, per_page * k:per_page * (k + 1)] = jnp.sum(x * w[None], axis=1)


def _cmp_pages(cache, page_table, w8):
    b, n_pages = page_table.shape
    per_page = PAGE_SIZE // CMP_BLOCK
    pps = PAGES_PER_STEP
    page = lambda k: pl.BlockSpec((1, PAGE_SIZE, KV_ROWS, HEAD_DIM),
                                  lambda i, p, pt: (pt[i, p * pps + k], 0, 0, 0))
    grid_spec = pltpu.PrefetchScalarGridSpec(
        num_scalar_prefetch=1,
        grid=(b, n_pages // pps),
        in_specs=[page(k) for k in range(pps)]
        + [pl.BlockSpec((CMP_BLOCK, KV_ROWS, HEAD_DIM), lambda i, p, pt: (0, 0, 0))],
        out_specs=pl.BlockSpec((1, pps * per_page, KV_ROWS, HEAD_DIM), lambda i, p, pt: (i, p, 0, 0)),
    )
    return pl.pallas_call(
        _cmp_pages_kernel,
        out_shape=jax.ShapeDtypeStruct((b, n_pages * per_page, KV_ROWS, HEAD_DIM), F32),
        grid_spec=grid_spec,
        compiler_params=_cparams(2),
        name="cmp_pages",
    )(page_table, *([cache] * pps), w8)


def _cmp_sample_kernel(tab_ref, q_ref, kvc_ref, new_ref, w_ref, gate_ref,
                       oc_ref, idx_ref, kall_ref, vall_ref, imp_ref, *, q_pos):
    nblk = kvc_ref.shape[1]
    nk = kall_ref.shape[0]
    blk = lax.broadcasted_iota(I32, (1, nk), 1)
    dist = q_pos - (blk * CMP_BLOCK + (CMP_BLOCK - 1))
    valid = dist >= 0
    bkt = _bucket(dist)
    new_c = new_ref[0] * w_ref[0]
    kall_ref[...] = jnp.zeros(kall_ref.shape, F32)
    vall_ref[...] = jnp.zeros(vall_ref.shape, F32)
    for g in range(N_KV):
        kall_ref[0:nblk, :] = kvc_ref[0, :, g, :]
        vall_ref[0:nblk, :] = kvc_ref[0, :, N_KV + g, :]
        kall_ref[nblk:nblk + 1, :] = new_c[g:g + 1, :]
        vall_ref[nblk:nblk + 1, :] = new_c[N_KV + g:N_KV + g + 1, :]
        kb = kall_ref[...].astype(BF16)
        vb = vall_ref[...].astype(BF16)
        biases = _bias_lookup(bkt, tab_ref, [g * HPG + h for h in range(HPG)])
        rows = lax.broadcasted_iota(I32, (HROWS, nk), 0)
        bias = jnp.zeros((HROWS, nk), F32)
        for h in range(HPG):
            bias = jnp.where(rows == h, biases[h], bias)
        s = _dot_nt(q_ref[0, g].astype(BF16), kb) * SCALE + bias
        p = _masked_softmax(s, valid)
        gate = _sigmoid(gate_ref[0, 0, g])
        oc_ref[0, g] = gate * _dot(p.astype(BF16), vb)
        head_row = jnp.where(rows < HPG, p, 0.0)
        imp_ref[g:g + 1, :] = jnp.sum(head_row, axis=0, keepdims=True)

    imp = imp_ref[0:N_KV, :]
    blk4 = lax.broadcasted_iota(I32, (N_KV, nk), 1)
    cur = q_pos // CMP_BLOCK
    score = _block_scores(imp, blk4, cur)
    rank = jnp.zeros((N_KV, nk), F32)
    for j in range(nblk + 1):
        col = score[:, j:j + 1]
        rank = rank + _ahead(col, score, blk4 > j)
    rank = jnp.where(blk4 <= nblk, rank, float(nk))
    blkf = blk4.astype(F32)
    slot = lax.broadcasted_iota(I32, (N_KV, N_SEL), 1)
    idx = jnp.zeros((N_KV, N_SEL), F32)
    for r in range(N_SEL):
        pick = jnp.sum(jnp.where(rank == float(r), blkf, 0.0), axis=-1, keepdims=True)
        idx = jnp.where(slot == r, pick, idx)
    idx_ref[0] = idx.astype(I32)


def _cmp_sample(rel_bias, q8, kvc, kvc_new, w8, gates8, q_pos):
    b, nblk = kvc.shape[:2]
    nk = 2 * HEAD_DIM
    assert nblk + 1 <= nk and q_pos // CMP_BLOCK == nblk
    return pl.pallas_call(
        functools.partial(_cmp_sample_kernel, q_pos=q_pos),
        out_shape=(jax.ShapeDtypeStruct((b, N_KV, HROWS, HEAD_DIM), F32),
                   jax.ShapeDtypeStruct((b, N_KV, N_SEL), I32)),
        grid=(b,),
        in_specs=[pl.BlockSpec(memory_space=pltpu.SMEM),
                  pl.BlockSpec((1, N_KV, HROWS, HEAD_DIM), lambda i: (i, 0, 0, 0)),
                  pl.BlockSpec((1, nblk, KV_ROWS, HEAD_DIM), lambda i: (i, 0, 0, 0)),
                  pl.BlockSpec((1, KV_ROWS, HEAD_DIM), lambda i: (i, 0, 0)),
                  pl.BlockSpec((CMP_BLOCK, KV_ROWS, HEAD_DIM), lambda i: (0, 0, 0)),
                  pl.BlockSpec((1, 1, N_KV, HROWS, 1), lambda i: (i, 0, 0, 0, 0))],
        out_specs=(pl.BlockSpec((1, N_KV, HROWS, HEAD_DIM), lambda i: (i, 0, 0, 0)),
                   pl.BlockSpec((1, N_KV, N_SEL), lambda i: (i, 0, 0))),
        scratch_shapes=[pltpu.VMEM((nk, HEAD_DIM), F32), pltpu.VMEM((nk, HEAD_DIM), F32),
                        pltpu.VMEM((HROWS, nk), F32)],
        compiler_params=_cparams(1),
        name="cmp_sample",
    )(rel_bias, q8, kvc, kvc_new, w8, gates8)


def _sel_sample_kernel(pt_ref, idx_ref, tab_ref, q_ref, *refs, q_pos, n_past):
    blk_refs = refs[:N_KV]
    new_ref, gate_ref, os_ref, m_ref, l_ref, acc_ref = refs[N_KV:]
    b = pl.program_id(0)
    j = pl.program_id(1)

    @pl.when(j == 0)
    def _():
        _attn_init(m_ref, l_ref, acc_ref)

    first = lax.broadcasted_iota(I32, (CMP_BLOCK, HEAD_DIM), 0) == 0
    rows = lax.broadcasted_iota(I32, (HROWS, CMP_BLOCK), 0)
    for g in range(N_KV):
        n = idx_ref[b, g, j]
        is_new = n >= n_past
        kpos = n * CMP_BLOCK + lax.broadcasted_iota(I32, (1, CMP_BLOCK), 1)
        dist = q_pos - kpos
        valid = dist >= 0
        biases = _bias_lookup(_bucket(dist), tab_ref, [g * HPG + h for h in range(HPG)])
        bias = jnp.zeros((HROWS, CMP_BLOCK), F32)
        for h in range(HPG):
            bias = jnp.where(rows == h, biases[h], bias)
        kn = new_ref[0][g:g + 1, :]
        vn = new_ref[0][N_KV + g:N_KV + g + 1, :]
        k = jnp.where(is_new, jnp.where(first, kn, 0.0), blk_refs[g][0, :, g, :])
        v = jnp.where(is_new, jnp.where(first, vn, 0.0), blk_refs[g][0, :, N_KV + g, :])
        _attn_update(g, q_ref[0, g].astype(BF16), k.astype(BF16), v.astype(BF16), bias, valid,
                     m_ref, l_ref, acc_ref)

    @pl.when(j == pl.num_programs(1) - 1)
    def _():
        for g in range(N_KV):
            o = acc_ref[g] / jnp.maximum(l_ref[g], 1e-30)
            os_ref[0, g] = _sigmoid(gate_ref[0, 0, g]) * o


def _sel_sample(page_table, idx, rel_bias, q8, cache_half, kvs_new, gates8, q_pos):
    b = q8.shape[0]
    n_past = q_pos // CMP_BLOCK
    per_page = PAGE_SIZE // CMP_BLOCK

    def blk_spec(g):
        def index(i, j, pt, ix):
            n = jnp.minimum(ix[i, g, j], n_past - 1)
            return (pt[i, n // per_page] * per_page + n % per_page, 0, 0, 0)
        return pl.BlockSpec((1, CMP_BLOCK, KV_ROWS, HEAD_DIM), index)

    hspec = pl.BlockSpec((1, N_KV, HROWS, HEAD_DIM), lambda i, j, pt, ix: (i, 0, 0, 0))
    grid_spec = pltpu.PrefetchScalarGridSpec(
        num_scalar_prefetch=2,
        grid=(b, N_SEL),
        in_specs=[pl.BlockSpec(memory_space=pltpu.SMEM), hspec]
        + [blk_spec(g) for g in range(N_KV)]
        + [pl.BlockSpec((1, KV_ROWS, HEAD_DIM), lambda i, j, pt, ix: (i, 0, 0)),
           pl.BlockSpec((1, 1, N_KV, HROWS, 1), lambda i, j, pt, ix: (i, 1, 0, 0, 0))],
        out_specs=hspec,
        scratch_shapes=[pltpu.VMEM((N_KV, HROWS, 1), F32), pltpu.VMEM((N_KV, HROWS, 1), F32),
                        pltpu.VMEM((N_KV, HROWS, HEAD_DIM), F32)],
    )
    return pl.pallas_call(
        functools.partial(_sel_sample_kernel, q_pos=q_pos, n_past=n_past),
        out_shape=jax.ShapeDtypeStruct((b, N_KV, HROWS, HEAD_DIM), F32),
        grid_spec=grid_spec,
        compiler_params=_cparams(2),
        name="sel_sample",
    )(page_table, idx, rel_bias, q8, *([cache_half] * N_KV), kvs_new, gates8)


def _win_sample_kernel(tab_ref, q_ref, win_ref, new_ref, gate_ref, oc_ref, os_ref,
                       yb_ref, kall_ref, vall_ref):
    wc = win_ref.shape[1]
    nk = kall_ref.shape[0]
    kidx = lax.broadcasted_iota(I32, (1, nk), 1)
    dist = wc - kidx
    valid = jnp.where(dist >= 0, dist, WINDOW) < WINDOW
    bkt = _bucket(dist)
    kall_ref[...] = jnp.zeros(kall_ref.shape, F32)
    vall_ref[...] = jnp.zeros(vall_ref.shape, F32)
    for g in range(N_KV):
        kall_ref[0:wc, :] = win_ref[0, :, g, :]
        vall_ref[0:wc, :] = win_ref[0, :, N_KV + g, :]
        kall_ref[wc:wc + 1, :] = new_ref[0][g:g + 1, :]
        vall_ref[wc:wc + 1, :] = new_ref[0][N_KV + g:N_KV + g + 1, :]
        biases = _bias_lookup(bkt, tab_ref, [g * HPG + h for h in range(HPG)])
        rows = lax.broadcasted_iota(I32, (HROWS, nk), 0)
        bias = jnp.zeros((HROWS, nk), F32)
        for h in range(HPG):
            bias = jnp.where(rows == h, biases[h], bias)
        s = _dot_nt(q_ref[0, g].astype(BF16), kall_ref[...].astype(BF16)) * SCALE + bias
        p = _masked_softmax(s, valid)
        o = _dot(p.astype(BF16), vall_ref[...].astype(BF16))
        yb_ref[0, g] = oc_ref[0, g] + os_ref[0, g] + _sigmoid(gate_ref[0, 0, g]) * o


def _win_sample(rel_bias, q8, win, kvw_new, gates8, oc, osel):
    b, wc = win.shape[:2]
    nk = wc + HEAD_DIM
    hspec = pl.BlockSpec((1, N_KV, HROWS, HEAD_DIM), lambda i: (i, 0, 0, 0))
    return pl.pallas_call(
        _win_sample_kernel,
        out_shape=jax.ShapeDtypeStruct((b, N_KV, HROWS, HEAD_DIM), F32),
        grid=(b,),
        in_specs=[pl.BlockSpec(memory_space=pltpu.SMEM),
                  hspec,
                  pl.BlockSpec((1, wc, KV_ROWS, HEAD_DIM), lambda i: (i, 0, 0, 0)),
                  pl.BlockSpec((1, KV_ROWS, HEAD_DIM), lambda i: (i, 0, 0)),
                  pl.BlockSpec((1, 1, N_KV, HROWS, 1), lambda i: (i, 2, 0, 0, 0)),
                  hspec, hspec],
        out_specs=hspec,
        scratch_shapes=[pltpu.VMEM((nk, HEAD_DIM), F32), pltpu.VMEM((nk, HEAD_DIM), F32)],
        compiler_params=_cparams(1),
        name="win_sample",
    )(rel_bias, q8, win, kvw_new, gates8, oc, osel)


def _in_proj(x, norm_w, w_in_t, *, tm, tm_norm):
    xn = _rmsnorm(x, norm_w, BF16, tm_norm)
    return _in_proj_all(xn, w_in_t, tm=tm, tn=IN_TN)


def _out_and_ffn(x, ya, yb, z, w_proj_a, w_proj_b, w_out, norm_ffn, w_gate, w_up, w_down,
                 norm_final, *, tm, tm_norm):
    d_ff = w_gate.shape[1]
    mix = _mix(ya, yb, w_proj_a, w_proj_b, z, tm=tm, tn=256)
    h = _mm(mix, w_out, n_cols=D_MODEL, tm=tm, tn=256, res=x, name="out_proj")
    hn = _rmsnorm(h, norm_ffn, BF16, tm_norm)
    ff = _swiglu(hn, w_gate, w_up, tm=tm, tn=256)
    half = d_ff // 2
    y = _mm(ff, w_down, n_cols=D_MODEL, tm=tm, tn=256, res=h, k_block=0, tk=half, name="ffn_down0")
    y = _mm(ff, w_down, n_cols=D_MODEL, tm=tm, tn=256, res=y, k_block=1, tk=half, name="ffn_down1")
    return _rmsnorm(y, norm_final, F32, tm_norm)


def kernel(x_prompt, x_sample, cache_cmp_kv, cache_sel_kv, cache_win_kv, state_rglru_h, state_conv,
           page_table, rel_bias, norm_mix, w_in, conv_w, conv_b, lru_wa, lru_ba, lru_wi, lru_bi,
           lru_lambda, nsa_w_cmp, w_proj_a, w_proj_b, w_out, norm_ffn, w_gate, w_up, w_down,
           norm_final):
    depth = w_in.shape[0]
    assert depth == 1, "single-layer trunk"
    bp, tp, _ = x_prompt.shape
    bs, ts, _ = x_sample.shape
    assert ts == 1
    n_pool = cache_cmp_kv.shape[1]
    past_len = page_table.shape[1] * PAGE_SIZE
    wc = cache_win_kv.shape[2]

    w_in0 = jnp.swapaxes(w_in[0], 0, 1)
    wexp = jnp.broadcast_to(nsa_w_cmp[0][..., None], (2, N_KV, CMP_BLOCK, HEAD_DIM))
    w8 = jnp.broadcast_to(nsa_w_cmp[0].reshape(KV_ROWS, CMP_BLOCK).T[..., None],
                          (CMP_BLOCK, KV_ROWS, HEAD_DIM))
    layer_w = (conv_w[0], conv_b[0], lru_wa[0], lru_ba[0], lru_wi[0], lru_bi[0], lru_lambda[0])
    tail_w = (w_proj_a[0], w_proj_b[0], w_out[0], norm_ffn[0], w_gate[0], w_up[0], w_down[0],
              norm_final)

    mp = bp * tp
    xp = x_prompt.reshape(mp, D_MODEL)
    z = _in_proj(xp, norm_mix[0], w_in0, tm=1024, tm_norm=256)
    ya, conv_p, h_p = _rglru_prompt(z, bp, tp, *layer_w, tt=128)
    gn_t = z[:, Z_GN:Z_GN + 3 * N_HEADS].reshape(mp, 3, N_KV, HPG).transpose(1, 2, 0, 3)
    bd, bo = _bias_tiles(rel_bias, ATT_T)
    oc, sel = _cmp_prompt(z, rel_bias, wexp, gn_t, bp, tp)
    nblk = tp // CMP_BLOCK
    key_blk = (jnp.arange(tp, dtype=I32) // CMP_BLOCK).reshape(tp // ATT_T, 1, ATT_T)
    expand = (key_blk == jnp.arange(nblk, dtype=I32)[None, :, None]).astype(BF16)
    osel = _sel_prompt(z, rel_bias, bd, bo, gn_t, sel, expand, bp, tp)
    yb = _win_prompt(z, rel_bias, bd, bo, gn_t, oc, osel, bp, tp)
    y_prompt = _out_and_ffn(xp, ya, yb, z, *tail_w, tm=1024, tm_norm=256)

    kv_shape = (1, bp, tp // PAGE_SIZE, PAGE_SIZE, 2, N_KV, HEAD_DIM)
    cmp_p = z[:, OFF_KVC:OFF_KVS].reshape(kv_shape)
    sel_p = z[:, OFF_KVS:OFF_KVW].reshape(kv_shape)
    wlen = min(WINDOW, tp)
    win_p = z[:, OFF_KVW:MAIN_W].reshape(bp, tp, 2, N_KV, HEAD_DIM)[None, :, tp - wlen:]

    xs = x_sample.reshape(bs, D_MODEL)
    zs = _in_proj(xs, norm_mix[0], w_in0, tm=bs, tm_norm=bs)
    buf_t = state_conv[0].transpose(1, 0, 2)
    ya_s, h_s = _rglru_step(zs, buf_t, state_rglru_h[0], *layer_w)

    q8 = jnp.pad(zs[:, OFF_Q:OFF_KVC].reshape(bs, N_KV, HPG, HEAD_DIM),
                 ((0, 0), (0, 0), (0, HROWS - HPG), (0, 0)))
    gates8 = jnp.pad(zs[:, Z_GN:Z_GN + 3 * N_HEADS].reshape(bs, 3, N_KV, HPG),
                     ((0, 0), (0, 0), (0, 0), (0, HROWS - HPG)))[..., None]
    kvc_new = zs[:, OFF_KVC:OFF_KVS].reshape(bs, KV_ROWS, HEAD_DIM)
    kvs_new = zs[:, OFF_KVS:OFF_KVW].reshape(bs, KV_ROWS, HEAD_DIM)
    kvw_new = zs[:, OFF_KVW:MAIN_W].reshape(bs, KV_ROWS, HEAD_DIM)
    per_page = PAGE_SIZE // CMP_BLOCK
    cache_c = cache_cmp_kv.reshape(n_pool, PAGE_SIZE, KV_ROWS, HEAD_DIM)
    cache_s = cache_sel_kv.reshape(n_pool * per_page, CMP_BLOCK, KV_ROWS, HEAD_DIM)
    win = cache_win_kv.reshape(bs, wc, KV_ROWS, HEAD_DIM)

    kvc = _cmp_pages(cache_c, page_table, w8)
    oc_s, idx = _cmp_sample(rel_bias, q8, kvc, kvc_new, w8, gates8, past_len)
    os_s = _sel_sample(page_table, idx, rel_bias, q8, cache_s, kvs_new, gates8, past_len)
    yb8 = _win_sample(rel_bias, q8, win, kvw_new, gates8, oc_s, os_s)
    yb_s = yb8[:, :, :HPG].reshape(bs, Q_W).astype(BF16)
    y_sample = _out_and_ffn(xs, ya_s, yb_s, zs, *tail_w, tm=bs, tm_norm=bs)

    row_shape = (1, bs, 1, 2, N_KV, HEAD_DIM)
    cmp_s = kvc_new.reshape(row_shape)
    sel_s = kvs_new.reshape(row_shape)
    win_s = jnp.concatenate([win[:, 1:], kvw_new[:, None]], axis=1).reshape(1, bs, wc, 2, N_KV, HEAD_DIM)
    conv_s = jnp.concatenate([state_conv[0], zs[:, None, OFF_XR:OFF_GR]], axis=1)[:, 1:][None]

    return (y_prompt.reshape(bp, tp, D_MODEL), y_sample.reshape(bs, ts, D_MODEL),
            cmp_p, sel_p, win_p, h_p.reshape(1, bp, D_RNN), conv_p[None],
            cmp_s, sel_s, win_s, h_s[None], conv_s)
```

```python
import functools
import math

import jax
import jax.numpy as jnp
from jax import lax
from jax.experimental import pallas as pl
from jax.experimental.pallas import tpu as pltpu

F32 = jnp.float32
BF16 = jnp.bfloat16
I32 = jnp.int32

D_MODEL = 4096
D_RNN = 2048
N_RNN_BLOCKS = 16
RNN_BLOCK = D_RNN // N_RNN_BLOCKS
CONV_W = 4
LRU_C = 8.0
N_HEADS = 16
HEAD_DIM = 128
N_KV = 4
HPG = N_HEADS // N_KV
CMP_BLOCK = 64
N_SEL = 16
WINDOW = 512
N_BUCKETS = 32
MAX_DISTANCE = 128
PAGE_SIZE = 128
Q_W = N_HEADS * HEAD_DIM
KV_W = 2 * N_KV * HEAD_DIM
EPS = 1e-6
NEG_INF = -1e30
FORCE = 1e9
SCALE = HEAD_DIM ** -0.5

OFF_XR = 0
OFF_GR = D_RNN
OFF_Q = 2 * D_RNN
OFF_KVC = OFF_Q + Q_W
OFF_KVS = OFF_KVC + KV_W
OFF_KVW = OFF_KVS + KV_W
OFF_GN = OFF_KVW + KV_W
OFF_GA = OFF_GN + 3 * N_HEADS
MAIN_W = OFF_GN
IN_TN = 256
Z_GN = MAIN_W
Z_GA = MAIN_W + IN_TN
Z_GB = Z_GA + D_MODEL

ATT_T = 256
HROWS = 8
KV_ROWS = 2 * N_KV
PAGES_PER_STEP = 8
VMEM_LIMIT = 56 * 1024 * 1024


def _cparams(n_axes):
    return pltpu.CompilerParams(dimension_semantics=("arbitrary",) * n_axes,
                                vmem_limit_bytes=VMEM_LIMIT)


def _sigmoid(x):
    return 1.0 / (1.0 + jnp.exp(-x))


def _gelu_tanh(x):
    return 0.5 * x * (1.0 + jnp.tanh(math.sqrt(2.0 / math.pi) * (x + 0.044715 * (x * x * x))))


def _dot(a, b):
    return jnp.dot(a, b, preferred_element_type=F32)


def _dot_nt(a, b):
    return lax.dot_general(a, b, (((1,), (1,)), ((), ())), preferred_element_type=F32)


def _bucket(dist):
    n = jnp.maximum(dist, 0)
    exact = N_BUCKETS // 2
    nf = jnp.maximum(n, 1).astype(F32)
    large = exact + (jnp.log(nf / exact) / math.log(MAX_DISTANCE / exact)
                     * (N_BUCKETS - exact)).astype(I32)
    return jnp.where(n < exact, n, jnp.minimum(large, N_BUCKETS - 1))


def _bias_lookup(bkt, tab_ref, cols):
    outs = [jnp.zeros(bkt.shape, F32) for _ in cols]
    for b in range(N_BUCKETS):
        eq = bkt == b
        outs = [jnp.where(eq, tab_ref[b, c], o) for c, o in zip(cols, outs)]
    return outs


def _block_scores(imp, blk, cur):
    score = jnp.where(blk == 0, FORCE, jnp.where(blk == cur, FORCE,
                                                 jnp.where(blk == cur - 1, FORCE, imp)))
    return jnp.where(blk <= cur, score, -FORCE)


def _ahead(col, score, later):
    return jnp.where(later, jnp.where(col >= score, 1.0, 0.0), jnp.where(col > score, 1.0, 0.0))


def _masked_softmax(s, valid):
    s = jnp.where(valid, s, NEG_INF)
    m = jnp.max(s, axis=-1, keepdims=True)
    e = jnp.where(valid, jnp.exp(s - m), 0.0)
    return e / jnp.maximum(jnp.sum(e, axis=-1, keepdims=True), 1e-30)


def _rmsnorm_kernel(x_ref, g_ref, o_ref):
    x = x_ref[...]
    y = x * lax.rsqrt(jnp.mean(x * x, axis=-1, keepdims=True) + EPS)
    o_ref[...] = (y * g_ref[...]).astype(o_ref.dtype)


def _rmsnorm(x, g, out_dtype, tm):
    m, d = x.shape
    return pl.pallas_call(
        _rmsnorm_kernel,
        out_shape=jax.ShapeDtypeStruct((m, d), out_dtype),
        grid=(m // tm,),
        in_specs=[pl.BlockSpec((tm, d), lambda i: (i, 0)),
                  pl.BlockSpec((1, d), lambda i: (0, 0))],
        out_specs=pl.BlockSpec((tm, d), lambda i: (i, 0)),
        compiler_params=_cparams(1),
        name="rmsnorm",
    )(x, g.reshape(1, d))


def _mm_kernel(a_ref, w_ref, o_ref):
    o_ref[...] = _dot(a_ref[...], w_ref[...].astype(BF16)).astype(o_ref.dtype)


def _mm_res_kernel(a_ref, w_ref, r_ref, o_ref):
    o_ref[...] = r_ref[...] + _dot(a_ref[...], w_ref[...].astype(BF16))


def _mm(a, w, *, n_cols, tm, tn, out_dtype=F32, res=None, k_block=0, tk=None, name="mm"):
    m = a.shape[0]
    tk = a.shape[1] if tk is None else tk
    grid = (m // tm, n_cols // tn)
    in_specs = [pl.BlockSpec((tm, tk), lambda i, j: (i, k_block)),
                pl.BlockSpec((tk, tn), lambda i, j: (k_block, j))]
    args = [a, w]
    kern = _mm_kernel
    if res is not None:
        in_specs.append(pl.BlockSpec((tm, tn), lambda i, j: (i, j)))
        args.append(res)
        kern = _mm_res_kernel
    return pl.pallas_call(
        kern,
        out_shape=jax.ShapeDtypeStruct((m, n_cols), out_dtype),
        grid=grid,
        in_specs=in_specs,
        out_specs=pl.BlockSpec((tm, tn), lambda i, j: (i, j)),
        compiler_params=_cparams(2),
        name=name,
    )(*args)


def _in_proj_kernel(a_ref, wt_ref, o_ref):
    o_ref[...] = _dot_nt(a_ref[...], wt_ref[...].astype(BF16))


def _in_proj_all(a, wt, *, tm, tn):
    m, k = a.shape
    assert MAIN_W % tn == 0 and Z_GN == MAIN_W and Z_GA == MAIN_W + tn and 3 * N_HEADS <= tn
    n_gate_blocks = 2 * D_MODEL // tn
    nb = MAIN_W // tn + 1 + n_gate_blocks
    assert OFF_GA + n_gate_blocks * tn == wt.shape[0]

    def w_index(i, j):
        row = jnp.where(j <= MAIN_W // tn, j * tn, OFF_GA + (j - MAIN_W // tn - 1) * tn)
        return (pl.multiple_of(row, 8), 0)

    return pl.pallas_call(
        _in_proj_kernel,
        out_shape=jax.ShapeDtypeStruct((m, nb * tn), F32),
        grid=(m // tm, nb),
        in_specs=[pl.BlockSpec((tm, k), lambda i, j: (i, 0)),
                  pl.BlockSpec((pl.Element(tn), pl.Element(k)), w_index)],
        out_specs=pl.BlockSpec((tm, tn), lambda i, j: (i, j)),
        compiler_params=_cparams(2),
        name="in_proj",
    )(a, wt)


def _swiglu_kernel(a_ref, wg_ref, wu_ref, o_ref):
    a = a_ref[...]
    g = _dot(a, wg_ref[...].astype(BF16))
    u = _dot(a, wu_ref[...].astype(BF16))
    o_ref[...] = ((g * _sigmoid(g)) * u).astype(o_ref.dtype)


def _swiglu(a, wg, wu, *, tm, tn):
    m, k = a.shape
    n = wg.shape[1]
    return pl.pallas_call(
        _swiglu_kernel,
        out_shape=jax.ShapeDtypeStruct((m, n), BF16),
        grid=(m // tm, n // tn),
        in_specs=[pl.BlockSpec((tm, k), lambda i, j: (i, 0)),
                  pl.BlockSpec((k, tn), lambda i, j: (0, j)),
                  pl.BlockSpec((k, tn), lambda i, j: (0, j))],
        out_specs=pl.BlockSpec((tm, tn), lambda i, j: (i, j)),
        compiler_params=_cparams(2),
        name="ffn_gate_up",
    )(a, wg, wu)


def _mix_kernel(ya_ref, yb_ref, wa_ref, wb_ref, ga_ref, gb_ref, o_ref):
    pa = _dot(ya_ref[...], wa_ref[...].astype(BF16))
    pb = _dot(yb_ref[...], wb_ref[...].astype(BF16))
    o_ref[...] = (_sigmoid(ga_ref[...]) * pa + _sigmoid(gb_ref[...]) * pb).astype(o_ref.dtype)


def _mix(ya, yb, wa, wb, z, *, tm, tn):
    m, k = ya.shape
    n = wa.shape[1]
    assert Z_GA % tn == 0 and Z_GB % tn == 0
    gate = lambda off: pl.BlockSpec((tm, tn), lambda i, j: (i, j + off // tn))
    return pl.pallas_call(
        _mix_kernel,
        out_shape=jax.ShapeDtypeStruct((m, n), BF16),
        grid=(m // tm, n // tn),
        in_specs=[pl.BlockSpec((tm, k), lambda i, j: (i, 0)),
                  pl.BlockSpec((tm, k), lambda i, j: (i, 0)),
                  pl.BlockSpec((k, tn), lambda i, j: (0, j)),
                  pl.BlockSpec((k, tn), lambda i, j: (0, j)),
                  gate(Z_GA), gate(Z_GB)],
        out_specs=pl.BlockSpec((tm, tn), lambda i, j: (i, j)),
        compiler_params=_cparams(2),
        name="branch_merge",
    )(ya, yb, wa, wb, z, z)


def _lru_gates(xc, wa_ref, ba_ref, wi_ref, bi_ref, lam_ref, a_ref, u_ref):
    for n in range(N_RNN_BLOCKS):
        sl = slice(n * RNN_BLOCK, (n + 1) * RNN_BLOCK)
        xn = xc[:, sl]
        xb = xn.astype(BF16)
        r = _sigmoid(_dot(xb, wa_ref[n].astype(BF16)) + ba_ref[:, sl])
        i = _sigmoid(_dot(xb, wi_ref[n].astype(BF16)) + bi_ref[:, sl])
        neg_lam = -lam_ref[:, sl]
        softplus = jnp.maximum(neg_lam, 0.0) + jnp.log1p(jnp.exp(-jnp.abs(neg_lam)))
        log_a = (-LRU_C * r) * softplus
        a = jnp.exp(log_a)
        a_ref[:, sl] = a
        u_ref[:, sl] = jnp.sqrt(-jnp.tanh(log_a) * (a * a + 1.0)) * (i * xn)


def _rglru_prompt_kernel(xr_ref, gr_ref, cw_ref, cb_ref, wa_ref, ba_ref, wi_ref, bi_ref, lam_ref,
                         ya_ref, conv_ref, hl_ref, xp_ref, a_ref, u_ref, h_ref, carry_ref):
    ti = pl.program_id(1)
    tt, c = xr_ref.shape

    @pl.when(ti == 0)
    def _():
        xp_ref[0:8, :] = jnp.zeros((8, c), F32)
        carry_ref[...] = jnp.zeros((1, c), F32)

    x = xr_ref[...]
    xp_ref[8:8 + tt, :] = x
    xc = cb_ref[...]
    for k in range(CONV_W):
        xc = xc + cw_ref[k:k + 1, :] * xp_ref[5 + k:5 + k + tt, :]
    xp_ref[0:8, :] = xp_ref[tt:tt + 8, :]

    _lru_gates(xc, wa_ref, ba_ref, wi_ref, bi_ref, lam_ref, a_ref, u_ref)

    row = lax.broadcasted_iota(I32, (8, c), 0)

    def chunk(ci, carry):
        off = pl.multiple_of(ci * 8, 8)
        a = a_ref[pl.ds(off, 8), :]
        b = u_ref[pl.ds(off, 8), :]
        for s in (1, 2, 4):
            keep = row >= s
            b = jnp.where(keep, a * pltpu.roll(b, s, 0) + b, b)
            a = jnp.where(keep, a * pltpu.roll(a, s, 0), a)
        h = a * carry + b
        h_ref[pl.ds(off, 8), :] = h
        return h[7:8, :]

    carry = lax.fori_loop(0, tt // 8, chunk, carry_ref[...])
    carry_ref[...] = carry
    ya_ref[...] = (h_ref[...] * _gelu_tanh(gr_ref[...])).astype(ya_ref.dtype)

    @pl.when(ti == pl.num_programs(1) - 1)
    def _():
        conv_ref[0] = x[tt - (CONV_W - 1):, :]
        hl_ref[0] = carry


def _rglru_prompt(z, batch, seq, conv_w, conv_b, lru_wa, lru_ba, lru_wi, lru_bi, lam, *, tt):
    nt = seq // tt
    c = D_RNN
    row = lambda b, t: (b * nt + t, 0)
    full2 = lambda b, t: (0, 0)
    full3 = lambda b, t: (0, 0, 0)
    return pl.pallas_call(
        _rglru_prompt_kernel,
        out_shape=(jax.ShapeDtypeStruct((batch * seq, c), BF16),
                   jax.ShapeDtypeStruct((batch, CONV_W - 1, c), F32),
                   jax.ShapeDtypeStruct((batch, 1, c), F32)),
        grid=(batch, nt),
        in_specs=[pl.BlockSpec((tt, c), row),
                  pl.BlockSpec((tt, c), lambda b, t: (b * nt + t, 1)),
                  pl.BlockSpec((CONV_W, c), full2),
                  pl.BlockSpec((1, c), full2),
                  pl.BlockSpec((N_RNN_BLOCKS, RNN_BLOCK, RNN_BLOCK), full3),
                  pl.BlockSpec((1, c), full2),
                  pl.BlockSpec((N_RNN_BLOCKS, RNN_BLOCK, RNN_BLOCK), full3),
                  pl.BlockSpec((1, c), full2),
                  pl.BlockSpec((1, c), full2)],
        out_specs=(pl.BlockSpec((tt, c), row),
                   pl.BlockSpec((1, CONV_W - 1, c), lambda b, t: (b, 0, 0)),
                   pl.BlockSpec((1, 1, c), lambda b, t: (b, 0, 0))),
        scratch_shapes=[pltpu.VMEM((tt + 8, c), F32), pltpu.VMEM((tt, c), F32),
                        pltpu.VMEM((tt, c), F32), pltpu.VMEM((tt, c), F32),
                        pltpu.VMEM((1, c), F32)],
        compiler_params=_cparams(2),
        name="rglru_prompt",
    )(z, z, conv_w, conv_b.reshape(1, c), lru_wa, lru_ba.reshape(1, c), lru_wi,
      lru_bi.reshape(1, c), lam.reshape(1, c))


def _rglru_step_kernel(xr_ref, gr_ref, buf_ref, h0_ref, cw_ref, cb_ref, wa_ref, ba_ref, wi_ref,
                       bi_ref, lam_ref, ya_ref, hn_ref, a_ref, u_ref):
    x = xr_ref[...]
    xc = cb_ref[...]
    for k in range(CONV_W - 1):
        xc = xc + cw_ref[k:k + 1, :] * buf_ref[k]
    xc = xc + cw_ref[CONV_W - 1:CONV_W, :] * x
    _lru_gates(xc, wa_ref, ba_ref, wi_ref, bi_ref, lam_ref, a_ref, u_ref)
    h = a_ref[...] * h0_ref[...] + u_ref[...]
    hn_ref[...] = h
    ya_ref[...] = (h * _gelu_tanh(gr_ref[...])).astype(ya_ref.dtype)


def _rglru_step(z, buf_t, h0, conv_w, conv_b, lru_wa, lru_ba, lru_wi, lru_bi, lam):
    b, c = h0.shape
    full2 = lambda i: (0, 0)
    full3 = lambda i: (0, 0, 0)
    return pl.pallas_call(
        _rglru_step_kernel,
        out_shape=(jax.ShapeDtypeStruct((b, c), BF16), jax.ShapeDtypeStruct((b, c), F32)),
        grid=(1,),
        in_specs=[pl.BlockSpec((b, c), full2),
                  pl.BlockSpec((b, c), lambda i: (0, 1)),
                  pl.BlockSpec((CONV_W - 1, b, c), full3),
                  pl.BlockSpec((b, c), full2),
                  pl.BlockSpec((CONV_W, c), full2),
                  pl.BlockSpec((1, c), full2),
                  pl.BlockSpec((N_RNN_BLOCKS, RNN_BLOCK, RNN_BLOCK), full3),
                  pl.BlockSpec((1, c), full2),
                  pl.BlockSpec((N_RNN_BLOCKS, RNN_BLOCK, RNN_BLOCK), full3),
                  pl.BlockSpec((1, c), full2),
                  pl.BlockSpec((1, c), full2)],
        out_specs=(pl.BlockSpec((b, c), full2), pl.BlockSpec((b, c), full2)),
        scratch_shapes=[pltpu.VMEM((b, c), F32), pltpu.VMEM((b, c), F32)],
        compiler_params=_cparams(1),
        name="rglru_step",
    )(z, z, buf_t, h0, conv_w, conv_b.reshape(1, c), lru_wa, lru_ba.reshape(1, c), lru_wi,
      lru_bi.reshape(1, c), lam.reshape(1, c))


def _bias_tiles_kernel(tab_ref, bd_ref, bo_ref):
    h = pl.program_id(0)
    t = bd_ref.shape[1]
    d0 = lax.broadcasted_iota(I32, (t, t), 1) - lax.broadcasted_iota(I32, (t, t), 0)
    bd_ref[0] = jnp.where(d0 >= 0, _bias_lookup(_bucket(d0), tab_ref, [h])[0], NEG_INF)
    bo_ref[0] = _bias_lookup(_bucket(d0 + t), tab_ref, [h])[0]


def _bias_tiles(rel_bias, t):
    shp = jax.ShapeDtypeStruct((N_HEADS, t, t), F32)
    spec = pl.BlockSpec((1, t, t), lambda h: (h, 0, 0))
    return pl.pallas_call(
        _bias_tiles_kernel,
        out_shape=(shp, shp),
        grid=(N_HEADS,),
        in_specs=[pl.BlockSpec(memory_space=pltpu.SMEM)],
        out_specs=(spec, spec),
        compiler_params=_cparams(1),
        name="bias_tiles",
    )(rel_bias)


def _cmp_prompt_kernel(tab_ref, q_ref, kc_ref, vc_ref, wk_ref, wv_ref, gate_ref,
                       oc_ref, sel_ref, kcmp_ref, vcmp_ref, vcmp_t_ref):
    g = pl.program_id(1)
    qi = pl.program_id(2)
    tq = q_ref.shape[0]
    seq = kc_ref.shape[0]
    nblk = seq // CMP_BLOCK

    @pl.when(qi == 0)
    def _():
        k3 = kc_ref[...].reshape(nblk, CMP_BLOCK, HEAD_DIM)
        v3 = vc_ref[...].reshape(nblk, CMP_BLOCK, HEAD_DIM)
        kcmp_ref[...] = jnp.sum(k3 * wk_ref[0, 0][None], axis=1).astype(BF16)
        vcmp_ref[...] = jnp.zeros(vcmp_ref.shape, F32)
        vcmp_ref[0:nblk, :] = jnp.sum(v3 * wv_ref[0, 0][None], axis=1)
        vcmp_t_ref[...] = jnp.transpose(vcmp_ref[...]).astype(BF16)

    tpos = qi * tq + lax.broadcasted_iota(I32, (nblk, tq), 1)
    blk = lax.broadcasted_iota(I32, (nblk, tq), 0)
    dist = tpos - (blk * CMP_BLOCK + (CMP_BLOCK - 1))
    valid = dist >= 0
    biases = _bias_lookup(_bucket(dist), tab_ref, [g * HPG + h for h in range(HPG)])
    kb = kcmp_ref[...]
    vt = vcmp_t_ref[:, 0:nblk]
    gates = _sigmoid(gate_ref[0, 0])
    imp = jnp.zeros((nblk, tq), F32)
    for h in range(HPG):
        sl = slice(h * HEAD_DIM, (h + 1) * HEAD_DIM)
        s = _dot_nt(kb, q_ref[:, sl].astype(BF16)) * SCALE + biases[h]
        s = jnp.where(valid, s, NEG_INF)
        m = jnp.max(s, axis=0, keepdims=True)
        e = jnp.where(valid, jnp.exp(s - m), 0.0)
        p = e / jnp.maximum(jnp.sum(e, axis=0, keepdims=True), 1e-30)
        imp = imp + p
        o_t = _dot(vt, p.astype(BF16)) * gates[h:h + 1, :]
        oc_ref[:, sl] = jnp.transpose(o_t)

    cur = jnp.right_shift(tpos, 6)
    score = _block_scores(imp, blk, cur)
    rank = jnp.zeros((nblk, tq), F32)
    for j in range(nblk):
        rank = rank + _ahead(score[j:j + 1, :], score, blk > j)
    sel_ref[0, 0] = jnp.where(rank < float(min(N_SEL, nblk)), 0.0, NEG_INF)


def _cmp_prompt(z, rel_bias, wexp, gn_t, batch, seq):
    tq = ATT_T
    nq = seq // tq
    nblk = seq // CMP_BLOCK
    m = batch * seq
    qcol = OFF_Q // (HPG * HEAD_DIM)
    kcol = OFF_KVC // HEAD_DIM
    return pl.pallas_call(
        _cmp_prompt_kernel,
        out_shape=(jax.ShapeDtypeStruct((m, Q_W), F32),
                   jax.ShapeDtypeStruct((batch, N_KV, nblk, seq), F32)),
        grid=(batch, N_KV, nq),
        in_specs=[pl.BlockSpec(memory_space=pltpu.SMEM),
                  pl.BlockSpec((tq, HPG * HEAD_DIM), lambda b, g, i: (b * nq + i, qcol + g)),
                  pl.BlockSpec((seq, HEAD_DIM), lambda b, g, i: (b, kcol + g)),
                  pl.BlockSpec((seq, HEAD_DIM), lambda b, g, i: (b, kcol + N_KV + g)),
                  pl.BlockSpec((1, 1, CMP_BLOCK, HEAD_DIM), lambda b, g, i: (0, g, 0, 0)),
                  pl.BlockSpec((1, 1, CMP_BLOCK, HEAD_DIM), lambda b, g, i: (1, g, 0, 0)),
                  pl.BlockSpec((1, 1, HPG, tq), lambda b, g, i: (0, g, 0, b * nq + i))],
        out_specs=(pl.BlockSpec((tq, HPG * HEAD_DIM), lambda b, g, i: (b * nq + i, g)),
                   pl.BlockSpec((1, 1, nblk, tq), lambda b, g, i: (b, g, 0, i))),
        scratch_shapes=[pltpu.VMEM((nblk, HEAD_DIM), BF16), pltpu.VMEM((HEAD_DIM, HEAD_DIM), F32),
                        pltpu.VMEM((HEAD_DIM, HEAD_DIM), BF16)],
        compiler_params=_cparams(3),
        name="cmp_prompt",
    )(rel_bias, z, z, z, wexp, wexp, gn_t)


def _attn_init(m_ref, l_ref, acc_ref):
    m_ref[...] = jnp.full(m_ref.shape, NEG_INF, F32)
    l_ref[...] = jnp.zeros(l_ref.shape, F32)
    acc_ref[...] = jnp.zeros(acc_ref.shape, F32)


def _attn_update(h, qh, k, v, bias, valid, m_ref, l_ref, acc_ref):
    s = _dot_nt(qh, k) * SCALE + bias
    if valid is not None:
        s = jnp.where(valid, s, NEG_INF)
    m_old = m_ref[h]
    m_new = jnp.maximum(m_old, jnp.max(s, axis=-1, keepdims=True))
    alpha = jnp.exp(m_old - m_new)
    e = jnp.exp(s - m_new)
    if valid is not None:
        e = jnp.where(valid, e, 0.0)
    l_ref[h] = alpha * l_ref[h] + jnp.sum(e, axis=-1, keepdims=True)
    acc_ref[h] = alpha * acc_ref[h] + _dot(e.astype(BF16), v)
    m_ref[h] = m_new


def _attn_prologue(qi, q_ref, k_ref, v_ref, kb_ref, vt_ref, qb_ref, m_ref, l_ref, acc_ref):
    n_tiles, t = kb_ref.shape[0], kb_ref.shape[1]

    @pl.when(qi == 0)
    def _():
        for kj in range(n_tiles):
            rows = slice(kj * t, (kj + 1) * t)
            kb_ref[kj] = k_ref[rows, :].astype(BF16)
            vt_ref[kj] = jnp.transpose(v_ref[rows, :]).astype(BF16)

    for h in range(HPG):
        qb_ref[h] = q_ref[:, h * HEAD_DIM:(h + 1) * HEAD_DIM].astype(BF16)
    _attn_init(m_ref, l_ref, acc_ref)


def _attn_update_t(h, k, vt, qh, bias, m_ref, l_ref, acc_ref, valid=None):
    s = _dot_nt(k, qh) * SCALE + bias
    if valid is not None:
        s = jnp.where(valid, s, NEG_INF)
    m_old = m_ref[h]
    m_new = jnp.maximum(m_old, jnp.max(s, axis=0, keepdims=True))
    alpha = jnp.exp(m_old - m_new)
    e = jnp.exp(s - m_new)
    l_ref[h] = alpha * l_ref[h] + jnp.sum(e, axis=0, keepdims=True)
    acc_ref[h] = alpha * acc_ref[h] + _dot(vt, e.astype(BF16))
    m_ref[h] = m_new


def _attn_output_t(h, gates, l_ref, acc_ref):
    o_t = acc_ref[h] / jnp.maximum(l_ref[h], 1e-30) * gates[h:h + 1, :]
    return jnp.transpose(o_t)


def _sel_prompt_kernel(tab_ref, q_ref, k_ref, v_ref, bd_ref, bo_ref, gate_ref, sel_ref,
                       os_ref, kb_ref, vt_ref, qb_ref, m_ref, l_ref, acc_ref):
    g = pl.program_id(1)
    qi = pl.program_id(2)
    t = q_ref.shape[0]
    per_tile = t // CMP_BLOCK
    _attn_prologue(qi, q_ref, k_ref, v_ref, kb_ref, vt_ref, qb_ref, m_ref, l_ref, acc_ref)

    def tile(kj, kind):
        rows = [jnp.broadcast_to(sel_ref[0, 0, pl.ds(kj * per_tile + i, 1), :], (CMP_BLOCK, t))
                for i in range(per_tile)]
        mask = jnp.concatenate(rows, axis=0)
        k = kb_ref[kj]
        vt = vt_ref[kj]
        for h in range(HPG):
            if kind == "far":
                bias = mask + tab_ref[N_BUCKETS - 1, g * HPG + h]
            elif kind == "off":
                bias = mask + bo_ref[h]
            else:
                bias = mask + bd_ref[h]
            _attn_update_t(h, k, vt, qb_ref[h], bias, m_ref, l_ref, acc_ref)

    def far_body(kj, carry):
        tile(kj, "far")
        return carry

    lax.fori_loop(0, jnp.maximum(qi - 1, 0), far_body, 0)

    @pl.when(qi >= 1)
    def _():
        tile(qi - 1, "off")

    tile(qi, "diag")
    gates = _sigmoid(gate_ref[0, 0])
    for h in range(HPG):
        os_ref[:, h * HEAD_DIM:(h + 1) * HEAD_DIM] = _attn_output_t(h, gates, l_ref, acc_ref)


def _win_prompt_kernel(tab_ref, q_ref, k_ref, v_ref, bd_ref, bo_ref, gate_ref, oc_ref, os_ref,
                       yb_ref, kb_ref, vt_ref, qb_ref, m_ref, l_ref, acc_ref):
    g = pl.program_id(1)
    qi = pl.program_id(2)
    t = q_ref.shape[0]
    _attn_prologue(qi, q_ref, k_ref, v_ref, kb_ref, vt_ref, qb_ref, m_ref, l_ref, acc_ref)
    key = lax.broadcasted_iota(I32, (t, t), 0)
    qry = lax.broadcasted_iota(I32, (t, t), 1)

    def tile(kj, kind):
        k = kb_ref[kj]
        vt = vt_ref[kj]
        for h in range(HPG):
            if kind == "far":
                bias, valid = tab_ref[N_BUCKETS - 1, g * HPG + h], qry < key
            elif kind == "off":
                bias, valid = bo_ref[h], None
            else:
                bias, valid = bd_ref[h], None
            _attn_update_t(h, k, vt, qb_ref[h], bias, m_ref, l_ref, acc_ref, valid)

    tile(qi, "diag")

    @pl.when(qi >= 1)
    def _():
        tile(qi - 1, "off")

    @pl.when(qi >= 2)
    def _():
        tile(qi - 2, "far")

    gates = _sigmoid(gate_ref[0, 0])
    for h in range(HPG):
        sl = slice(h * HEAD_DIM, (h + 1) * HEAD_DIM)
        o = _attn_output_t(h, gates, l_ref, acc_ref)
        yb_ref[:, sl] = (oc_ref[:, sl] + os_ref[:, sl] + o).astype(yb_ref.dtype)


def _attn_prompt_specs(batch, seq, kv_off, gate_idx):
    t = ATT_T
    nq = seq // t
    qcol = OFF_Q // (HPG * HEAD_DIM)
    kcol = kv_off // HEAD_DIM
    rowblk = lambda b, g, i: (b * nq + i, g)
    specs = [pl.BlockSpec(memory_space=pltpu.SMEM),
             pl.BlockSpec((t, HPG * HEAD_DIM), lambda b, g, i: (b * nq + i, qcol + g)),
             pl.BlockSpec((seq, HEAD_DIM), lambda b, g, i: (b, kcol + g)),
             pl.BlockSpec((seq, HEAD_DIM), lambda b, g, i: (b, kcol + N_KV + g)),
             pl.BlockSpec((HPG, t, t), lambda b, g, i: (g, 0, 0)),
             pl.BlockSpec((HPG, t, t), lambda b, g, i: (g, 0, 0)),
             pl.BlockSpec((1, 1, HPG, t), lambda b, g, i: (gate_idx, g, 0, b * nq + i))]
    scratch = [pltpu.VMEM((seq // t, t, HEAD_DIM), BF16), pltpu.VMEM((seq // t, HEAD_DIM, t), BF16),
               pltpu.VMEM((HPG, t, HEAD_DIM), BF16),
               pltpu.VMEM((HPG, 1, t), F32), pltpu.VMEM((HPG, 1, t), F32),
               pltpu.VMEM((HPG, HEAD_DIM, t), F32)]
    return specs, scratch, pl.BlockSpec((t, HPG * HEAD_DIM), rowblk), (batch, N_KV, nq)


def _sel_prompt(z, rel_bias, bd, bo, gn_t, sel, batch, seq):
    specs, scratch, out_spec, grid = _attn_prompt_specs(batch, seq, OFF_KVS, 1)
    t = ATT_T
    nblk = seq // CMP_BLOCK
    specs += [pl.BlockSpec((1, 1, nblk, t), lambda b, g, i: (b, g, 0, i))]
    return pl.pallas_call(
        _sel_prompt_kernel,
        out_shape=jax.ShapeDtypeStruct((batch * seq, Q_W), F32),
        grid=grid, in_specs=specs, out_specs=out_spec, scratch_shapes=scratch,
        compiler_params=_cparams(3), name="sel_prompt",
    )(rel_bias, z, z, z, bd, bo, gn_t, sel)


def _win_prompt(z, rel_bias, bd, bo, gn_t, oc, osel, batch, seq):
    specs, scratch, out_spec, grid = _attn_prompt_specs(batch, seq, OFF_KVW, 2)
    specs += [out_spec, out_spec]
    return pl.pallas_call(
        _win_prompt_kernel,
        out_shape=jax.ShapeDtypeStruct((batch * seq, Q_W), BF16),
        grid=grid, in_specs=specs, out_specs=out_spec, scratch_shapes=scratch,
        compiler_params=_cparams(3), name="win_prompt",
    )(rel_bias, z, z, z, bd, bo, gn_t, oc, osel)


def _cmp_pages_kernel(pt_ref, *refs):
    page_refs, w_ref, o_ref = refs[:-2], refs[-2], refs[-1]
    per_page = PAGE_SIZE // CMP_BLOCK
    w = w_ref[...]
    for k, page_ref in enumerate(page_refs):
        x = page_ref[0].reshape(per_page, CMP_BLOCK, KV_ROWS, HEAD_DIM)
        o_ref[0, per_page * k:per_page * (k + 1)] = jnp.sum(x * w[None], axis=1)


def _cmp_pages(cache, page_table, w8):
    b, n_pages = page_table.shape
    per_page = PAGE_SIZE // CMP_BLOCK
    pps = PAGES_PER_STEP
    page = lambda k: pl.BlockSpec((1, PAGE_SIZE, KV_ROWS, HEAD_DIM),
                                  lambda i, p, pt: (pt[i, p * pps + k], 0, 0, 0))
    grid_spec = pltpu.PrefetchScalarGridSpec(
        num_scalar_prefetch=1,
        grid=(b, n_pages // pps),
        in_specs=[page(k) for k in range(pps)]
        + [pl.BlockSpec((CMP_BLOCK, KV_ROWS, HEAD_DIM), lambda i, p, pt: (0, 0, 0))],
        out_specs=pl.BlockSpec((1, pps * per_page, KV_ROWS, HEAD_DIM), lambda i, p, pt: (i, p, 0, 0)),
    )
    return pl.pallas_call(
        _cmp_pages_kernel,
        out_shape=jax.ShapeDtypeStruct((b, n_pages * per_page, KV_ROWS, HEAD_DIM), F32),
        grid_spec=grid_spec,
        compiler_params=_cparams(2),
        name="cmp_pages",
    )(page_table, *([cache] * pps), w8)


def _cmp_sample_kernel(tab_ref, q_ref, kvc_ref, new_ref, w_ref, gate_ref,
                       oc_ref, idx_ref, kall_ref, vall_ref, imp_ref, *, q_pos):
    nblk = kvc_ref.shape[1]
    nk = kall_ref.shape[0]
    blk = lax.broadcasted_iota(I32, (1, nk), 1)
    dist = q_pos - (blk * CMP_BLOCK + (CMP_BLOCK - 1))
    valid = dist >= 0
    bkt = _bucket(dist)
    new_c = new_ref[0] * w_ref[0]
    kall_ref[...] = jnp.zeros(kall_ref.shape, F32)
    vall_ref[...] = jnp.zeros(vall_ref.shape, F32)
    for g in range(N_KV):
        kall_ref[0:nblk, :] = kvc_ref[0, :, g, :]
        vall_ref[0:nblk, :] = kvc_ref[0, :, N_KV + g, :]
        kall_ref[nblk:nblk + 1, :] = new_c[g:g + 1, :]
        vall_ref[nblk:nblk + 1, :] = new_c[N_KV + g:N_KV + g + 1, :]
        kb = kall_ref[...].astype(BF16)
        vb = vall_ref[...].astype(BF16)
        biases = _bias_lookup(bkt, tab_ref, [g * HPG + h for h in range(HPG)])
        rows = lax.broadcasted_iota(I32, (HROWS, nk), 0)
        bias = jnp.zeros((HROWS, nk), F32)
        for h in range(HPG):
            bias = jnp.where(rows == h, biases[h], bias)
        s = _dot_nt(q_ref[0, g].astype(BF16), kb) * SCALE + bias
        p = _masked_softmax(s, valid)
        gate = _sigmoid(gate_ref[0, 0, g])
        oc_ref[0, g] = gate * _dot(p.astype(BF16), vb)
        head_row = jnp.where(rows < HPG, p, 0.0)
        imp_ref[g:g + 1, :] = jnp.sum(head_row, axis=0, keepdims=True)

    imp = imp_ref[0:N_KV, :]
    blk4 = lax.broadcasted_iota(I32, (N_KV, nk), 1)
    cur = q_pos // CMP_BLOCK
    score = _block_scores(imp, blk4, cur)
    rank = jnp.zeros((N_KV, nk), F32)
    for j in range(nblk + 1):
        col = score[:, j:j + 1]
        rank = rank + _ahead(col, score, blk4 > j)
    rank = jnp.where(blk4 <= nblk, rank, float(nk))
    blkf = blk4.astype(F32)
    slot = lax.broadcasted_iota(I32, (N_KV, N_SEL), 1)
    idx = jnp.zeros((N_KV, N_SEL), F32)
    for r in range(N_SEL):
        pick = jnp.sum(jnp.where(rank == float(r), blkf, 0.0), axis=-1, keepdims=True)
        idx = jnp.where(slot == r, pick, idx)
    idx_ref[0] = idx.astype(I32)


def _cmp_sample(rel_bias, q8, kvc, kvc_new, w8, gates8, q_pos):
    b, nblk = kvc.shape[:2]
    nk = 2 * HEAD_DIM
    assert nblk + 1 <= nk and q_pos // CMP_BLOCK == nblk
    return pl.pallas_call(
        functools.partial(_cmp_sample_kernel, q_pos=q_pos),
        out_shape=(jax.ShapeDtypeStruct((b, N_KV, HROWS, HEAD_DIM), F32),
                   jax.ShapeDtypeStruct((b, N_KV, N_SEL), I32)),
        grid=(b,),
        in_specs=[pl.BlockSpec(memory_space=pltpu.SMEM),
                  pl.BlockSpec((1, N_KV, HROWS, HEAD_DIM), lambda i: (i, 0, 0, 0)),
                  pl.BlockSpec((1, nblk, KV_ROWS, HEAD_DIM), lambda i: (i, 0, 0, 0)),
                  pl.BlockSpec((1, KV_ROWS, HEAD_DIM), lambda i: (i, 0, 0)),
                  pl.BlockSpec((CMP_BLOCK, KV_ROWS, HEAD_DIM), lambda i: (0, 0, 0)),
                  pl.BlockSpec((1, 1, N_KV, HROWS, 1), lambda i: (i, 0, 0, 0, 0))],
        out_specs=(pl.BlockSpec((1, N_KV, HROWS, HEAD_DIM), lambda i: (i, 0, 0, 0)),
                   pl.BlockSpec((1, N_KV, N_SEL), lambda i: (i, 0, 0))),
        scratch_shapes=[pltpu.VMEM((nk, HEAD_DIM), F32), pltpu.VMEM((nk, HEAD_DIM), F32),
                        pltpu.VMEM((HROWS, nk), F32)],
        compiler_params=_cparams(1),
        name="cmp_sample",
    )(rel_bias, q8, kvc, kvc_new, w8, gates8)


def _sel_bias_kernel(tab_ref, o_ref, *, q_pos):
    g = pl.program_id(0)
    shape = o_ref.shape[1:]
    blk = lax.broadcasted_iota(I32, shape, 0)
    rows = lax.broadcasted_iota(I32, shape, 1)
    offs = lax.broadcasted_iota(I32, shape, 2)
    dist = q_pos - (blk * CMP_BLOCK + offs)
    biases = _bias_lookup(_bucket(dist), tab_ref, [g * HPG + h for h in range(HPG)])
    bias = jnp.zeros(shape, F32)
    for h in range(HPG):
        bias = jnp.where(rows == h, biases[h], bias)
    o_ref[0] = bias


def _sel_bias(rel_bias, q_pos, n_blocks):
    return pl.pallas_call(
        functools.partial(_sel_bias_kernel, q_pos=q_pos),
        out_shape=jax.ShapeDtypeStruct((N_KV, n_blocks, HROWS, CMP_BLOCK), F32),
        grid=(N_KV,),
        in_specs=[pl.BlockSpec(memory_space=pltpu.SMEM)],
        out_specs=pl.BlockSpec((1, n_blocks, HROWS, CMP_BLOCK), lambda g: (g, 0, 0, 0)),
        compiler_params=_cparams(1),
        name="sel_bias",
    )(rel_bias)


def _sel_sample_kernel(pt_ref, idx_ref, bias_ref, q_ref, *refs, q_pos, n_past):
    blk_refs = refs[:N_KV]
    new_ref, gate_ref, os_ref, m_ref, l_ref, acc_ref = refs[N_KV:]
    b = pl.program_id(0)
    j = pl.program_id(1)

    @pl.when(j == 0)
    def _():
        _attn_init(m_ref, l_ref, acc_ref)

    first = lax.broadcasted_iota(I32, (CMP_BLOCK, HEAD_DIM), 0) == 0
    for g in range(N_KV):
        n = idx_ref[b, g, j]
        is_new = n >= n_past
        kpos = n * CMP_BLOCK + lax.broadcasted_iota(I32, (1, CMP_BLOCK), 1)
        valid = q_pos - kpos >= 0
        bias = bias_ref[g, n]
        kn = new_ref[0][g:g + 1, :]
        vn = new_ref[0][N_KV + g:N_KV + g + 1, :]
        k = jnp.where(is_new, jnp.where(first, kn, 0.0), blk_refs[g][0, :, g, :])
        v = jnp.where(is_new, jnp.where(first, vn, 0.0), blk_refs[g][0, :, N_KV + g, :])
        _attn_update(g, q_ref[0, g].astype(BF16), k.astype(BF16), v.astype(BF16), bias, valid,
                     m_ref, l_ref, acc_ref)

    @pl.when(j == pl.num_programs(1) - 1)
    def _():
        for g in range(N_KV):
            o = acc_ref[g] / jnp.maximum(l_ref[g], 1e-30)
            os_ref[0, g] = _sigmoid(gate_ref[0, 0, g]) * o


def _sel_sample(page_table, idx, rel_bias, q8, cache_half, kvs_new, gates8, q_pos):
    b = q8.shape[0]
    n_past = q_pos // CMP_BLOCK
    per_page = PAGE_SIZE // CMP_BLOCK
    bias = _sel_bias(rel_bias, q_pos, n_past + 1)

    def blk_spec(g):
        def index(i, j, pt, ix):
            n = jnp.minimum(ix[i, g, j], n_past - 1)
            return (pt[i, n // per_page] * per_page + n % per_page, 0, 0, 0)
        return pl.BlockSpec((1, CMP_BLOCK, KV_ROWS, HEAD_DIM), index)

    hspec = pl.BlockSpec((1, N_KV, HROWS, HEAD_DIM), lambda i, j, pt, ix: (i, 0, 0, 0))
    grid_spec = pltpu.PrefetchScalarGridSpec(
        num_scalar_prefetch=2,
        grid=(b, N_SEL),
        in_specs=[pl.BlockSpec(bias.shape, lambda i, j, pt, ix: (0, 0, 0, 0)), hspec]
        + [blk_spec(g) for g in range(N_KV)]
        + [pl.BlockSpec((1, KV_ROWS, HEAD_DIM), lambda i, j, pt, ix: (i, 0, 0)),
           pl.BlockSpec((1, 1, N_KV, HROWS, 1), lambda i, j, pt, ix: (i, 1, 0, 0, 0))],
        out_specs=hspec,
        scratch_shapes=[pltpu.VMEM((N_KV, HROWS, 1), F32), pltpu.VMEM((N_KV, HROWS, 1), F32),
                        pltpu.VMEM((N_KV, HROWS, HEAD_DIM), F32)],
    )
    return pl.pallas_call(
        functools.partial(_sel_sample_kernel, q_pos=q_pos, n_past=n_past),
        out_shape=jax.ShapeDtypeStruct((b, N_KV, HROWS, HEAD_DIM), F32),
        grid_spec=grid_spec,
        compiler_params=_cparams(2),
        name="sel_sample",
    )(page_table, idx, bias, q8, *([cache_half] * N_KV), kvs_new, gates8)


def _win_sample_kernel(tab_ref, q_ref, win_ref, new_ref, gate_ref, oc_ref, os_ref,
                       yb_ref, kall_ref, vall_ref):
    wc = win_ref.shape[1]
    nk = kall_ref.shape[0]
    kidx = lax.broadcasted_iota(I32, (1, nk), 1)
    dist = wc - kidx
    valid = jnp.where(dist >= 0, dist, WINDOW) < WINDOW
    bkt = _bucket(dist)
    kall_ref[...] = jnp.zeros(kall_ref.shape, F32)
    vall_ref[...] = jnp.zeros(vall_ref.shape, F32)
    for g in range(N_KV):
        kall_ref[0:wc, :] = win_ref[0, :, g, :]
        vall_ref[0:wc, :] = win_ref[0, :, N_KV + g, :]
        kall_ref[wc:wc + 1, :] = new_ref[0][g:g + 1, :]
        vall_ref[wc:wc + 1, :] = new_ref[0][N_KV + g:N_KV + g + 1, :]
        biases = _bias_lookup(bkt, tab_ref, [g * HPG + h for h in range(HPG)])
        rows = lax.broadcasted_iota(I32, (HROWS, nk), 0)
        bias = jnp.zeros((HROWS, nk), F32)
        for h in range(HPG):
            bias = jnp.where(rows == h, biases[h], bias)
        s = _dot_nt(q_ref[0, g].astype(BF16), kall_ref[...].astype(BF16)) * SCALE + bias
        p = _masked_softmax(s, valid)
        o = _dot(p.astype(BF16), vall_ref[...].astype(BF16))
        yb_ref[0, g] = oc_ref[0, g] + os_ref[0, g] + _sigmoid(gate_ref[0, 0, g]) * o


def _win_sample(rel_bias, q8, win, kvw_new, gates8, oc, osel):
    b, wc = win.shape[:2]
    nk = wc + HEAD_DIM
    hspec = pl.BlockSpec((1, N_KV, HROWS, HEAD_DIM), lambda i: (i, 0, 0, 0))
    return pl.pallas_call(
        _win_sample_kernel,
        out_shape=jax.ShapeDtypeStruct((b, N_KV, HROWS, HEAD_DIM), F32),
        grid=(b,),
        in_specs=[pl.BlockSpec(memory_space=pltpu.SMEM),
                  hspec,
                  pl.BlockSpec((1, wc, KV_ROWS, HEAD_DIM), lambda i: (i, 0, 0, 0)),
                  pl.BlockSpec((1, KV_ROWS, HEAD_DIM), lambda i: (i, 0, 0)),
                  pl.BlockSpec((1, 1, N_KV, HROWS, 1), lambda i: (i, 2, 0, 0, 0)),
                  hspec, hspec],
        out_specs=hspec,
        scratch_shapes=[pltpu.VMEM((nk, HEAD_DIM), F32), pltpu.VMEM((nk, HEAD_DIM), F32)],
        compiler_params=_cparams(1),
        name="win_sample",
    )(rel_bias, q8, win, kvw_new, gates8, oc, osel)


def _in_proj(x, norm_w, w_in_t, *, tm, tm_norm):
    xn = _rmsnorm(x, norm_w, BF16, tm_norm)
    return _in_proj_all(xn, w_in_t, tm=tm, tn=IN_TN)


def _out_and_ffn(x, ya, yb, z, w_proj_a, w_proj_b, w_out, norm_ffn, w_gate, w_up, w_down,
                 norm_final, *, tm, tm_norm):
    d_ff = w_gate.shape[1]
    mix = _mix(ya, yb, w_proj_a, w_proj_b, z, tm=tm, tn=256)
    h = _mm(mix, w_out, n_cols=D_MODEL, tm=tm, tn=256, res=x, name="out_proj")
    hn = _rmsnorm(h, norm_ffn, BF16, tm_norm)
    ff = _swiglu(hn, w_gate, w_up, tm=tm, tn=256)
    half = d_ff // 2
    y = _mm(ff, w_down, n_cols=D_MODEL, tm=tm, tn=256, res=h, k_block=0, tk=half, name="ffn_down0")
    y = _mm(ff, w_down, n_cols=D_MODEL, tm=tm, tn=256, res=y, k_block=1, tk=half, name="ffn_down1")
    return _rmsnorm(y, norm_final, F32, tm_norm)


def kernel(x_prompt, x_sample, cache_cmp_kv, cache_sel_kv, cache_win_kv, state_rglru_h, state_conv,
           page_table, rel_bias, norm_mix, w_in, conv_w, conv_b, lru_wa, lru_ba, lru_wi, lru_bi,
           lru_lambda, nsa_w_cmp, w_proj_a, w_proj_b, w_out, norm_ffn, w_gate, w_up, w_down,
           norm_final):
    depth = w_in.shape[0]
    assert depth == 1, "single-layer trunk"
    bp, tp, _ = x_prompt.shape
    bs, ts, _ = x_sample.shape
    assert ts == 1
    n_pool = cache_cmp_kv.shape[1]
    past_len = page_table.shape[1] * PAGE_SIZE
    wc = cache_win_kv.shape[2]

    w_in0 = jnp.swapaxes(w_in[0], 0, 1)
    wexp = jnp.broadcast_to(nsa_w_cmp[0][..., None], (2, N_KV, CMP_BLOCK, HEAD_DIM))
    w8 = jnp.broadcast_to(nsa_w_cmp[0].reshape(KV_ROWS, CMP_BLOCK).T[..., None],
                          (CMP_BLOCK, KV_ROWS, HEAD_DIM))
    layer_w = (conv_w[0], conv_b[0], lru_wa[0], lru_ba[0], lru_wi[0], lru_bi[0], lru_lambda[0])
    tail_w = (w_proj_a[0], w_proj_b[0], w_out[0], norm_ffn[0], w_gate[0], w_up[0], w_down[0],
              norm_final)

    mp = bp * tp
    xp = x_prompt.reshape(mp, D_MODEL)
    z = _in_proj(xp, norm_mix[0], w_in0, tm=1024, tm_norm=256)
    ya, conv_p, h_p = _rglru_prompt(z, bp, tp, *layer_w, tt=128)
    gn_t = z[:, Z_GN:Z_GN + 3 * N_HEADS].reshape(mp, 3, N_KV, HPG).transpose(1, 2, 3, 0)
    bd, bo = _bias_tiles(rel_bias, ATT_T)
    oc, sel = _cmp_prompt(z, rel_bias, wexp, gn_t, bp, tp)
    osel = _sel_prompt(z, rel_bias, bd, bo, gn_t, sel, bp, tp)
    yb = _win_prompt(z, rel_bias, bd, bo, gn_t, oc, osel, bp, tp)
    y_prompt = _out_and_ffn(xp, ya, yb, z, *tail_w, tm=1024, tm_norm=256)

    kv_shape = (1, bp, tp // PAGE_SIZE, PAGE_SIZE, 2, N_KV, HEAD_DIM)
    cmp_p = z[:, OFF_KVC:OFF_KVS].reshape(kv_shape)
    sel_p = z[:, OFF_KVS:OFF_KVW].reshape(kv_shape)
    wlen = min(WINDOW, tp)
    win_p = z[:, OFF_KVW:MAIN_W].reshape(bp, tp, 2, N_KV, HEAD_DIM)[None, :, tp - wlen:]

    xs = x_sample.reshape(bs, D_MODEL)
    zs = _in_proj(xs, norm_mix[0], w_in0, tm=bs, tm_norm=bs)
    buf_t = state_conv[0].transpose(1, 0, 2)
    ya_s, h_s = _rglru_step(zs, buf_t, state_rglru_h[0], *layer_w)

    q8 = jnp.pad(zs[:, OFF_Q:OFF_KVC].reshape(bs, N_KV, HPG, HEAD_DIM),
                 ((0, 0), (0, 0), (0, HROWS - HPG), (0, 0)))
    gates8 = jnp.pad(zs[:, Z_GN:Z_GN + 3 * N_HEADS].reshape(bs, 3, N_KV, HPG),
                     ((0, 0), (0, 0), (0, 0), (0, HROWS - HPG)))[..., None]
    kvc_new = zs[:, OFF_KVC:OFF_KVS].reshape(bs, KV_ROWS, HEAD_DIM)
    kvs_new = zs[:, OFF_KVS:OFF_KVW].reshape(bs, KV_ROWS, HEAD_DIM)
    kvw_new = zs[:, OFF_KVW:MAIN_W].reshape(bs, KV_ROWS, HEAD_DIM)
    per_page = PAGE_SIZE // CMP_BLOCK
    cache_c = cache_cmp_kv.reshape(n_pool, PAGE_SIZE, KV_ROWS, HEAD_DIM)
    cache_s = cache_sel_kv.reshape(n_pool * per_page, CMP_BLOCK, KV_ROWS, HEAD_DIM)
    win = cache_win_kv.reshape(bs, wc, KV_ROWS, HEAD_DIM)

    kvc = _cmp_pages(cache_c, page_table, w8)
    oc_s, idx = _cmp_sample(rel_bias, q8, kvc, kvc_new, w8, gates8, past_len)
    os_s = _sel_sample(page_table, idx, rel_bias, q8, cache_s, kvs_new, gates8, past_len)
    yb8 = _win_sample(rel_bias, q8, win, kvw_new, gates8, oc_s, os_s)
    yb_s = yb8[:, :, :HPG].reshape(bs, Q_W).astype(BF16)
    y_sample = _out_and_ffn(xs, ya_s, yb_s, zs, *tail_w, tm=bs, tm_norm=bs)

    row_shape = (1, bs, 1, 2, N_KV, HEAD_DIM)
    cmp_s = kvc_new.reshape(row_shape)
    sel_s = kvs_new.reshape(row_shape)
    win_s = jnp.concatenate([win[:, 1:], kvw_new[:, None]], axis=1).reshape(1, bs, wc, 2, N_KV, HEAD_DIM)
    conv_s = jnp.concatenate([state_conv[0], zs[:, None, OFF_XR:OFF_GR]], axis=1)[:, 1:][None]

    return (y_prompt.reshape(bp, tp, D_MODEL), y_sample.reshape(bs, ts, D_MODEL),
            cmp_p, sel_p, win_p, h_p.reshape(1, bp, D_RNN), conv_p[None],
            cmp_s, sel_s, win_s, h_s[None], conv_s)
```

```python
import functools
import math

import jax
import jax.numpy as jnp
from jax import lax
from jax.experimental import pallas as pl
from jax.experimental.pallas import tpu as pltpu

F32 = jnp.float32
BF16 = jnp.bfloat16
I32 = jnp.int32

D_MODEL = 4096
D_RNN = 2048
N_RNN_BLOCKS = 16
RNN_BLOCK = D_RNN // N_RNN_BLOCKS
CONV_W = 4
LRU_C = 8.0
N_HEADS = 16
HEAD_DIM = 128
N_KV = 4
HPG = N_HEADS // N_KV
CMP_BLOCK = 64
N_SEL = 16
WINDOW = 512
N_BUCKETS = 32
MAX_DISTANCE = 128
PAGE_SIZE = 128
Q_W = N_HEADS * HEAD_DIM
KV_W = 2 * N_KV * HEAD_DIM
EPS = 1e-6
NEG_INF = -1e30
FORCE = 1e9
SCALE = HEAD_DIM ** -0.5

OFF_XR = 0
OFF_GR = D_RNN
OFF_Q = 2 * D_RNN
OFF_KVC = OFF_Q + Q_W
OFF_KVS = OFF_KVC + KV_W
OFF_KVW = OFF_KVS + KV_W
OFF_GN = OFF_KVW + KV_W
OFF_GA = OFF_GN + 3 * N_HEADS
MAIN_W = OFF_GN
IN_TN = 512
Z_GN = MAIN_W
Z_GA = MAIN_W + IN_TN
Z_GB = Z_GA + D_MODEL

ATT_T = 256
HROWS = 8
KV_ROWS = 2 * N_KV
PAGES_PER_STEP = 8
SEL_BLOCKS_PER_STEP = 4
VMEM_LIMIT = 56 * 1024 * 1024


def _cparams(n_axes):
    return pltpu.CompilerParams(dimension_semantics=("arbitrary",) * n_axes,
                                vmem_limit_bytes=VMEM_LIMIT)


def _sigmoid(x):
    return 1.0 / (1.0 + jnp.exp(-x))


def _gelu_tanh(x):
    return 0.5 * x * (1.0 + jnp.tanh(math.sqrt(2.0 / math.pi) * (x + 0.044715 * (x * x * x))))


def _dot(a, b):
    return jnp.dot(a, b, preferred_element_type=F32)


def _dot_nt(a, b):
    return lax.dot_general(a, b, (((1,), (1,)), ((), ())), preferred_element_type=F32)


def _bucket(dist):
    n = jnp.maximum(dist, 0)
    exact = N_BUCKETS // 2
    nf = jnp.maximum(n, 1).astype(F32)
    large = exact + (jnp.log(nf / exact) / math.log(MAX_DISTANCE / exact)
                     * (N_BUCKETS - exact)).astype(I32)
    return jnp.where(n < exact, n, jnp.minimum(large, N_BUCKETS - 1))


def _bias_lookup(bkt, tab_ref, cols):
    outs = [jnp.zeros(bkt.shape, F32) for _ in cols]
    for b in range(N_BUCKETS):
        eq = bkt == b
        outs = [jnp.where(eq, tab_ref[b, c], o) for c, o in zip(cols, outs)]
    return outs


def _block_scores(imp, blk, cur):
    score = jnp.where(blk == 0, FORCE, jnp.where(blk == cur, FORCE,
                                                 jnp.where(blk == cur - 1, FORCE, imp)))
    return jnp.where(blk <= cur, score, -FORCE)


def _ahead(col, score, later):
    return jnp.where(later, jnp.where(col >= score, 1.0, 0.0), jnp.where(col > score, 1.0, 0.0))


def _masked_softmax(s, valid):
    s = jnp.where(valid, s, NEG_INF)
    m = jnp.max(s, axis=-1, keepdims=True)
    e = jnp.where(valid, jnp.exp(s - m), 0.0)
    return e / jnp.maximum(jnp.sum(e, axis=-1, keepdims=True), 1e-30)


def _rmsnorm_kernel(x_ref, g_ref, o_ref):
    x = x_ref[...]
    y = x * lax.rsqrt(jnp.mean(x * x, axis=-1, keepdims=True) + EPS)
    o_ref[...] = (y * g_ref[...]).astype(o_ref.dtype)


def _rmsnorm(x, g, out_dtype, tm):
    m, d = x.shape
    return pl.pallas_call(
        _rmsnorm_kernel,
        out_shape=jax.ShapeDtypeStruct((m, d), out_dtype),
        grid=(m // tm,),
        in_specs=[pl.BlockSpec((tm, d), lambda i: (i, 0)),
                  pl.BlockSpec((1, d), lambda i: (0, 0))],
        out_specs=pl.BlockSpec((tm, d), lambda i: (i, 0)),
        compiler_params=_cparams(1),
        name="rmsnorm",
    )(x, g.reshape(1, d))


def _mm_kernel(a_ref, w_ref, o_ref):
    o_ref[...] = _dot(a_ref[...], w_ref[...].astype(BF16)).astype(o_ref.dtype)


def _mm_res_kernel(a_ref, w_ref, r_ref, o_ref):
    o_ref[...] = r_ref[...] + _dot(a_ref[...], w_ref[...].astype(BF16))


def _mm(a, w, *, n_cols, tm, tn, out_dtype=F32, res=None, k_block=0, tk=None, name="mm"):
    m = a.shape[0]
    tk = a.shape[1] if tk is None else tk
    grid = (m // tm, n_cols // tn)
    in_specs = [pl.BlockSpec((tm, tk), lambda i, j: (i, k_block)),
                pl.BlockSpec((tk, tn), lambda i, j: (k_block, j))]
    args = [a, w]
    kern = _mm_kernel
    if res is not None:
        in_specs.append(pl.BlockSpec((tm, tn), lambda i, j: (i, j)))
        args.append(res)
        kern = _mm_res_kernel
    return pl.pallas_call(
        kern,
        out_shape=jax.ShapeDtypeStruct((m, n_cols), out_dtype),
        grid=grid,
        in_specs=in_specs,
        out_specs=pl.BlockSpec((tm, tn), lambda i, j: (i, j)),
        compiler_params=_cparams(2),
        name=name,
    )(*args)


def _in_proj_kernel(a_ref, wt_ref, o_ref):
    o_ref[...] = _dot_nt(a_ref[...], wt_ref[...].astype(BF16))


def _in_proj_all(a, wt, *, tm, tn):
    m, k = a.shape
    assert MAIN_W % tn == 0 and Z_GN == MAIN_W and Z_GA == MAIN_W + tn and 3 * N_HEADS <= tn
    n_gate_blocks = 2 * D_MODEL // tn
    nb = MAIN_W // tn + 1 + n_gate_blocks
    assert OFF_GA + n_gate_blocks * tn == wt.shape[0]

    def w_index(i, j):
        row = jnp.where(j <= MAIN_W // tn, j * tn, OFF_GA + (j - MAIN_W // tn - 1) * tn)
        return (pl.multiple_of(row, 8), 0)

    return pl.pallas_call(
        _in_proj_kernel,
        out_shape=jax.ShapeDtypeStruct((m, nb * tn), F32),
        grid=(m // tm, nb),
        in_specs=[pl.BlockSpec((tm, k), lambda i, j: (i, 0)),
                  pl.BlockSpec((pl.Element(tn), pl.Element(k)), w_index)],
        out_specs=pl.BlockSpec((tm, tn), lambda i, j: (i, j)),
        compiler_params=_cparams(2),
        name="in_proj",
    )(a, wt)


def _swiglu_kernel(a_ref, wg_ref, wu_ref, o_ref):
    a = a_ref[...]
    g = _dot(a, wg_ref[...].astype(BF16))
    u = _dot(a, wu_ref[...].astype(BF16))
    o_ref[...] = ((g * _sigmoid(g)) * u).astype(o_ref.dtype)


def _swiglu(a, wg, wu, *, tm, tn):
    m, k = a.shape
    n = wg.shape[1]
    return pl.pallas_call(
        _swiglu_kernel,
        out_shape=jax.ShapeDtypeStruct((m, n), BF16),
        grid=(m // tm, n // tn),
        in_specs=[pl.BlockSpec((tm, k), lambda i, j: (i, 0)),
                  pl.BlockSpec((k, tn), lambda i, j: (0, j)),
                  pl.BlockSpec((k, tn), lambda i, j: (0, j))],
        out_specs=pl.BlockSpec((tm, tn), lambda i, j: (i, j)),
        compiler_params=_cparams(2),
        name="ffn_gate_up",
    )(a, wg, wu)


def _mix_kernel(ya_ref, yb_ref, wa_ref, wb_ref, ga_ref, gb_ref, o_ref):
    pa = _dot(ya_ref[...], wa_ref[...].astype(BF16))
    pb = _dot(yb_ref[...], wb_ref[...].astype(BF16))
    o_ref[...] = (_sigmoid(ga_ref[...]) * pa + _sigmoid(gb_ref[...]) * pb).astype(o_ref.dtype)


def _mix(ya, yb, wa, wb, z, *, tm, tn):
    m, k = ya.shape
    n = wa.shape[1]
    assert Z_GA % tn == 0 and Z_GB % tn == 0
    gate = lambda off: pl.BlockSpec((tm, tn), lambda i, j: (i, j + off // tn))
    return pl.pallas_call(
        _mix_kernel,
        out_shape=jax.ShapeDtypeStruct((m, n), BF16),
        grid=(m // tm, n // tn),
        in_specs=[pl.BlockSpec((tm, k), lambda i, j: (i, 0)),
                  pl.BlockSpec((tm, k), lambda i, j: (i, 0)),
                  pl.BlockSpec((k, tn), lambda i, j: (0, j)),
                  pl.BlockSpec((k, tn), lambda i, j: (0, j)),
                  gate(Z_GA), gate(Z_GB)],
        out_specs=pl.BlockSpec((tm, tn), lambda i, j: (i, j)),
        compiler_params=_cparams(2),
        name="branch_merge",
    )(ya, yb, wa, wb, z, z)


def _lru_gates(xc, wa_ref, ba_ref, wi_ref, bi_ref, lam_ref, a_ref, u_ref):
    for n in range(N_RNN_BLOCKS):
        sl = slice(n * RNN_BLOCK, (n + 1) * RNN_BLOCK)
        xn = xc[:, sl]
        xb = xn.astype(BF16)
        r = _sigmoid(_dot(xb, wa_ref[n].astype(BF16)) + ba_ref[:, sl])
        i = _sigmoid(_dot(xb, wi_ref[n].astype(BF16)) + bi_ref[:, sl])
        neg_lam = -lam_ref[:, sl]
        softplus = jnp.maximum(neg_lam, 0.0) + jnp.log1p(jnp.exp(-jnp.abs(neg_lam)))
        log_a = (-LRU_C * r) * softplus
        a = jnp.exp(log_a)
        a_ref[:, sl] = a
        u_ref[:, sl] = jnp.sqrt(-jnp.tanh(log_a) * (a * a + 1.0)) * (i * xn)


def _rglru_prompt_kernel(xr_ref, gr_ref, cw_ref, cb_ref, wa_ref, ba_ref, wi_ref, bi_ref, lam_ref,
                         ya_ref, conv_ref, hl_ref, xp_ref, a_ref, u_ref, h_ref, carry_ref):
    ti = pl.program_id(1)
    tt, c = xr_ref.shape

    @pl.when(ti == 0)
    def _():
        xp_ref[0:8, :] = jnp.zeros((8, c), F32)
        carry_ref[...] = jnp.zeros((1, c), F32)

    x = xr_ref[...]
    xp_ref[8:8 + tt, :] = x
    xc = cb_ref[...]
    for k in range(CONV_W):
        xc = xc + cw_ref[k:k + 1, :] * xp_ref[5 + k:5 + k + tt, :]
    xp_ref[0:8, :] = xp_ref[tt:tt + 8, :]

    _lru_gates(xc, wa_ref, ba_ref, wi_ref, bi_ref, lam_ref, a_ref, u_ref)

    row = lax.broadcasted_iota(I32, (8, c), 0)

    def chunk(ci, carry):
        off = pl.multiple_of(ci * 8, 8)
        a = a_ref[pl.ds(off, 8), :]
        b = u_ref[pl.ds(off, 8), :]
        for s in (1, 2, 4):
            keep = row >= s
            b = jnp.where(keep, a * pltpu.roll(b, s, 0) + b, b)
            a = jnp.where(keep, a * pltpu.roll(a, s, 0), a)
        h = a * carry + b
        h_ref[pl.ds(off, 8), :] = h
        return h[7:8, :]

    carry = lax.fori_loop(0, tt // 8, chunk, carry_ref[...])
    carry_ref[...] = carry
    ya_ref[...] = (h_ref[...] * _gelu_tanh(gr_ref[...])).astype(ya_ref.dtype)

    @pl.when(ti == pl.num_programs(1) - 1)
    def _():
        conv_ref[0] = x[tt - (CONV_W - 1):, :]
        hl_ref[0] = carry


def _rglru_prompt(z, batch, seq, conv_w, conv_b, lru_wa, lru_ba, lru_wi, lru_bi, lam, *, tt):
    nt = seq // tt
    c = D_RNN
    row = lambda b, t: (b * nt + t, 0)
    full2 = lambda b, t: (0, 0)
    full3 = lambda b, t: (0, 0, 0)
    return pl.pallas_call(
        _rglru_prompt_kernel,
        out_shape=(jax.ShapeDtypeStruct((batch * seq, c), BF16),
                   jax.ShapeDtypeStruct((batch, CONV_W - 1, c), F32),
                   jax.ShapeDtypeStruct((batch, 1, c), F32)),
        grid=(batch, nt),
        in_specs=[pl.BlockSpec((tt, c), row),
                  pl.BlockSpec((tt, c), lambda b, t: (b * nt + t, 1)),
                  pl.BlockSpec((CONV_W, c), full2),
                  pl.BlockSpec((1, c), full2),
                  pl.BlockSpec((N_RNN_BLOCKS, RNN_BLOCK, RNN_BLOCK), full3),
                  pl.BlockSpec((1, c), full2),
                  pl.BlockSpec((N_RNN_BLOCKS, RNN_BLOCK, RNN_BLOCK), full3),
                  pl.BlockSpec((1, c), full2),
                  pl.BlockSpec((1, c), full2)],
        out_specs=(pl.BlockSpec((tt, c), row),
                   pl.BlockSpec((1, CONV_W - 1, c), lambda b, t: (b, 0, 0)),
                   pl.BlockSpec((1, 1, c), lambda b, t: (b, 0, 0))),
        scratch_shapes=[pltpu.VMEM((tt + 8, c), F32), pltpu.VMEM((tt, c), F32),
                        pltpu.VMEM((tt, c), F32), pltpu.VMEM((tt, c), F32),
                        pltpu.VMEM((1, c), F32)],
        compiler_params=_cparams(2),
        name="rglru_prompt",
    )(z, z, conv_w, conv_b.reshape(1, c), lru_wa, lru_ba.reshape(1, c), lru_wi,
      lru_bi.reshape(1, c), lam.reshape(1, c))


def _rglru_step_kernel(xr_ref, gr_ref, buf_ref, h0_ref, cw_ref, cb_ref, wa_ref, ba_ref, wi_ref,
                       bi_ref, lam_ref, ya_ref, hn_ref, a_ref, u_ref):
    x = xr_ref[...]
    xc = cb_ref[...]
    for k in range(CONV_W - 1):
        xc = xc + cw_ref[k:k + 1, :] * buf_ref[k]
    xc = xc + cw_ref[CONV_W - 1:CONV_W, :] * x
    _lru_gates(xc, wa_ref, ba_ref, wi_ref, bi_ref, lam_ref, a_ref, u_ref)
    h = a_ref[...] * h0_ref[...] + u_ref[...]
    hn_ref[...] = h
    ya_ref[...] = (h * _gelu_tanh(gr_ref[...])).astype(ya_ref.dtype)


def _rglru_step(z, buf_t, h0, conv_w, conv_b, lru_wa, lru_ba, lru_wi, lru_bi, lam):
    b, c = h0.shape
    full2 = lambda i: (0, 0)
    full3 = lambda i: (0, 0, 0)
    return pl.pallas_call(
        _rglru_step_kernel,
        out_shape=(jax.ShapeDtypeStruct((b, c), BF16), jax.ShapeDtypeStruct((b, c), F32)),
        grid=(1,),
        in_specs=[pl.BlockSpec((b, c), full2),
                  pl.BlockSpec((b, c), lambda i: (0, 1)),
                  pl.BlockSpec((CONV_W - 1, b, c), full3),
                  pl.BlockSpec((b, c), full2),
                  pl.BlockSpec((CONV_W, c), full2),
                  pl.BlockSpec((1, c), full2),
                  pl.BlockSpec((N_RNN_BLOCKS, RNN_BLOCK, RNN_BLOCK), full3),
                  pl.BlockSpec((1, c), full2),
                  pl.BlockSpec((N_RNN_BLOCKS, RNN_BLOCK, RNN_BLOCK), full3),
                  pl.BlockSpec((1, c), full2),
                  pl.BlockSpec((1, c), full2)],
        out_specs=(pl.BlockSpec((b, c), full2), pl.BlockSpec((b, c), full2)),
        scratch_shapes=[pltpu.VMEM((b, c), F32), pltpu.VMEM((b, c), F32)],
        compiler_params=_cparams(1),
        name="rglru_step",
    )(z, z, buf_t, h0, conv_w, conv_b.reshape(1, c), lru_wa, lru_ba.reshape(1, c), lru_wi,
      lru_bi.reshape(1, c), lam.reshape(1, c))


def _bias_tiles_kernel(tab_ref, bd_ref, bo_ref):
    h = pl.program_id(0)
    t = bd_ref.shape[1]
    d0 = lax.broadcasted_iota(I32, (t, t), 1) - lax.broadcasted_iota(I32, (t, t), 0)
    bd_ref[0] = jnp.where(d0 >= 0, _bias_lookup(_bucket(d0), tab_ref, [h])[0], NEG_INF)
    bo_ref[0] = _bias_lookup(_bucket(d0 + t), tab_ref, [h])[0]


def _bias_tiles(rel_bias, t):
    shp = jax.ShapeDtypeStruct((N_HEADS, t, t), F32)
    spec = pl.BlockSpec((1, t, t), lambda h: (h, 0, 0))
    return pl.pallas_call(
        _bias_tiles_kernel,
        out_shape=(shp, shp),
        grid=(N_HEADS,),
        in_specs=[pl.BlockSpec(memory_space=pltpu.SMEM)],
        out_specs=(spec, spec),
        compiler_params=_cparams(1),
        name="bias_tiles",
    )(rel_bias)


def _cmp_prompt_kernel(tab_ref, q_ref, kc_ref, vc_ref, wk_ref, wv_ref, gate_ref,
                       oc_ref, sel_ref, kcmp_ref, vcmp_ref, vcmp_t_ref):
    g = pl.program_id(1)
    qi = pl.program_id(2)
    tq = q_ref.shape[0]
    seq = kc_ref.shape[0]
    nblk = seq // CMP_BLOCK

    @pl.when(qi == 0)
    def _():
        k3 = kc_ref[...].reshape(nblk, CMP_BLOCK, HEAD_DIM)
        v3 = vc_ref[...].reshape(nblk, CMP_BLOCK, HEAD_DIM)
        kcmp_ref[...] = jnp.sum(k3 * wk_ref[0, 0][None], axis=1).astype(BF16)
        vcmp_ref[...] = jnp.zeros(vcmp_ref.shape, F32)
        vcmp_ref[0:nblk, :] = jnp.sum(v3 * wv_ref[0, 0][None], axis=1)
        vcmp_t_ref[...] = jnp.transpose(vcmp_ref[...]).astype(BF16)

    tpos = qi * tq + lax.broadcasted_iota(I32, (nblk, tq), 1)
    blk = lax.broadcasted_iota(I32, (nblk, tq), 0)
    dist = tpos - (blk * CMP_BLOCK + (CMP_BLOCK - 1))
    valid = dist >= 0
    biases = _bias_lookup(_bucket(dist), tab_ref, [g * HPG + h for h in range(HPG)])
    kb = kcmp_ref[...]
    vt = vcmp_t_ref[:, 0:nblk]
    gates = _sigmoid(gate_ref[0, 0])
    imp = jnp.zeros((nblk, tq), F32)
    for h in range(HPG):
        sl = slice(h * HEAD_DIM, (h + 1) * HEAD_DIM)
        s = _dot_nt(kb, q_ref[:, sl].astype(BF16)) * SCALE + biases[h]
        s = jnp.where(valid, s, NEG_INF)
        m = jnp.max(s, axis=0, keepdims=True)
        e = jnp.where(valid, jnp.exp(s - m), 0.0)
        p = e / jnp.maximum(jnp.sum(e, axis=0, keepdims=True), 1e-30)
        imp = imp + p
        o_t = _dot(vt, p.astype(BF16)) * gates[h:h + 1, :]
        oc_ref[:, sl] = jnp.transpose(o_t)

    cur = jnp.right_shift(tpos, 6)
    score = _block_scores(imp, blk, cur)
    rank = jnp.zeros((nblk, tq), F32)
    for j in range(nblk):
        rank = rank + _ahead(score[j:j + 1, :], score, blk > j)
    sel_ref[0, 0] = jnp.where(rank < float(min(N_SEL, nblk)), 0.0, NEG_INF)


def _cmp_prompt(z, rel_bias, wexp, gn_t, batch, seq):
    tq = ATT_T
    nq = seq // tq
    nblk = seq // CMP_BLOCK
    m = batch * seq
    qcol = OFF_Q // (HPG * HEAD_DIM)
    kcol = OFF_KVC // HEAD_DIM
    return pl.pallas_call(
        _cmp_prompt_kernel,
        out_shape=(jax.ShapeDtypeStruct((m, Q_W), F32),
                   jax.ShapeDtypeStruct((batch, N_KV, nblk, seq), F32)),
        grid=(batch, N_KV, nq),
        in_specs=[pl.BlockSpec(memory_space=pltpu.SMEM),
                  pl.BlockSpec((tq, HPG * HEAD_DIM), lambda b, g, i: (b * nq + i, qcol + g)),
                  pl.BlockSpec((seq, HEAD_DIM), lambda b, g, i: (b, kcol + g)),
                  pl.BlockSpec((seq, HEAD_DIM), lambda b, g, i: (b, kcol + N_KV + g)),
                  pl.BlockSpec((1, 1, CMP_BLOCK, HEAD_DIM), lambda b, g, i: (0, g, 0, 0)),
                  pl.BlockSpec((1, 1, CMP_BLOCK, HEAD_DIM), lambda b, g, i: (1, g, 0, 0)),
                  pl.BlockSpec((1, 1, HPG, tq), lambda b, g, i: (0, g, 0, b * nq + i))],
        out_specs=(pl.BlockSpec((tq, HPG * HEAD_DIM), lambda b, g, i: (b * nq + i, g)),
                   pl.BlockSpec((1, 1, nblk, tq), lambda b, g, i: (b, g, 0, i))),
        scratch_shapes=[pltpu.VMEM((nblk, HEAD_DIM), BF16), pltpu.VMEM((HEAD_DIM, HEAD_DIM), F32),
                        pltpu.VMEM((HEAD_DIM, HEAD_DIM), BF16)],
        compiler_params=_cparams(3),
        name="cmp_prompt",
    )(rel_bias, z, z, z, wexp, wexp, gn_t)


def _attn_init(m_ref, l_ref, acc_ref):
    m_ref[...] = jnp.full(m_ref.shape, NEG_INF, F32)
    l_ref[...] = jnp.zeros(l_ref.shape, F32)
    acc_ref[...] = jnp.zeros(acc_ref.shape, F32)


def _attn_update(h, qh, k, v, bias, valid, m_ref, l_ref, acc_ref):
    s = _dot_nt(qh, k) * SCALE + bias
    if valid is not None:
        s = jnp.where(valid, s, NEG_INF)
    m_old = m_ref[h]
    m_new = jnp.maximum(m_old, jnp.max(s, axis=-1, keepdims=True))
    alpha = jnp.exp(m_old - m_new)
    e = jnp.exp(s - m_new)
    if valid is not None:
        e = jnp.where(valid, e, 0.0)
    l_ref[h] = alpha * l_ref[h] + jnp.sum(e, axis=-1, keepdims=True)
    acc_ref[h] = alpha * acc_ref[h] + _dot(e.astype(BF16), v)
    m_ref[h] = m_new


def _attn_prologue(qi, q_ref, k_ref, v_ref, kb_ref, vt_ref, qb_ref, m_ref, l_ref, acc_ref):
    n_tiles, t = kb_ref.shape[0], kb_ref.shape[1]

    @pl.when(qi == 0)
    def _():
        for kj in range(n_tiles):
            rows = slice(kj * t, (kj + 1) * t)
            kb_ref[kj] = k_ref[rows, :].astype(BF16)
            vt_ref[kj] = jnp.transpose(v_ref[rows, :]).astype(BF16)

    for h in range(HPG):
        qb_ref[h] = q_ref[:, h * HEAD_DIM:(h + 1) * HEAD_DIM].astype(BF16)
    _attn_init(m_ref, l_ref, acc_ref)


def _attn_update_t(k, vt, qb_ref, biases, m_ref, l_ref, acc_ref, valid=None):
    heads = range(len(biases))
    scores = [_dot_nt(k, qb_ref[h]) for h in heads]
    scores = [s * SCALE + b for s, b in zip(scores, biases)]
    if valid is not None:
        scores = [jnp.where(valid, s, NEG_INF) for s in scores]
    m_old = [m_ref[h] for h in heads]
    m_new = [jnp.maximum(mo, jnp.max(s, axis=0, keepdims=True)) for mo, s in zip(m_old, scores)]
    alpha = [jnp.exp(mo - mn) for mo, mn in zip(m_old, m_new)]
    e = [jnp.exp(s - mn) for s, mn in zip(scores, m_new)]
    for h in heads:
        l_ref[h] = alpha[h] * l_ref[h] + jnp.sum(e[h], axis=0, keepdims=True)
        m_ref[h] = m_new[h]
    pv = [_dot(vt, e[h].astype(BF16)) for h in heads]
    for h in heads:
        acc_ref[h] = alpha[h] * acc_ref[h] + pv[h]


def _attn_output_t(h, gates, l_ref, acc_ref):
    o_t = acc_ref[h] / jnp.maximum(l_ref[h], 1e-30) * gates[h:h + 1, :]
    return jnp.transpose(o_t)


def _sel_prompt_kernel(tab_ref, q_ref, k_ref, v_ref, bd_ref, bo_ref, gate_ref, sel_ref,
                       os_ref, kb_ref, vt_ref, qb_ref, m_ref, l_ref, acc_ref):
    g = pl.program_id(1)
    qi = pl.program_id(2)
    t = q_ref.shape[0]
    per_tile = t // CMP_BLOCK
    _attn_prologue(qi, q_ref, k_ref, v_ref, kb_ref, vt_ref, qb_ref, m_ref, l_ref, acc_ref)

    def tile(kj, kind):
        rows = [jnp.broadcast_to(sel_ref[0, 0, pl.ds(kj * per_tile + i, 1), :], (CMP_BLOCK, t))
                for i in range(per_tile)]
        mask = jnp.concatenate(rows, axis=0)
        if kind == "far":
            biases = [mask + tab_ref[N_BUCKETS - 1, g * HPG + h] for h in range(HPG)]
        elif kind == "off":
            biases = [mask + bo_ref[h] for h in range(HPG)]
        else:
            biases = [mask + bd_ref[h] for h in range(HPG)]
        _attn_update_t(kb_ref[kj], vt_ref[kj], qb_ref, biases, m_ref, l_ref, acc_ref)

    def far_body(kj, carry):
        tile(kj, "far")
        return carry

    lax.fori_loop(0, jnp.maximum(qi - 1, 0), far_body, 0)

    @pl.when(qi >= 1)
    def _():
        tile(qi - 1, "off")

    tile(qi, "diag")
    gates = _sigmoid(gate_ref[0, 0])
    for h in range(HPG):
        os_ref[:, h * HEAD_DIM:(h + 1) * HEAD_DIM] = _attn_output_t(h, gates, l_ref, acc_ref)


def _win_prompt_kernel(tab_ref, q_ref, k_ref, v_ref, bd_ref, bo_ref, gate_ref, oc_ref, os_ref,
                       yb_ref, kb_ref, vt_ref, qb_ref, m_ref, l_ref, acc_ref):
    g = pl.program_id(1)
    qi = pl.program_id(2)
    t = q_ref.shape[0]
    _attn_prologue(qi, q_ref, k_ref, v_ref, kb_ref, vt_ref, qb_ref, m_ref, l_ref, acc_ref)
    key = lax.broadcasted_iota(I32, (t, t), 0)
    qry = lax.broadcasted_iota(I32, (t, t), 1)

    def tile(kj, kind):
        if kind == "far":
            biases = [tab_ref[N_BUCKETS - 1, g * HPG + h] for h in range(HPG)]
            valid = qry < key
        elif kind == "off":
            biases, valid = [bo_ref[h] for h in range(HPG)], None
        else:
            biases, valid = [bd_ref[h] for h in range(HPG)], None
        _attn_update_t(kb_ref[kj], vt_ref[kj], qb_ref, biases, m_ref, l_ref, acc_ref, valid)

    tile(qi, "diag")

    @pl.when(qi >= 1)
    def _():
        tile(qi - 1, "off")

    @pl.when(qi >= 2)
    def _():
        tile(qi - 2, "far")

    gates = _sigmoid(gate_ref[0, 0])
    for h in range(HPG):
        sl = slice(h * HEAD_DIM, (h + 1) * HEAD_DIM)
        o = _attn_output_t(h, gates, l_ref, acc_ref)
        yb_ref[:, sl] = (oc_ref[:, sl] + os_ref[:, sl] + o).astype(yb_ref.dtype)


def _attn_prompt_specs(batch, seq, kv_off, gate_idx):
    t = ATT_T
    nq = seq // t
    qcol = OFF_Q // (HPG * HEAD_DIM)
    kcol = kv_off // HEAD_DIM
    rowblk = lambda b, g, i: (b * nq + i, g)
    specs = [pl.BlockSpec(memory_space=pltpu.SMEM),
             pl.BlockSpec((t, HPG * HEAD_DIM), lambda b, g, i: (b * nq + i, qcol + g)),
             pl.BlockSpec((seq, HEAD_DIM), lambda b, g, i: (b, kcol + g)),
             pl.BlockSpec((seq, HEAD_DIM), lambda b, g, i: (b, kcol + N_KV + g)),
             pl.BlockSpec((HPG, t, t), lambda b, g, i: (g, 0, 0)),
             pl.BlockSpec((HPG, t, t), lambda b, g, i: (g, 0, 0)),
             pl.BlockSpec((1, 1, HPG, t), lambda b, g, i: (gate_idx, g, 0, b * nq + i))]
    scratch = [pltpu.VMEM((seq // t, t, HEAD_DIM), BF16), pltpu.VMEM((seq // t, HEAD_DIM, t), BF16),
               pltpu.VMEM((HPG, t, HEAD_DIM), BF16),
               pltpu.VMEM((HPG, 1, t), F32), pltpu.VMEM((HPG, 1, t), F32),
               pltpu.VMEM((HPG, HEAD_DIM, t), F32)]
    return specs, scratch, pl.BlockSpec((t, HPG * HEAD_DIM), rowblk), (batch, N_KV, nq)


def _sel_prompt(z, rel_bias, bd, bo, gn_t, sel, batch, seq):
    specs, scratch, out_spec, grid = _attn_prompt_specs(batch, seq, OFF_KVS, 1)
    t = ATT_T
    nblk = seq // CMP_BLOCK
    specs += [pl.BlockSpec((1, 1, nblk, t), lambda b, g, i: (b, g, 0, i))]
    return pl.pallas_call(
        _sel_prompt_kernel,
        out_shape=jax.ShapeDtypeStruct((batch * seq, Q_W), F32),
        grid=grid, in_specs=specs, out_specs=out_spec, scratch_shapes=scratch,
        compiler_params=_cparams(3), name="sel_prompt",
    )(rel_bias, z, z, z, bd, bo, gn_t, sel)


def _win_prompt(z, rel_bias, bd, bo, gn_t, oc, osel, batch, seq):
    specs, scratch, out_spec, grid = _attn_prompt_specs(batch, seq, OFF_KVW, 2)
    specs += [out_spec, out_spec]
    return pl.pallas_call(
        _win_prompt_kernel,
        out_shape=jax.ShapeDtypeStruct((batch * seq, Q_W), BF16),
        grid=grid, in_specs=specs, out_specs=out_spec, scratch_shapes=scratch,
        compiler_params=_cparams(3), name="win_prompt",
    )(rel_bias, z, z, z, bd, bo, gn_t, oc, osel)


def _cmp_pages_kernel(pt_ref, *refs):
    page_refs, w_ref, o_ref = refs[:-2], refs[-2], refs[-1]
    per_page = PAGE_SIZE // CMP_BLOCK
    w = w_ref[...]
    for k, page_ref in enumerate(page_refs):
        x = page_ref[0].reshape(per_page, CMP_BLOCK, KV_ROWS, HEAD_DIM)
        o_ref[0, per_page * k:per_page * (k + 1)] = jnp.sum(x * w[None], axis=1)


def _cmp_pages(cache, page_table, w8):
    b, n_pages = page_table.shape
    per_page = PAGE_SIZE // CMP_BLOCK
    pps = PAGES_PER_STEP
    page = lambda k: pl.BlockSpec((1, PAGE_SIZE, KV_ROWS, HEAD_DIM),
                                  lambda i, p, pt: (pt[i, p * pps + k], 0, 0, 0))
    grid_spec = pltpu.PrefetchScalarGridSpec(
        num_scalar_prefetch=1,
        grid=(b, n_pages // pps),
        in_specs=[page(k) for k in range(pps)]
        + [pl.BlockSpec((CMP_BLOCK, KV_ROWS, HEAD_DIM), lambda i, p, pt: (0, 0, 0))],
        out_specs=pl.BlockSpec((1, pps * per_page, KV_ROWS, HEAD_DIM), lambda i, p, pt: (i, p, 0, 0)),
    )
    return pl.pallas_call(
        _cmp_pages_kernel,
        out_shape=jax.ShapeDtypeStruct((b, n_pages * per_page, KV_ROWS, HEAD_DIM), F32),
        grid_spec=grid_spec,
        compiler_params=_cparams(2),
        name="cmp_pages",
    )(page_table, *([cache] * pps), w8)


def _cmp_sample_kernel(tab_ref, q_ref, kvc_ref, new_ref, w_ref, gate_ref,
                       oc_ref, idx_ref, kall_ref, vall_ref, imp_ref, *, q_pos):
    nblk = kvc_ref.shape[1]
    nk = kall_ref.shape[0]
    blk = lax.broadcasted_iota(I32, (1, nk), 1)
    dist = q_pos - (blk * CMP_BLOCK + (CMP_BLOCK - 1))
    valid = dist >= 0
    bkt = _bucket(dist)
    new_c = new_ref[0] * w_ref[0]
    kall_ref[...] = jnp.zeros(kall_ref.shape, F32)
    vall_ref[...] = jnp.zeros(vall_ref.shape, F32)
    for g in range(N_KV):
        kall_ref[0:nblk, :] = kvc_ref[0, :, g, :]
        vall_ref[0:nblk, :] = kvc_ref[0, :, N_KV + g, :]
        kall_ref[nblk:nblk + 1, :] = new_c[g:g + 1, :]
        vall_ref[nblk:nblk + 1, :] = new_c[N_KV + g:N_KV + g + 1, :]
        kb = kall_ref[...].astype(BF16)
        vb = vall_ref[...].astype(BF16)
        biases = _bias_lookup(bkt, tab_ref, [g * HPG + h for h in range(HPG)])
        rows = lax.broadcasted_iota(I32, (HROWS, nk), 0)
        bias = jnp.zeros((HROWS, nk), F32)
        for h in range(HPG):
            bias = jnp.where(rows == h, biases[h], bias)
        s = _dot_nt(q_ref[0, g].astype(BF16), kb) * SCALE + bias
        p = _masked_softmax(s, valid)
        gate = _sigmoid(gate_ref[0, 0, g])
        oc_ref[0, g] = gate * _dot(p.astype(BF16), vb)
        head_row = jnp.where(rows < HPG, p, 0.0)
        imp_ref[g:g + 1, :] = jnp.sum(head_row, axis=0, keepdims=True)

    imp = imp_ref[0:N_KV, :]
    blk4 = lax.broadcasted_iota(I32, (N_KV, nk), 1)
    cur = q_pos // CMP_BLOCK
    score = _block_scores(imp, blk4, cur)
    rank = jnp.zeros((N_KV, nk), F32)
    for j in range(nblk + 1):
        col = score[:, j:j + 1]
        rank = rank + _ahead(col, score, blk4 > j)
    rank = jnp.where(blk4 <= nblk, rank, float(nk))
    blkf = blk4.astype(F32)
    slot = lax.broadcasted_iota(I32, (N_KV, N_SEL), 1)
    idx = jnp.zeros((N_KV, N_SEL), F32)
    for r in range(N_SEL):
        pick = jnp.sum(jnp.where(rank == float(r), blkf, 0.0), axis=-1, keepdims=True)
        idx = jnp.where(slot == r, pick, idx)
    idx_ref[0] = idx.astype(I32)


def _cmp_sample(rel_bias, q8, kvc, kvc_new, w8, gates8, q_pos):
    b, nblk = kvc.shape[:2]
    nk = 2 * HEAD_DIM
    assert nblk + 1 <= nk and q_pos // CMP_BLOCK == nblk
    return pl.pallas_call(
        functools.partial(_cmp_sample_kernel, q_pos=q_pos),
        out_shape=(jax.ShapeDtypeStruct((b, N_KV, HROWS, HEAD_DIM), F32),
                   jax.ShapeDtypeStruct((b, N_KV, N_SEL), I32)),
        grid=(b,),
        in_specs=[pl.BlockSpec(memory_space=pltpu.SMEM),
                  pl.BlockSpec((1, N_KV, HROWS, HEAD_DIM), lambda i: (i, 0, 0, 0)),
                  pl.BlockSpec((1, nblk, KV_ROWS, HEAD_DIM), lambda i: (i, 0, 0, 0)),
                  pl.BlockSpec((1, KV_ROWS, HEAD_DIM), lambda i: (i, 0, 0)),
                  pl.BlockSpec((CMP_BLOCK, KV_ROWS, HEAD_DIM), lambda i: (0, 0, 0)),
                  pl.BlockSpec((1, 1, N_KV, HROWS, 1), lambda i: (i, 0, 0, 0, 0))],
        out_specs=(pl.BlockSpec((1, N_KV, HROWS, HEAD_DIM), lambda i: (i, 0, 0, 0)),
                   pl.BlockSpec((1, N_KV, N_SEL), lambda i: (i, 0, 0))),
        scratch_shapes=[pltpu.VMEM((nk, HEAD_DIM), F32), pltpu.VMEM((nk, HEAD_DIM), F32),
                        pltpu.VMEM((HROWS, nk), F32)],
        compiler_params=_cparams(1),
        name="cmp_sample",
    )(rel_bias, q8, kvc, kvc_new, w8, gates8)


def _sel_bias_kernel(tab_ref, o_ref, *, q_pos):
    g = pl.program_id(0)
    shape = o_ref.shape[1:]
    blk = lax.broadcasted_iota(I32, shape, 0)
    rows = lax.broadcasted_iota(I32, shape, 1)
    offs = lax.broadcasted_iota(I32, shape, 2)
    dist = q_pos - (blk * CMP_BLOCK + offs)
    biases = _bias_lookup(_bucket(dist), tab_ref, [g * HPG + h for h in range(HPG)])
    bias = jnp.zeros(shape, F32)
    for h in range(HPG):
        bias = jnp.where(rows == h, biases[h], bias)
    o_ref[0] = jnp.where(dist >= 0, bias, NEG_INF)


def _sel_bias(rel_bias, q_pos, n_blocks):
    return pl.pallas_call(
        functools.partial(_sel_bias_kernel, q_pos=q_pos),
        out_shape=jax.ShapeDtypeStruct((N_KV, n_blocks, HROWS, CMP_BLOCK), F32),
        grid=(N_KV,),
        in_specs=[pl.BlockSpec(memory_space=pltpu.SMEM)],
        out_specs=pl.BlockSpec((1, n_blocks, HROWS, CMP_BLOCK), lambda g: (g, 0, 0, 0)),
        compiler_params=_cparams(1),
        name="sel_bias",
    )(rel_bias)


def _sel_sample_kernel(pt_ref, idx_ref, bias_ref, q_ref, *refs, n_past):
    nb = SEL_BLOCKS_PER_STEP
    blk_refs = refs[:N_KV * nb]
    new_ref, gate_ref, os_ref, kcat_ref, vcat_ref, bcat_ref, m_ref, l_ref, acc_ref = refs[N_KV * nb:]
    b = pl.program_id(0)
    step = pl.program_id(1)

    @pl.when(step == 0)
    def _():
        _attn_init(m_ref, l_ref, acc_ref)

    first = lax.broadcasted_iota(I32, (CMP_BLOCK, HEAD_DIM), 0) == 0
    for g in range(N_KV):
        kn = jnp.where(first, new_ref[0][g:g + 1, :], 0.0)
        vn = jnp.where(first, new_ref[0][N_KV + g:N_KV + g + 1, :], 0.0)
        for j in range(nb):
            n = idx_ref[b, g, step * nb + j]
            is_new = n >= n_past
            blk_ref = blk_refs[g * nb + j]
            keys = slice(j * CMP_BLOCK, (j + 1) * CMP_BLOCK)
            kcat_ref[keys, :] = jnp.where(is_new, kn, blk_ref[0, :, g, :]).astype(BF16)
            vcat_ref[keys, :] = jnp.where(is_new, vn, blk_ref[0, :, N_KV + g, :]).astype(BF16)
            bcat_ref[:, keys] = bias_ref[g, n]
        _attn_update(g, q_ref[0, g].astype(BF16), kcat_ref[...], vcat_ref[...], bcat_ref[...], None,
                     m_ref, l_ref, acc_ref)

    @pl.when(step == pl.num_programs(1) - 1)
    def _():
        for g in range(N_KV):
            o = acc_ref[g] / jnp.maximum(l_ref[g], 1e-30)
            os_ref[0, g] = _sigmoid(gate_ref[0, 0, g]) * o


def _sel_sample(page_table, idx, rel_bias, q8, cache_half, kvs_new, gates8, q_pos):
    b = q8.shape[0]
    nb = SEL_BLOCKS_PER_STEP
    n_past = q_pos // CMP_BLOCK
    per_page = PAGE_SIZE // CMP_BLOCK
    bias = _sel_bias(rel_bias, q_pos, n_past + 1)

    def blk_spec(g, j):
        def index(i, s, pt, ix):
            n = jnp.minimum(ix[i, g, s * nb + j], n_past - 1)
            return (pt[i, n // per_page] * per_page + n % per_page, 0, 0, 0)
        return pl.BlockSpec((1, CMP_BLOCK, KV_ROWS, HEAD_DIM), index)

    hspec = pl.BlockSpec((1, N_KV, HROWS, HEAD_DIM), lambda i, s, pt, ix: (i, 0, 0, 0))
    grid_spec = pltpu.PrefetchScalarGridSpec(
        num_scalar_prefetch=2,
        grid=(b, N_SEL // nb),
        in_specs=[pl.BlockSpec(bias.shape, lambda i, s, pt, ix: (0, 0, 0, 0)), hspec]
        + [blk_spec(g, j) for g in range(N_KV) for j in range(nb)]
        + [pl.BlockSpec((1, KV_ROWS, HEAD_DIM), lambda i, s, pt, ix: (i, 0, 0)),
           pl.BlockSpec((1, 1, N_KV, HROWS, 1), lambda i, s, pt, ix: (i, 1, 0, 0, 0))],
        out_specs=hspec,
        scratch_shapes=[pltpu.VMEM((nb * CMP_BLOCK, HEAD_DIM), BF16),
                        pltpu.VMEM((nb * CMP_BLOCK, HEAD_DIM), BF16),
                        pltpu.VMEM((HROWS, nb * CMP_BLOCK), F32),
                        pltpu.VMEM((N_KV, HROWS, 1), F32), pltpu.VMEM((N_KV, HROWS, 1), F32),
                        pltpu.VMEM((N_KV, HROWS, HEAD_DIM), F32)],
    )
    return pl.pallas_call(
        functools.partial(_sel_sample_kernel, n_past=n_past),
        out_shape=jax.ShapeDtypeStruct((b, N_KV, HROWS, HEAD_DIM), F32),
        grid_spec=grid_spec,
        compiler_params=_cparams(2),
        name="sel_sample",
    )(page_table, idx, bias, q8, *([cache_half] * (N_KV * nb)), kvs_new, gates8)


def _win_sample_kernel(tab_ref, q_ref, win_ref, new_ref, gate_ref, oc_ref, os_ref,
                       yb_ref, nwin_ref, kall_ref, vall_ref):
    wc = win_ref.shape[1]
    nk = kall_ref.shape[0]
    nwin_ref[0, 0:wc - 1] = win_ref[0, 1:wc]
    nwin_ref[0, wc - 1] = new_ref[0]
    kidx = lax.broadcasted_iota(I32, (1, nk), 1)
    dist = wc - kidx
    valid = jnp.where(dist >= 0, dist, WINDOW) < WINDOW
    bkt = _bucket(dist)
    kall_ref[...] = jnp.zeros(kall_ref.shape, F32)
    vall_ref[...] = jnp.zeros(vall_ref.shape, F32)
    for g in range(N_KV):
        kall_ref[0:wc, :] = win_ref[0, :, g, :]
        vall_ref[0:wc, :] = win_ref[0, :, N_KV + g, :]
        kall_ref[wc:wc + 1, :] = new_ref[0][g:g + 1, :]
        vall_ref[wc:wc + 1, :] = new_ref[0][N_KV + g:N_KV + g + 1, :]
        biases = _bias_lookup(bkt, tab_ref, [g * HPG + h for h in range(HPG)])
        rows = lax.broadcasted_iota(I32, (HROWS, nk), 0)
        bias = jnp.zeros((HROWS, nk), F32)
        for h in range(HPG):
            bias = jnp.where(rows == h, biases[h], bias)
        s = _dot_nt(q_ref[0, g].astype(BF16), kall_ref[...].astype(BF16)) * SCALE + bias
        p = _masked_softmax(s, valid)
        o = _dot(p.astype(BF16), vall_ref[...].astype(BF16))
        yb_ref[0, g] = oc_ref[0, g] + os_ref[0, g] + _sigmoid(gate_ref[0, 0, g]) * o


def _win_sample(rel_bias, q8, win, kvw_new, gates8, oc, osel):
    b, wc = win.shape[:2]
    nk = wc + HEAD_DIM
    hspec = pl.BlockSpec((1, N_KV, HROWS, HEAD_DIM), lambda i: (i, 0, 0, 0))
    wspec = pl.BlockSpec((1, wc, KV_ROWS, HEAD_DIM), lambda i: (i, 0, 0, 0))
    return pl.pallas_call(
        _win_sample_kernel,
        out_shape=(jax.ShapeDtypeStruct((b, N_KV, HROWS, HEAD_DIM), F32),
                   jax.ShapeDtypeStruct(win.shape, F32)),
        grid=(b,),
        in_specs=[pl.BlockSpec(memory_space=pltpu.SMEM),
                  hspec,
                  wspec,
                  pl.BlockSpec((1, KV_ROWS, HEAD_DIM), lambda i: (i, 0, 0)),
                  pl.BlockSpec((1, 1, N_KV, HROWS, 1), lambda i: (i, 2, 0, 0, 0)),
                  hspec, hspec],
        out_specs=(hspec, wspec),
        scratch_shapes=[pltpu.VMEM((nk, HEAD_DIM), F32), pltpu.VMEM((nk, HEAD_DIM), F32)],
        compiler_params=_cparams(1),
        name="win_sample",
    )(rel_bias, q8, win, kvw_new, gates8, oc, osel)


def _in_proj(x, norm_w, w_in_t, *, tm, tm_norm):
    xn = _rmsnorm(x, norm_w, BF16, tm_norm)
    return _in_proj_all(xn, w_in_t, tm=tm, tn=IN_TN)


def _out_and_ffn(x, ya, yb, z, w_proj_a, w_proj_b, w_out, norm_ffn, w_gate, w_up, w_down,
                 norm_final, *, tm, tm_norm):
    d_ff = w_gate.shape[1]
    mix = _mix(ya, yb, w_proj_a, w_proj_b, z, tm=tm, tn=256)
    h = _mm(mix, w_out, n_cols=D_MODEL, tm=tm, tn=256, res=x, name="out_proj")
    hn = _rmsnorm(h, norm_ffn, BF16, tm_norm)
    ff = _swiglu(hn, w_gate, w_up, tm=tm, tn=256)
    half = d_ff // 2
    y = _mm(ff, w_down, n_cols=D_MODEL, tm=tm, tn=256, res=h, k_block=0, tk=half, name="ffn_down0")
    y = _mm(ff, w_down, n_cols=D_MODEL, tm=tm, tn=256, res=y, k_block=1, tk=half, name="ffn_down1")
    return _rmsnorm(y, norm_final, F32, tm_norm)


def kernel(x_prompt, x_sample, cache_cmp_kv, cache_sel_kv, cache_win_kv, state_rglru_h, state_conv,
           page_table, rel_bias, norm_mix, w_in, conv_w, conv_b, lru_wa, lru_ba, lru_wi, lru_bi,
           lru_lambda, nsa_w_cmp, w_proj_a, w_proj_b, w_out, norm_ffn, w_gate, w_up, w_down,
           norm_final):
    depth = w_in.shape[0]
    assert depth == 1, "single-layer trunk"
    bp, tp, _ = x_prompt.shape
    bs, ts, _ = x_sample.shape
    assert ts == 1
    n_pool = cache_cmp_kv.shape[1]
    past_len = page_table.shape[1] * PAGE_SIZE
    wc = cache_win_kv.shape[2]

    w_in0 = jnp.swapaxes(w_in[0], 0, 1)
    wexp = jnp.broadcast_to(nsa_w_cmp[0][..., None], (2, N_KV, CMP_BLOCK, HEAD_DIM))
    w8 = jnp.broadcast_to(nsa_w_cmp[0].reshape(KV_ROWS, CMP_BLOCK).T[..., None],
                          (CMP_BLOCK, KV_ROWS, HEAD_DIM))
    layer_w = (conv_w[0], conv_b[0], lru_wa[0], lru_ba[0], lru_wi[0], lru_bi[0], lru_lambda[0])
    tail_w = (w_proj_a[0], w_proj_b[0], w_out[0], norm_ffn[0], w_gate[0], w_up[0], w_down[0],
              norm_final)

    mp = bp * tp
    xp = x_prompt.reshape(mp, D_MODEL)
    z = _in_proj(xp, norm_mix[0], w_in0, tm=1024, tm_norm=256)
    ya, conv_p, h_p = _rglru_prompt(z, bp, tp, *layer_w, tt=128)
    gn_t = z[:, Z_GN:Z_GN + 3 * N_HEADS].reshape(mp, 3, N_KV, HPG).transpose(1, 2, 3, 0)
    bd, bo = _bias_tiles(rel_bias, ATT_T)
    oc, sel = _cmp_prompt(z, rel_bias, wexp, gn_t, bp, tp)
    osel = _sel_prompt(z, rel_bias, bd, bo, gn_t, sel, bp, tp)
    yb = _win_prompt(z, rel_bias, bd, bo, gn_t, oc, osel, bp, tp)
    y_prompt = _out_and_ffn(xp, ya, yb, z, *tail_w, tm=1024, tm_norm=256)

    kv_shape = (1, bp, tp // PAGE_SIZE, PAGE_SIZE, 2, N_KV, HEAD_DIM)
    cmp_p = z[:, OFF_KVC:OFF_KVS].reshape(kv_shape)
    sel_p = z[:, OFF_KVS:OFF_KVW].reshape(kv_shape)
    wlen = min(WINDOW, tp)
    win_p = z[:, OFF_KVW:MAIN_W].reshape(bp, tp, 2, N_KV, HEAD_DIM)[None, :, tp - wlen:]

    xs = x_sample.reshape(bs, D_MODEL)
    zs = _in_proj(xs, norm_mix[0], w_in0, tm=bs, tm_norm=bs)
    buf_t = state_conv[0].transpose(1, 0, 2)
    ya_s, h_s = _rglru_step(zs, buf_t, state_rglru_h[0], *layer_w)

    q8 = jnp.pad(zs[:, OFF_Q:OFF_KVC].reshape(bs, N_KV, HPG, HEAD_DIM),
                 ((0, 0), (0, 0), (0, HROWS - HPG), (0, 0)))
    gates8 = jnp.pad(zs[:, Z_GN:Z_GN + 3 * N_HEADS].reshape(bs, 3, N_KV, HPG),
                     ((0, 0), (0, 0), (0, 0), (0, HROWS - HPG)))[..., None]
    kvc_new = zs[:, OFF_KVC:OFF_KVS].reshape(bs, KV_ROWS, HEAD_DIM)
    kvs_new = zs[:, OFF_KVS:OFF_KVW].reshape(bs, KV_ROWS, HEAD_DIM)
    kvw_new = zs[:, OFF_KVW:MAIN_W].reshape(bs, KV_ROWS, HEAD_DIM)
    per_page = PAGE_SIZE // CMP_BLOCK
    cache_c = cache_cmp_kv.reshape(n_pool, PAGE_SIZE, KV_ROWS, HEAD_DIM)
    cache_s = cache_sel_kv.reshape(n_pool * per_page, CMP_BLOCK, KV_ROWS, HEAD_DIM)
    win = cache_win_kv.reshape(bs, wc, KV_ROWS, HEAD_DIM)

    kvc = _cmp_pages(cache_c, page_table, w8)
    oc_s, idx = _cmp_sample(rel_bias, q8, kvc, kvc_new, w8, gates8, past_len)
    os_s = _sel_sample(page_table, idx, rel_bias, q8, cache_s, kvs_new, gates8, past_len)
    yb8, win_next = _win_sample(rel_bias, q8, win, kvw_new, gates8, oc_s, os_s)
    yb_s = yb8[:, :, :HPG].reshape(bs, Q_W).astype(BF16)
    y_sample = _out_and_ffn(xs, ya_s, yb_s, zs, *tail_w, tm=bs, tm_norm=bs)

    row_shape = (1, bs, 1, 2, N_KV, HEAD_DIM)
    cmp_s = kvc_new.reshape(row_shape)
    sel_s = kvs_new.reshape(row_shape)
    win_s = win_next.reshape(1, bs, wc, 2, N_KV, HEAD_DIM)
    conv_s = jnp.concatenate([state_conv[0], zs[:, None, OFF_XR:OFF_GR]], axis=1)[:, 1:][None]

    return (y_prompt.reshape(bp, tp, D_MODEL), y_sample.reshape(bs, ts, D_MODEL),
            cmp_p, sel_p, win_p, h_p.reshape(1, bp, D_RNN), conv_p[None],
            cmp_s, sel_s, win_s, h_s[None], conv_s)
```

```python
import functools
import math

import jax
import jax.numpy as jnp
from jax import lax
from jax.experimental import pallas as pl
from jax.experimental.pallas import tpu as pltpu

F32 = jnp.float32
BF16 = jnp.bfloat16
I32 = jnp.int32

D_MODEL = 4096
D_RNN = 2048
N_RNN_BLOCKS = 16
RNN_BLOCK = D_RNN // N_RNN_BLOCKS
CONV_W = 4
LRU_C = 8.0
N_HEADS = 16
HEAD_DIM = 128
N_KV = 4
HPG = N_HEADS // N_KV
CMP_BLOCK = 64
N_SEL = 16
WINDOW = 512
N_BUCKETS = 32
MAX_DISTANCE = 128
PAGE_SIZE = 128
Q_W = N_HEADS * HEAD_DIM
KV_W = 2 * N_KV * HEAD_DIM
EPS = 1e-6
NEG_INF = -1e30
FORCE = 1e9
SCALE = HEAD_DIM ** -0.5

OFF_XR = 0
OFF_GR = D_RNN
OFF_Q = 2 * D_RNN
OFF_KVC = OFF_Q + Q_W
OFF_KVS = OFF_KVC + KV_W
OFF_KVW = OFF_KVS + KV_W
OFF_GN = OFF_KVW + KV_W
OFF_GA = OFF_GN + 3 * N_HEADS
MAIN_W = OFF_GN
IN_TN = 512
Z_GN = MAIN_W
Z_GA = MAIN_W + IN_TN
Z_GB = Z_GA + D_MODEL

ATT_T = 256
HROWS = 8
KV_ROWS = 2 * N_KV
PAGES_PER_STEP = 8
SEL_BLOCKS_PER_STEP = 4
VMEM_LIMIT = 56 * 1024 * 1024


def _cparams(n_axes):
    return pltpu.CompilerParams(dimension_semantics=("arbitrary",) * n_axes,
                                vmem_limit_bytes=VMEM_LIMIT)


def _sigmoid(x):
    return 1.0 / (1.0 + jnp.exp(-x))


def _gelu_tanh(x):
    return 0.5 * x * (1.0 + jnp.tanh(math.sqrt(2.0 / math.pi) * (x + 0.044715 * (x * x * x))))


def _dot(a, b):
    return jnp.dot(a, b, preferred_element_type=F32)


def _dot_nt(a, b):
    return lax.dot_general(a, b, (((1,), (1,)), ((), ())), preferred_element_type=F32)


def _bucket(dist):
    n = jnp.maximum(dist, 0)
    exact = N_BUCKETS // 2
    nf = jnp.maximum(n, 1).astype(F32)
    large = exact + (jnp.log(nf / exact) / math.log(MAX_DISTANCE / exact)
                     * (N_BUCKETS - exact)).astype(I32)
    return jnp.where(n < exact, n, jnp.minimum(large, N_BUCKETS - 1))


def _bias_lookup(bkt, tab_ref, cols):
    outs = [jnp.zeros(bkt.shape, F32) for _ in cols]
    for b in range(N_BUCKETS):
        eq = bkt == b
        outs = [jnp.where(eq, tab_ref[b, c], o) for c, o in zip(cols, outs)]
    return outs


def _block_scores(imp, blk, cur):
    score = jnp.where(blk == 0, FORCE, jnp.where(blk == cur, FORCE,
                                                 jnp.where(blk == cur - 1, FORCE, imp)))
    return jnp.where(blk <= cur, score, -FORCE)


def _ahead(col, score, later):
    return jnp.where(later, jnp.where(col >= score, 1.0, 0.0), jnp.where(col > score, 1.0, 0.0))


def _masked_softmax(s, valid):
    s = jnp.where(valid, s, NEG_INF)
    m = jnp.max(s, axis=-1, keepdims=True)
    e = jnp.where(valid, jnp.exp(s - m), 0.0)
    return e / jnp.maximum(jnp.sum(e, axis=-1, keepdims=True), 1e-30)


def _rmsnorm_kernel(x_ref, g_ref, o_ref):
    x = x_ref[...]
    y = x * lax.rsqrt(jnp.mean(x * x, axis=-1, keepdims=True) + EPS)
    o_ref[...] = (y * g_ref[...]).astype(o_ref.dtype)


def _rmsnorm(x, g, out_dtype, tm):
    m, d = x.shape
    return pl.pallas_call(
        _rmsnorm_kernel,
        out_shape=jax.ShapeDtypeStruct((m, d), out_dtype),
        grid=(m // tm,),
        in_specs=[pl.BlockSpec((tm, d), lambda i: (i, 0)),
                  pl.BlockSpec((1, d), lambda i: (0, 0))],
        out_specs=pl.BlockSpec((tm, d), lambda i: (i, 0)),
        compiler_params=_cparams(1),
        name="rmsnorm",
    )(x, g.reshape(1, d))


def _mm_kernel(a_ref, w_ref, o_ref):
    o_ref[...] = _dot(a_ref[...], w_ref[...].astype(BF16)).astype(o_ref.dtype)


def _mm_res_kernel(a_ref, w_ref, r_ref, o_ref):
    o_ref[...] = r_ref[...] + _dot(a_ref[...], w_ref[...].astype(BF16))


def _mm(a, w, *, n_cols, tm, tn, out_dtype=F32, res=None, k_block=0, tk=None, name="mm"):
    m = a.shape[0]
    tk = a.shape[1] if tk is None else tk
    grid = (m // tm, n_cols // tn)
    in_specs = [pl.BlockSpec((tm, tk), lambda i, j: (i, k_block)),
                pl.BlockSpec((tk, tn), lambda i, j: (k_block, j))]
    args = [a, w]
    kern = _mm_kernel
    if res is not None:
        in_specs.append(pl.BlockSpec((tm, tn), lambda i, j: (i, j)))
        args.append(res)
        kern = _mm_res_kernel
    return pl.pallas_call(
        kern,
        out_shape=jax.ShapeDtypeStruct((m, n_cols), out_dtype),
        grid=grid,
        in_specs=in_specs,
        out_specs=pl.BlockSpec((tm, tn), lambda i, j: (i, j)),
        compiler_params=_cparams(2),
        name=name,
    )(*args)


def _in_proj_kernel(a_ref, wt_ref, o_ref):
    o_ref[...] = _dot_nt(a_ref[...], wt_ref[...].astype(BF16))


def _compress_pages(page_refs, w_ref, o_ref):
    per_page = PAGE_SIZE // CMP_BLOCK
    w = w_ref[...]
    for k, page_ref in enumerate(page_refs):
        x = page_ref[0].reshape(per_page, CMP_BLOCK, KV_ROWS, HEAD_DIM)
        o_ref[0, per_page * k:per_page * (k + 1)] = jnp.sum(x * w[None], axis=1)


def _in_proj_pages_kernel(pt_ref, a_ref, wt_ref, *refs):
    page_refs, w8_ref, o_ref, kvc_ref = refs[:-3], refs[-3], refs[-2], refs[-1]
    o_ref[...] = _dot_nt(a_ref[...], wt_ref[...].astype(BF16))
    _compress_pages(page_refs, w8_ref, kvc_ref)


def _in_proj_all(a, wt, *, tm, tn, pages=None):
    m, k = a.shape
    assert MAIN_W % tn == 0 and Z_GN == MAIN_W and Z_GA == MAIN_W + tn and 3 * N_HEADS <= tn
    n_gate_blocks = 2 * D_MODEL // tn
    nb = MAIN_W // tn + 1 + n_gate_blocks
    assert OFF_GA + n_gate_blocks * tn == wt.shape[0]

    def w_index(i, j, *_):
        row = jnp.where(j <= MAIN_W // tn, j * tn, OFF_GA + (j - MAIN_W // tn - 1) * tn)
        return (pl.multiple_of(row, 8), 0)

    grid = (m // tm, nb)
    in_specs = [pl.BlockSpec((tm, k), lambda i, j, *_: (i, 0)),
                pl.BlockSpec((pl.Element(tn), pl.Element(k)), w_index)]
    out_spec = pl.BlockSpec((tm, tn), lambda i, j, *_: (i, j))
    out_shape = jax.ShapeDtypeStruct((m, nb * tn), F32)
    if pages is None:
        return pl.pallas_call(
            _in_proj_kernel, out_shape=out_shape, grid=grid, in_specs=in_specs, out_specs=out_spec,
            compiler_params=_cparams(2), name="in_proj",
        )(a, wt)

    cache, page_table, w8 = pages
    b, n_pages = page_table.shape
    per_page = PAGE_SIZE // CMP_BLOCK
    pps = PAGES_PER_STEP
    chunks = n_pages // pps
    assert grid[0] * grid[1] >= b * chunks

    def chunk(i, j):
        t = jnp.minimum(i * nb + j, b * chunks - 1)
        return t // chunks, t % chunks

    def page_index(k):
        def index(i, j, pt):
            seq, c = chunk(i, j)
            return (pt[seq, c * pps + k], 0, 0, 0)
        return index

    def kvc_index(i, j, pt):
        seq, c = chunk(i, j)
        return (seq, c, 0, 0)

    grid_spec = pltpu.PrefetchScalarGridSpec(
        num_scalar_prefetch=1,
        grid=grid,
        in_specs=in_specs
        + [pl.BlockSpec((1, PAGE_SIZE, KV_ROWS, HEAD_DIM), page_index(k)) for k in range(pps)]
        + [pl.BlockSpec((CMP_BLOCK, KV_ROWS, HEAD_DIM), lambda i, j, pt: (0, 0, 0))],
        out_specs=(out_spec,
                   pl.BlockSpec((1, pps * per_page, KV_ROWS, HEAD_DIM), kvc_index)),
    )
    return pl.pallas_call(
        _in_proj_pages_kernel,
        out_shape=(out_shape,
                   jax.ShapeDtypeStruct((b, n_pages * per_page, KV_ROWS, HEAD_DIM), F32)),
        grid_spec=grid_spec,
        compiler_params=_cparams(2),
        name="in_proj_pages",
    )(page_table, a, wt, *([cache] * pps), w8)


def _swiglu_kernel(a_ref, wg_ref, wu_ref, o_ref):
    a = a_ref[...]
    g = _dot(a, wg_ref[...].astype(BF16))
    u = _dot(a, wu_ref[...].astype(BF16))
    o_ref[...] = ((g * _sigmoid(g)) * u).astype(o_ref.dtype)


def _swiglu(a, wg, wu, *, tm, tn):
    m, k = a.shape
    n = wg.shape[1]
    return pl.pallas_call(
        _swiglu_kernel,
        out_shape=jax.ShapeDtypeStruct((m, n), BF16),
        grid=(m // tm, n // tn),
        in_specs=[pl.BlockSpec((tm, k), lambda i, j: (i, 0)),
                  pl.BlockSpec((k, tn), lambda i, j: (0, j)),
                  pl.BlockSpec((k, tn), lambda i, j: (0, j))],
        out_specs=pl.BlockSpec((tm, tn), lambda i, j: (i, j)),
        compiler_params=_cparams(2),
        name="ffn_gate_up",
    )(a, wg, wu)


def _mix_kernel(ya_ref, yb_ref, wa_ref, wb_ref, ga_ref, gb_ref, o_ref):
    pa = _dot(ya_ref[...], wa_ref[...].astype(BF16))
    pb = _dot(yb_ref[...], wb_ref[...].astype(BF16))
    o_ref[...] = (_sigmoid(ga_ref[...]) * pa + _sigmoid(gb_ref[...]) * pb).astype(o_ref.dtype)


def _mix(ya, yb, wa, wb, z, *, tm, tn):
    m, k = ya.shape
    n = wa.shape[1]
    assert Z_GA % tn == 0 and Z_GB % tn == 0
    gate = lambda off: pl.BlockSpec((tm, tn), lambda i, j: (i, j + off // tn))
    return pl.pallas_call(
        _mix_kernel,
        out_shape=jax.ShapeDtypeStruct((m, n), BF16),
        grid=(m // tm, n // tn),
        in_specs=[pl.BlockSpec((tm, k), lambda i, j: (i, 0)),
                  pl.BlockSpec((tm, k), lambda i, j: (i, 0)),
                  pl.BlockSpec((k, tn), lambda i, j: (0, j)),
                  pl.BlockSpec((k, tn), lambda i, j: (0, j)),
                  gate(Z_GA), gate(Z_GB)],
        out_specs=pl.BlockSpec((tm, tn), lambda i, j: (i, j)),
        compiler_params=_cparams(2),
        name="branch_merge",
    )(ya, yb, wa, wb, z, z)


def _lru_gates(xc, wa_ref, ba_ref, wi_ref, bi_ref, lam_ref, a_ref, u_ref):
    for n in range(N_RNN_BLOCKS):
        sl = slice(n * RNN_BLOCK, (n + 1) * RNN_BLOCK)
        xn = xc[:, sl]
        xb = xn.astype(BF16)
        r = _sigmoid(_dot(xb, wa_ref[n].astype(BF16)) + ba_ref[:, sl])
        i = _sigmoid(_dot(xb, wi_ref[n].astype(BF16)) + bi_ref[:, sl])
        neg_lam = -lam_ref[:, sl]
        softplus = jnp.maximum(neg_lam, 0.0) + jnp.log1p(jnp.exp(-jnp.abs(neg_lam)))
        log_a = (-LRU_C * r) * softplus
        a = jnp.exp(log_a)
        a_ref[:, sl] = a
        u_ref[:, sl] = jnp.sqrt(-jnp.tanh(log_a) * (a * a + 1.0)) * (i * xn)


def _rglru_prompt_kernel(xr_ref, gr_ref, cw_ref, cb_ref, wa_ref, ba_ref, wi_ref, bi_ref, lam_ref,
                         ya_ref, conv_ref, hl_ref, xp_ref, a_ref, u_ref, h_ref, carry_ref):
    ti = pl.program_id(1)
    tt, c = xr_ref.shape

    @pl.when(ti == 0)
    def _():
        xp_ref[0:8, :] = jnp.zeros((8, c), F32)
        carry_ref[...] = jnp.zeros((1, c), F32)

    x = xr_ref[...]
    xp_ref[8:8 + tt, :] = x
    xc = cb_ref[...]
    for k in range(CONV_W):
        xc = xc + cw_ref[k:k + 1, :] * xp_ref[5 + k:5 + k + tt, :]
    xp_ref[0:8, :] = xp_ref[tt:tt + 8, :]

    _lru_gates(xc, wa_ref, ba_ref, wi_ref, bi_ref, lam_ref, a_ref, u_ref)

    row = lax.broadcasted_iota(I32, (8, c), 0)

    def chunk(ci, carry):
        off = pl.multiple_of(ci * 8, 8)
        a = a_ref[pl.ds(off, 8), :]
        b = u_ref[pl.ds(off, 8), :]
        for s in (1, 2, 4):
            keep = row >= s
            b = jnp.where(keep, a * pltpu.roll(b, s, 0) + b, b)
            a = jnp.where(keep, a * pltpu.roll(a, s, 0), a)
        h = a * carry + b
        h_ref[pl.ds(off, 8), :] = h
        return h[7:8, :]

    carry = lax.fori_loop(0, tt // 8, chunk, carry_ref[...])
    carry_ref[...] = carry
    ya_ref[...] = (h_ref[...] * _gelu_tanh(gr_ref[...])).astype(ya_ref.dtype)

    @pl.when(ti == pl.num_programs(1) - 1)
    def _():
        conv_ref[0] = x[tt - (CONV_W - 1):, :]
        hl_ref[0] = carry


def _rglru_prompt(z, batch, seq, conv_w, conv_b, lru_wa, lru_ba, lru_wi, lru_bi, lam, *, tt):
    nt = seq // tt
    c = D_RNN
    row = lambda b, t: (b * nt + t, 0)
    full2 = lambda b, t: (0, 0)
    full3 = lambda b, t: (0, 0, 0)
    return pl.pallas_call(
        _rglru_prompt_kernel,
        out_shape=(jax.ShapeDtypeStruct((batch * seq, c), BF16),
                   jax.ShapeDtypeStruct((batch, CONV_W - 1, c), F32),
                   jax.ShapeDtypeStruct((batch, 1, c), F32)),
        grid=(batch, nt),
        in_specs=[pl.BlockSpec((tt, c), row),
                  pl.BlockSpec((tt, c), lambda b, t: (b * nt + t, 1)),
                  pl.BlockSpec((CONV_W, c), full2),
                  pl.BlockSpec((1, c), full2),
                  pl.BlockSpec((N_RNN_BLOCKS, RNN_BLOCK, RNN_BLOCK), full3),
                  pl.BlockSpec((1, c), full2),
                  pl.BlockSpec((N_RNN_BLOCKS, RNN_BLOCK, RNN_BLOCK), full3),
                  pl.BlockSpec((1, c), full2),
                  pl.BlockSpec((1, c), full2)],
        out_specs=(pl.BlockSpec((tt, c), row),
                   pl.BlockSpec((1, CONV_W - 1, c), lambda b, t: (b, 0, 0)),
                   pl.BlockSpec((1, 1, c), lambda b, t: (b, 0, 0))),
        scratch_shapes=[pltpu.VMEM((tt + 8, c), F32), pltpu.VMEM((tt, c), F32),
                        pltpu.VMEM((tt, c), F32), pltpu.VMEM((tt, c), F32),
                        pltpu.VMEM((1, c), F32)],
        compiler_params=_cparams(2),
        name="rglru_prompt",
    )(z, z, conv_w, conv_b.reshape(1, c), lru_wa, lru_ba.reshape(1, c), lru_wi,
      lru_bi.reshape(1, c), lam.reshape(1, c))


def _rglru_step_kernel(xr_ref, gr_ref, buf_ref, h0_ref, cw_ref, cb_ref, wa_ref, ba_ref, wi_ref,
                       bi_ref, lam_ref, ya_ref, hn_ref, a_ref, u_ref):
    x = xr_ref[...]
    xc = cb_ref[...]
    for k in range(CONV_W - 1):
        xc = xc + cw_ref[k:k + 1, :] * buf_ref[k]
    xc = xc + cw_ref[CONV_W - 1:CONV_W, :] * x
    _lru_gates(xc, wa_ref, ba_ref, wi_ref, bi_ref, lam_ref, a_ref, u_ref)
    h = a_ref[...] * h0_ref[...] + u_ref[...]
    hn_ref[...] = h
    ya_ref[...] = (h * _gelu_tanh(gr_ref[...])).astype(ya_ref.dtype)


def _rglru_step(z, buf_t, h0, conv_w, conv_b, lru_wa, lru_ba, lru_wi, lru_bi, lam):
    b, c = h0.shape
    full2 = lambda i: (0, 0)
    full3 = lambda i: (0, 0, 0)
    return pl.pallas_call(
        _rglru_step_kernel,
        out_shape=(jax.ShapeDtypeStruct((b, c), BF16), jax.ShapeDtypeStruct((b, c), F32)),
        grid=(1,),
        in_specs=[pl.BlockSpec((b, c), full2),
                  pl.BlockSpec((b, c), lambda i: (0, 1)),
                  pl.BlockSpec((CONV_W - 1, b, c), full3),
                  pl.BlockSpec((b, c), full2),
                  pl.BlockSpec((CONV_W, c), full2),
                  pl.BlockSpec((1, c), full2),
                  pl.BlockSpec((N_RNN_BLOCKS, RNN_BLOCK, RNN_BLOCK), full3),
                  pl.BlockSpec((1, c), full2),
                  pl.BlockSpec((N_RNN_BLOCKS, RNN_BLOCK, RNN_BLOCK), full3),
                  pl.BlockSpec((1, c), full2),
                  pl.BlockSpec((1, c), full2)],
        out_specs=(pl.BlockSpec((b, c), full2), pl.BlockSpec((b, c), full2)),
        scratch_shapes=[pltpu.VMEM((b, c), F32), pltpu.VMEM((b, c), F32)],
        compiler_params=_cparams(1),
        name="rglru_step",
    )(z, z, buf_t, h0, conv_w, conv_b.reshape(1, c), lru_wa, lru_ba.reshape(1, c), lru_wi,
      lru_bi.reshape(1, c), lam.reshape(1, c))


def _bias_tiles_kernel(tab_ref, bd_ref, bo_ref):
    h = pl.program_id(0)
    t = bd_ref.shape[1]
    d0 = lax.broadcasted_iota(I32, (t, t), 1) - lax.broadcasted_iota(I32, (t, t), 0)
    bd_ref[0] = jnp.where(d0 >= 0, _bias_lookup(_bucket(d0), tab_ref, [h])[0], NEG_INF)
    bo_ref[0] = _bias_lookup(_bucket(d0 + t), tab_ref, [h])[0]


def _bias_tiles(rel_bias, t):
    shp = jax.ShapeDtypeStruct((N_HEADS, t, t), F32)
    spec = pl.BlockSpec((1, t, t), lambda h: (h, 0, 0))
    return pl.pallas_call(
        _bias_tiles_kernel,
        out_shape=(shp, shp),
        grid=(N_HEADS,),
        in_specs=[pl.BlockSpec(memory_space=pltpu.SMEM)],
        out_specs=(spec, spec),
        compiler_params=_cparams(1),
        name="bias_tiles",
    )(rel_bias)


def _cmp_prompt_kernel(tab_ref, q_ref, kc_ref, vc_ref, wk_ref, wv_ref, gate_ref,
                       oc_ref, sel_ref, kcmp_ref, vcmp_ref, vcmp_t_ref):
    g = pl.program_id(1)
    qi = pl.program_id(2)
    tq = q_ref.shape[0]
    seq = kc_ref.shape[0]
    nblk = seq // CMP_BLOCK

    @pl.when(qi == 0)
    def _():
        k3 = kc_ref[...].reshape(nblk, CMP_BLOCK, HEAD_DIM)
        v3 = vc_ref[...].reshape(nblk, CMP_BLOCK, HEAD_DIM)
        kcmp_ref[...] = jnp.sum(k3 * wk_ref[0, 0][None], axis=1).astype(BF16)
        vcmp_ref[...] = jnp.zeros(vcmp_ref.shape, F32)
        vcmp_ref[0:nblk, :] = jnp.sum(v3 * wv_ref[0, 0][None], axis=1)
        vcmp_t_ref[...] = jnp.transpose(vcmp_ref[...]).astype(BF16)

    tpos = qi * tq + lax.broadcasted_iota(I32, (nblk, tq), 1)
    blk = lax.broadcasted_iota(I32, (nblk, tq), 0)
    dist = tpos - (blk * CMP_BLOCK + (CMP_BLOCK - 1))
    valid = dist >= 0
    biases = _bias_lookup(_bucket(dist), tab_ref, [g * HPG + h for h in range(HPG)])
    kb = kcmp_ref[...]
    vt = vcmp_t_ref[:, 0:nblk]
    gates = _sigmoid(gate_ref[0, 0])
    imp = jnp.zeros((nblk, tq), F32)
    for h in range(HPG):
        sl = slice(h * HEAD_DIM, (h + 1) * HEAD_DIM)
        s = _dot_nt(kb, q_ref[:, sl].astype(BF16)) * SCALE + biases[h]
        s = jnp.where(valid, s, NEG_INF)
        m = jnp.max(s, axis=0, keepdims=True)
        e = jnp.where(valid, jnp.exp(s - m), 0.0)
        p = e / jnp.maximum(jnp.sum(e, axis=0, keepdims=True), 1e-30)
        imp = imp + p
        o_t = _dot(vt, p.astype(BF16)) * gates[h:h + 1, :]
        oc_ref[:, sl] = jnp.transpose(o_t)

    cur = jnp.right_shift(tpos, 6)
    score = _block_scores(imp, blk, cur)
    rank = jnp.zeros((nblk, tq), F32)
    for j in range(nblk):
        rank = rank + _ahead(score[j:j + 1, :], score, blk > j)
    sel_ref[0, 0] = jnp.where(rank < float(min(N_SEL, nblk)), 0.0, NEG_INF)


def _cmp_prompt(z, rel_bias, wexp, gn_t, batch, seq):
    tq = ATT_T
    nq = seq // tq
    nblk = seq // CMP_BLOCK
    m = batch * seq
    qcol = OFF_Q // (HPG * HEAD_DIM)
    kcol = OFF_KVC // HEAD_DIM
    return pl.pallas_call(
        _cmp_prompt_kernel,
        out_shape=(jax.ShapeDtypeStruct((m, Q_W), F32),
                   jax.ShapeDtypeStruct((batch, N_KV, nblk, seq), F32)),
        grid=(batch, N_KV, nq),
        in_specs=[pl.BlockSpec(memory_space=pltpu.SMEM),
                  pl.BlockSpec((tq, HPG * HEAD_DIM), lambda b, g, i: (b * nq + i, qcol + g)),
                  pl.BlockSpec((seq, HEAD_DIM), lambda b, g, i: (b, kcol + g)),
                  pl.BlockSpec((seq, HEAD_DIM), lambda b, g, i: (b, kcol + N_KV + g)),
                  pl.BlockSpec((1, 1, CMP_BLOCK, HEAD_DIM), lambda b, g, i: (0, g, 0, 0)),
                  pl.BlockSpec((1, 1, CMP_BLOCK, HEAD_DIM), lambda b, g, i: (1, g, 0, 0)),
                  pl.BlockSpec((1, 1, HPG, tq), lambda b, g, i: (0, g, 0, b * nq + i))],
        out_specs=(pl.BlockSpec((tq, HPG * HEAD_DIM), lambda b, g, i: (b * nq + i, g)),
                   pl.BlockSpec((1, 1, nblk, tq), lambda b, g, i: (b, g, 0, i))),
        scratch_shapes=[pltpu.VMEM((nblk, HEAD_DIM), BF16), pltpu.VMEM((HEAD_DIM, HEAD_DIM), F32),
                        pltpu.VMEM((HEAD_DIM, HEAD_DIM), BF16)],
        compiler_params=_cparams(3),
        name="cmp_prompt",
    )(rel_bias, z, z, z, wexp, wexp, gn_t)


def _attn_init(m_ref, l_ref, acc_ref):
    m_ref[...] = jnp.full(m_ref.shape, NEG_INF, F32)
    l_ref[...] = jnp.zeros(l_ref.shape, F32)
    acc_ref[...] = jnp.zeros(acc_ref.shape, F32)


def _attn_update(h, qh, k, v, bias, valid, m_ref, l_ref, acc_ref):
    s = _dot_nt(qh, k) * SCALE + bias
    if valid is not None:
        s = jnp.where(valid, s, NEG_INF)
    m_old = m_ref[h]
    m_new = jnp.maximum(m_old, jnp.max(s, axis=-1, keepdims=True))
    alpha = jnp.exp(m_old - m_new)
    e = jnp.exp(s - m_new)
    if valid is not None:
        e = jnp.where(valid, e, 0.0)
    l_ref[h] = alpha * l_ref[h] + jnp.sum(e, axis=-1, keepdims=True)
    acc_ref[h] = alpha * acc_ref[h] + _dot(e.astype(BF16), v)
    m_ref[h] = m_new


def _attn_prologue(qi, q_ref, k_ref, v_ref, kb_ref, vt_ref, qb_ref, m_ref, l_ref, acc_ref):
    n_tiles, t = kb_ref.shape[0], kb_ref.shape[1]

    @pl.when(qi == 0)
    def _():
        for kj in range(n_tiles):
            rows = slice(kj * t, (kj + 1) * t)
            kb_ref[kj] = k_ref[rows, :].astype(BF16)
            vt_ref[kj] = jnp.transpose(v_ref[rows, :]).astype(BF16)

    for h in range(HPG):
        qb_ref[h] = q_ref[:, h * HEAD_DIM:(h + 1) * HEAD_DIM].astype(BF16)
    _attn_init(m_ref, l_ref, acc_ref)


def _attn_update_t(k, vt, qb_ref, biases, m_ref, l_ref, acc_ref, valid=None):
    heads = range(len(biases))
    scores = [_dot_nt(k, qb_ref[h]) for h in heads]
    scores = [s * SCALE + b for s, b in zip(scores, biases)]
    if valid is not None:
        scores = [jnp.where(valid, s, NEG_INF) for s in scores]
    m_old = [m_ref[h] for h in heads]
    m_new = [jnp.maximum(mo, jnp.max(s, axis=0, keepdims=True)) for mo, s in zip(m_old, scores)]
    alpha = [jnp.exp(mo - mn) for mo, mn in zip(m_old, m_new)]
    e = [jnp.exp(s - mn) for s, mn in zip(scores, m_new)]
    for h in heads:
        l_ref[h] = alpha[h] * l_ref[h] + jnp.sum(e[h], axis=0, keepdims=True)
        m_ref[h] = m_new[h]
    pv = [_dot(vt, e[h].astype(BF16)) for h in heads]
    for h in heads:
        acc_ref[h] = alpha[h] * acc_ref[h] + pv[h]


def _attn_output_t(h, gates, l_ref, acc_ref):
    o_t = acc_ref[h] / jnp.maximum(l_ref[h], 1e-30) * gates[h:h + 1, :]
    return jnp.transpose(o_t)


def _sel_prompt_kernel(tab_ref, q_ref, k_ref, v_ref, bd_ref, bo_ref, gate_ref, sel_ref,
                       os_ref, kb_ref, vt_ref, qb_ref, m_ref, l_ref, acc_ref):
    g = pl.program_id(1)
    qi = pl.program_id(2)
    t = q_ref.shape[0]
    per_tile = t // CMP_BLOCK
    _attn_prologue(qi, q_ref, k_ref, v_ref, kb_ref, vt_ref, qb_ref, m_ref, l_ref, acc_ref)

    def tile(kj, kind):
        rows = [jnp.broadcast_to(sel_ref[0, 0, pl.ds(kj * per_tile + i, 1), :], (CMP_BLOCK, t))
                for i in range(per_tile)]
        mask = jnp.concatenate(rows, axis=0)
        if kind == "far":
            biases = [mask + tab_ref[N_BUCKETS - 1, g * HPG + h] for h in range(HPG)]
        elif kind == "off":
            biases = [mask + bo_ref[h] for h in range(HPG)]
        else:
            biases = [mask + bd_ref[h] for h in range(HPG)]
        _attn_update_t(kb_ref[kj], vt_ref[kj], qb_ref, biases, m_ref, l_ref, acc_ref)

    def far_body(kj, carry):
        tile(kj, "far")
        return carry

    lax.fori_loop(0, jnp.maximum(qi - 1, 0), far_body, 0)

    @pl.when(qi >= 1)
    def _():
        tile(qi - 1, "off")

    tile(qi, "diag")
    gates = _sigmoid(gate_ref[0, 0])
    for h in range(HPG):
        os_ref[:, h * HEAD_DIM:(h + 1) * HEAD_DIM] = _attn_output_t(h, gates, l_ref, acc_ref)


def _win_prompt_kernel(tab_ref, q_ref, k_ref, v_ref, bd_ref, bo_ref, gate_ref, oc_ref, os_ref,
                       yb_ref, kb_ref, vt_ref, qb_ref, m_ref, l_ref, acc_ref):
    g = pl.program_id(1)
    qi = pl.program_id(2)
    t = q_ref.shape[0]
    _attn_prologue(qi, q_ref, k_ref, v_ref, kb_ref, vt_ref, qb_ref, m_ref, l_ref, acc_ref)
    key = lax.broadcasted_iota(I32, (t, t), 0)
    qry = lax.broadcasted_iota(I32, (t, t), 1)

    def tile(kj, kind):
        if kind == "far":
            biases = [tab_ref[N_BUCKETS - 1, g * HPG + h] for h in range(HPG)]
            valid = qry < key
        elif kind == "off":
            biases, valid = [bo_ref[h] for h in range(HPG)], None
        else:
            biases, valid = [bd_ref[h] for h in range(HPG)], None
        _attn_update_t(kb_ref[kj], vt_ref[kj], qb_ref, biases, m_ref, l_ref, acc_ref, valid)

    tile(qi, "diag")

    @pl.when(qi >= 1)
    def _():
        tile(qi - 1, "off")

    @pl.when(qi >= 2)
    def _():
        tile(qi - 2, "far")

    gates = _sigmoid(gate_ref[0, 0])
    for h in range(HPG):
        sl = slice(h * HEAD_DIM, (h + 1) * HEAD_DIM)
        o = _attn_output_t(h, gates, l_ref, acc_ref)
        yb_ref[:, sl] = (oc_ref[:, sl] + os_ref[:, sl] + o).astype(yb_ref.dtype)


def _attn_prompt_specs(batch, seq, kv_off, gate_idx):
    t = ATT_T
    nq = seq // t
    qcol = OFF_Q // (HPG * HEAD_DIM)
    kcol = kv_off // HEAD_DIM
    rowblk = lambda b, g, i: (b * nq + i, g)
    specs = [pl.BlockSpec(memory_space=pltpu.SMEM),
             pl.BlockSpec((t, HPG * HEAD_DIM), lambda b, g, i: (b * nq + i, qcol + g)),
             pl.BlockSpec((seq, HEAD_DIM), lambda b, g, i: (b, kcol + g)),
             pl.BlockSpec((seq, HEAD_DIM), lambda b, g, i: (b, kcol + N_KV + g)),
             pl.BlockSpec((HPG, t, t), lambda b, g, i: (g, 0, 0)),
             pl.BlockSpec((HPG, t, t), lambda b, g, i: (g, 0, 0)),
             pl.BlockSpec((1, 1, HPG, t), lambda b, g, i: (gate_idx, g, 0, b * nq + i))]
    scratch = [pltpu.VMEM((seq // t, t, HEAD_DIM), BF16), pltpu.VMEM((seq // t, HEAD_DIM, t), BF16),
               pltpu.VMEM((HPG, t, HEAD_DIM), BF16),
               pltpu.VMEM((HPG, 1, t), F32), pltpu.VMEM((HPG, 1, t), F32),
               pltpu.VMEM((HPG, HEAD_DIM, t), F32)]
    return specs, scratch, pl.BlockSpec((t, HPG * HEAD_DIM), rowblk), (batch, N_KV, nq)


def _sel_prompt(z, rel_bias, bd, bo, gn_t, sel, batch, seq):
    specs, scratch, out_spec, grid = _attn_prompt_specs(batch, seq, OFF_KVS, 1)
    t = ATT_T
    nblk = seq // CMP_BLOCK
    specs += [pl.BlockSpec((1, 1, nblk, t), lambda b, g, i: (b, g, 0, i))]
    return pl.pallas_call(
        _sel_prompt_kernel,
        out_shape=jax.ShapeDtypeStruct((batch * seq, Q_W), F32),
        grid=grid, in_specs=specs, out_specs=out_spec, scratch_shapes=scratch,
        compiler_params=_cparams(3), name="sel_prompt",
    )(rel_bias, z, z, z, bd, bo, gn_t, sel)


def _win_prompt(z, rel_bias, bd, bo, gn_t, oc, osel, batch, seq):
    specs, scratch, out_spec, grid = _attn_prompt_specs(batch, seq, OFF_KVW, 2)
    specs += [out_spec, out_spec]
    return pl.pallas_call(
        _win_prompt_kernel,
        out_shape=jax.ShapeDtypeStruct((batch * seq, Q_W), BF16),
        grid=grid, in_specs=specs, out_specs=out_spec, scratch_shapes=scratch,
        compiler_params=_cparams(3), name="win_prompt",
    )(rel_bias, z, z, z, bd, bo, gn_t, oc, osel)


def _cmp_sample_kernel(tab_ref, q_ref, kvc_ref, new_ref, w_ref, gate_ref,
                       oc_ref, idx_ref, kall_ref, vall_ref, imp_ref, *, q_pos):
    nblk = kvc_ref.shape[1]
    nk = kall_ref.shape[0]
    blk = lax.broadcasted_iota(I32, (1, nk), 1)
    dist = q_pos - (blk * CMP_BLOCK + (CMP_BLOCK - 1))
    valid = dist >= 0
    bkt = _bucket(dist)
    new_c = new_ref[0] * w_ref[0]
    kall_ref[...] = jnp.zeros(kall_ref.shape, F32)
    vall_ref[...] = jnp.zeros(vall_ref.shape, F32)
    for g in range(N_KV):
        kall_ref[0:nblk, :] = kvc_ref[0, :, g, :]
        vall_ref[0:nblk, :] = kvc_ref[0, :, N_KV + g, :]
        kall_ref[nblk:nblk + 1, :] = new_c[g:g + 1, :]
        vall_ref[nblk:nblk + 1, :] = new_c[N_KV + g:N_KV + g + 1, :]
        kb = kall_ref[...].astype(BF16)
        vb = vall_ref[...].astype(BF16)
        biases = _bias_lookup(bkt, tab_ref, [g * HPG + h for h in range(HPG)])
        rows = lax.broadcasted_iota(I32, (HROWS, nk), 0)
        bias = jnp.zeros((HROWS, nk), F32)
        for h in range(HPG):
            bias = jnp.where(rows == h, biases[h], bias)
        s = _dot_nt(q_ref[0, g].astype(BF16), kb) * SCALE + bias
        p = _masked_softmax(s, valid)
        gate = _sigmoid(gate_ref[0, 0, g])
        oc_ref[0, g] = gate * _dot(p.astype(BF16), vb)
        head_row = jnp.where(rows < HPG, p, 0.0)
        imp_ref[g:g + 1, :] = jnp.sum(head_row, axis=0, keepdims=True)

    imp = imp_ref[0:N_KV, :]
    blk4 = lax.broadcasted_iota(I32, (N_KV, nk), 1)
    cur = q_pos // CMP_BLOCK
    score = _block_scores(imp, blk4, cur)
    rank = jnp.zeros((N_KV, nk), F32)
    for j in range(nblk + 1):
        col = score[:, j:j + 1]
        rank = rank + _ahead(col, score, blk4 > j)
    rank = jnp.where(blk4 <= nblk, rank, float(nk))
    blkf = blk4.astype(F32)
    slot = lax.broadcasted_iota(I32, (N_KV, N_SEL), 1)
    idx = jnp.zeros((N_KV, N_SEL), F32)
    for r in range(N_SEL):
        pick = jnp.sum(jnp.where(rank == float(r), blkf, 0.0), axis=-1, keepdims=True)
        idx = jnp.where(slot == r, pick, idx)
    idx_ref[0] = idx.astype(I32)


def _cmp_sample(rel_bias, q8, kvc, kvc_new, w8, gates8, q_pos):
    b, nblk = kvc.shape[:2]
    nk = 2 * HEAD_DIM
    assert nblk + 1 <= nk and q_pos // CMP_BLOCK == nblk
    return pl.pallas_call(
        functools.partial(_cmp_sample_kernel, q_pos=q_pos),
        out_shape=(jax.ShapeDtypeStruct((b, N_KV, HROWS, HEAD_DIM), F32),
                   jax.ShapeDtypeStruct((b, N_KV, N_SEL), I32)),
        grid=(b,),
        in_specs=[pl.BlockSpec(memory_space=pltpu.SMEM),
                  pl.BlockSpec((1, N_KV, HROWS, HEAD_DIM), lambda i: (i, 0, 0, 0)),
                  pl.BlockSpec((1, nblk, KV_ROWS, HEAD_DIM), lambda i: (i, 0, 0, 0)),
                  pl.BlockSpec((1, KV_ROWS, HEAD_DIM), lambda i: (i, 0, 0)),
                  pl.BlockSpec((CMP_BLOCK, KV_ROWS, HEAD_DIM), lambda i: (0, 0, 0)),
                  pl.BlockSpec((1, 1, N_KV, HROWS, 1), lambda i: (i, 0, 0, 0, 0))],
        out_specs=(pl.BlockSpec((1, N_KV, HROWS, HEAD_DIM), lambda i: (i, 0, 0, 0)),
                   pl.BlockSpec((1, N_KV, N_SEL), lambda i: (i, 0, 0))),
        scratch_shapes=[pltpu.VMEM((nk, HEAD_DIM), F32), pltpu.VMEM((nk, HEAD_DIM), F32),
                        pltpu.VMEM((HROWS, nk), F32)],
        compiler_params=_cparams(1),
        name="cmp_sample",
    )(rel_bias, q8, kvc, kvc_new, w8, gates8)


def _sel_bias_kernel(tab_ref, o_ref, *, q_pos):
    g = pl.program_id(0)
    shape = o_ref.shape[1:]
    blk = lax.broadcasted_iota(I32, shape, 0)
    rows = lax.broadcasted_iota(I32, shape, 1)
    offs = lax.broadcasted_iota(I32, shape, 2)
    dist = q_pos - (blk * CMP_BLOCK + offs)
    biases = _bias_lookup(_bucket(dist), tab_ref, [g * HPG + h for h in range(HPG)])
    bias = jnp.zeros(shape, F32)
    for h in range(HPG):
        bias = jnp.where(rows == h, biases[h], bias)
    o_ref[0] = jnp.where(dist >= 0, bias, NEG_INF)


def _sel_bias(rel_bias, q_pos, n_blocks):
    return pl.pallas_call(
        functools.partial(_sel_bias_kernel, q_pos=q_pos),
        out_shape=jax.ShapeDtypeStruct((N_KV, n_blocks, HROWS, CMP_BLOCK), F32),
        grid=(N_KV,),
        in_specs=[pl.BlockSpec(memory_space=pltpu.SMEM)],
        out_specs=pl.BlockSpec((1, n_blocks, HROWS, CMP_BLOCK), lambda g: (g, 0, 0, 0)),
        compiler_params=_cparams(1),
        name="sel_bias",
    )(rel_bias)


def _sel_sample_kernel(pt_ref, idx_ref, bias_ref, q_ref, *refs, n_past):
    nb = SEL_BLOCKS_PER_STEP
    blk_refs = refs[:N_KV * nb]
    new_ref, gate_ref, os_ref, kcat_ref, vcat_ref, bcat_ref, m_ref, l_ref, acc_ref = refs[N_KV * nb:]
    b = pl.program_id(0)
    step = pl.program_id(1)

    @pl.when(step == 0)
    def _():
        _attn_init(m_ref, l_ref, acc_ref)

    first = lax.broadcasted_iota(I32, (CMP_BLOCK, HEAD_DIM), 0) == 0
    for g in range(N_KV):
        kn = jnp.where(first, new_ref[0][g:g + 1, :], 0.0)
        vn = jnp.where(first, new_ref[0][N_KV + g:N_KV + g + 1, :], 0.0)
        for j in range(nb):
            n = idx_ref[b, g, step * nb + j]
            is_new = n >= n_past
            blk_ref = blk_refs[g * nb + j]
            keys = slice(j * CMP_BLOCK, (j + 1) * CMP_BLOCK)
            kcat_ref[keys, :] = jnp.where(is_new, kn, blk_ref[0, :, g, :]).astype(BF16)
            vcat_ref[keys, :] = jnp.where(is_new, vn, blk_ref[0, :, N_KV + g, :]).astype(BF16)
            bcat_ref[:, keys] = bias_ref[g, n]
        _attn_update(g, q_ref[0, g].astype(BF16), kcat_ref[...], vcat_ref[...], bcat_ref[...], None,
                     m_ref, l_ref, acc_ref)

    @pl.when(step == pl.num_programs(1) - 1)
    def _():
        for g in range(N_KV):
            o = acc_ref[g] / jnp.maximum(l_ref[g], 1e-30)
            os_ref[0, g] = _sigmoid(gate_ref[0, 0, g]) * o


def _sel_sample(page_table, idx, rel_bias, q8, cache_half, kvs_new, gates8, q_pos):
    b = q8.shape[0]
    nb = SEL_BLOCKS_PER_STEP
    n_past = q_pos // CMP_BLOCK
    per_page = PAGE_SIZE // CMP_BLOCK
    bias = _sel_bias(rel_bias, q_pos, n_past + 1)

    def blk_spec(g, j):
        def index(i, s, pt, ix):
            n = jnp.minimum(ix[i, g, s * nb + j], n_past - 1)
            return (pt[i, n // per_page] * per_page + n % per_page, 0, 0, 0)
        return pl.BlockSpec((1, CMP_BLOCK, KV_ROWS, HEAD_DIM), index)

    hspec = pl.BlockSpec((1, N_KV, HROWS, HEAD_DIM), lambda i, s, pt, ix: (i, 0, 0, 0))
    grid_spec = pltpu.PrefetchScalarGridSpec(
        num_scalar_prefetch=2,
        grid=(b, N_SEL // nb),
        in_specs=[pl.BlockSpec(bias.shape, lambda i, s, pt, ix: (0, 0, 0, 0)), hspec]
        + [blk_spec(g, j) for g in range(N_KV) for j in range(nb)]
        + [pl.BlockSpec((1, KV_ROWS, HEAD_DIM), lambda i, s, pt, ix: (i, 0, 0)),
           pl.BlockSpec((1, 1, N_KV, HROWS, 1), lambda i, s, pt, ix: (i, 1, 0, 0, 0))],
        out_specs=hspec,
        scratch_shapes=[pltpu.VMEM((nb * CMP_BLOCK, HEAD_DIM), BF16),
                        pltpu.VMEM((nb * CMP_BLOCK, HEAD_DIM), BF16),
                        pltpu.VMEM((HROWS, nb * CMP_BLOCK), F32),
                        pltpu.VMEM((N_KV, HROWS, 1), F32), pltpu.VMEM((N_KV, HROWS, 1), F32),
                        pltpu.VMEM((N_KV, HROWS, HEAD_DIM), F32)],
    )
    return pl.pallas_call(
        functools.partial(_sel_sample_kernel, n_past=n_past),
        out_shape=jax.ShapeDtypeStruct((b, N_KV, HROWS, HEAD_DIM), F32),
        grid_spec=grid_spec,
        compiler_params=_cparams(2),
        name="sel_sample",
    )(page_table, idx, bias, q8, *([cache_half] * (N_KV * nb)), kvs_new, gates8)


def _win_sample_kernel(tab_ref, q_ref, win_ref, new_ref, gate_ref, oc_ref, os_ref,
                       yb_ref, nwin_ref, kall_ref, vall_ref):
    wc = win_ref.shape[1]
    nk = kall_ref.shape[0]
    nwin_ref[0, 0:wc - 1] = win_ref[0, 1:wc]
    nwin_ref[0, wc - 1] = new_ref[0]
    kidx = lax.broadcasted_iota(I32, (1, nk), 1)
    dist = wc - kidx
    valid = jnp.where(dist >= 0, dist, WINDOW) < WINDOW
    bkt = _bucket(dist)
    kall_ref[...] = jnp.zeros(kall_ref.shape, F32)
    vall_ref[...] = jnp.zeros(vall_ref.shape, F32)
    for g in range(N_KV):
        kall_ref[0:wc, :] = win_ref[0, :, g, :]
        vall_ref[0:wc, :] = win_ref[0, :, N_KV + g, :]
        kall_ref[wc:wc + 1, :] = new_ref[0][g:g + 1, :]
        vall_ref[wc:wc + 1, :] = new_ref[0][N_KV + g:N_KV + g + 1, :]
        biases = _bias_lookup(bkt, tab_ref, [g * HPG + h for h in range(HPG)])
        rows = lax.broadcasted_iota(I32, (HROWS, nk), 0)
        bias = jnp.zeros((HROWS, nk), F32)
        for h in range(HPG):
            bias = jnp.where(rows == h, biases[h], bias)
        s = _dot_nt(q_ref[0, g].astype(BF16), kall_ref[...].astype(BF16)) * SCALE + bias
        p = _masked_softmax(s, valid)
        o = _dot(p.astype(BF16), vall_ref[...].astype(BF16))
        yb_ref[0, g] = oc_ref[0, g] + os_ref[0, g] + _sigmoid(gate_ref[0, 0, g]) * o


def _win_sample(rel_bias, q8, win, kvw_new, gates8, oc, osel):
    b, wc = win.shape[:2]
    nk = wc + HEAD_DIM
    hspec = pl.BlockSpec((1, N_KV, HROWS, HEAD_DIM), lambda i: (i, 0, 0, 0))
    wspec = pl.BlockSpec((1, wc, KV_ROWS, HEAD_DIM), lambda i: (i, 0, 0, 0))
    return pl.pallas_call(
        _win_sample_kernel,
        out_shape=(jax.ShapeDtypeStruct((b, N_KV, HROWS, HEAD_DIM), F32),
                   jax.ShapeDtypeStruct(win.shape, F32)),
        grid=(b,),
        in_specs=[pl.BlockSpec(memory_space=pltpu.SMEM),
                  hspec,
                  wspec,
                  pl.BlockSpec((1, KV_ROWS, HEAD_DIM), lambda i: (i, 0, 0)),
                  pl.BlockSpec((1, 1, N_KV, HROWS, 1), lambda i: (i, 2, 0, 0, 0)),
                  hspec, hspec],
        out_specs=(hspec, wspec),
        scratch_shapes=[pltpu.VMEM((nk, HEAD_DIM), F32), pltpu.VMEM((nk, HEAD_DIM), F32)],
        compiler_params=_cparams(1),
        name="win_sample",
    )(rel_bias, q8, win, kvw_new, gates8, oc, osel)


def _in_proj(x, norm_w, w_in_t, *, tm, tm_norm, pages=None):
    xn = _rmsnorm(x, norm_w, BF16, tm_norm)
    return _in_proj_all(xn, w_in_t, tm=tm, tn=IN_TN, pages=pages)


def _out_and_ffn(x, ya, yb, z, w_proj_a, w_proj_b, w_out, norm_ffn, w_gate, w_up, w_down,
                 norm_final, *, tm, tm_norm):
    d_ff = w_gate.shape[1]
    mix = _mix(ya, yb, w_proj_a, w_proj_b, z, tm=tm, tn=512)
    h = _mm(mix, w_out, n_cols=D_MODEL, tm=tm, tn=512, res=x, name="out_proj")
    hn = _rmsnorm(h, norm_ffn, BF16, tm_norm)
    ff = _swiglu(hn, w_gate, w_up, tm=tm, tn=256)
    half = d_ff // 2
    y = _mm(ff, w_down, n_cols=D_MODEL, tm=tm, tn=256, res=h, k_block=0, tk=half, name="ffn_down0")
    y = _mm(ff, w_down, n_cols=D_MODEL, tm=tm, tn=256, res=y, k_block=1, tk=half, name="ffn_down1")
    return _rmsnorm(y, norm_final, F32, tm_norm)


def kernel(x_prompt, x_sample, cache_cmp_kv, cache_sel_kv, cache_win_kv, state_rglru_h, state_conv,
           page_table, rel_bias, norm_mix, w_in, conv_w, conv_b, lru_wa, lru_ba, lru_wi, lru_bi,
           lru_lambda, nsa_w_cmp, w_proj_a, w_proj_b, w_out, norm_ffn, w_gate, w_up, w_down,
           norm_final):
    depth = w_in.shape[0]
    assert depth == 1, "single-layer trunk"
    bp, tp, _ = x_prompt.shape
    bs, ts, _ = x_sample.shape
    assert ts == 1
    n_pool = cache_cmp_kv.shape[1]
    past_len = page_table.shape[1] * PAGE_SIZE
    wc = cache_win_kv.shape[2]

    w_in0 = jnp.swapaxes(w_in[0], 0, 1)
    wexp = jnp.broadcast_to(nsa_w_cmp[0][..., None], (2, N_KV, CMP_BLOCK, HEAD_DIM))
    w8 = jnp.broadcast_to(nsa_w_cmp[0].reshape(KV_ROWS, CMP_BLOCK).T[..., None],
                          (CMP_BLOCK, KV_ROWS, HEAD_DIM))
    layer_w = (conv_w[0], conv_b[0], lru_wa[0], lru_ba[0], lru_wi[0], lru_bi[0], lru_lambda[0])
    tail_w = (w_proj_a[0], w_proj_b[0], w_out[0], norm_ffn[0], w_gate[0], w_up[0], w_down[0],
              norm_final)

    mp = bp * tp
    xp = x_prompt.reshape(mp, D_MODEL)
    per_page = PAGE_SIZE // CMP_BLOCK
    cache_c = cache_cmp_kv.reshape(n_pool, PAGE_SIZE, KV_ROWS, HEAD_DIM)
    z, kvc = _in_proj(xp, norm_mix[0], w_in0, tm=1024, tm_norm=256, pages=(cache_c, page_table, w8))
    ya, conv_p, h_p = _rglru_prompt(z, bp, tp, *layer_w, tt=128)
    gn_t = z[:, Z_GN:Z_GN + 3 * N_HEADS].reshape(mp, 3, N_KV, HPG).transpose(1, 2, 3, 0)
    bd, bo = _bias_tiles(rel_bias, ATT_T)
    oc, sel = _cmp_prompt(z, rel_bias, wexp, gn_t, bp, tp)
    osel = _sel_prompt(z, rel_bias, bd, bo, gn_t, sel, bp, tp)
    yb = _win_prompt(z, rel_bias, bd, bo, gn_t, oc, osel, bp, tp)
    y_prompt = _out_and_ffn(xp, ya, yb, z, *tail_w, tm=1024, tm_norm=256)

    kv_shape = (1, bp, tp // PAGE_SIZE, PAGE_SIZE, 2, N_KV, HEAD_DIM)
    cmp_p = z[:, OFF_KVC:OFF_KVS].reshape(kv_shape)
    sel_p = z[:, OFF_KVS:OFF_KVW].reshape(kv_shape)
    wlen = min(WINDOW, tp)
    win_p = z[:, OFF_KVW:MAIN_W].reshape(bp, tp, 2, N_KV, HEAD_DIM)[None, :, tp - wlen:]

    xs = x_sample.reshape(bs, D_MODEL)
    zs = _in_proj(xs, norm_mix[0], w_in0, tm=bs, tm_norm=bs)
    buf_t = state_conv[0].transpose(1, 0, 2)
    ya_s, h_s = _rglru_step(zs, buf_t, state_rglru_h[0], *layer_w)

    q8 = jnp.pad(zs[:, OFF_Q:OFF_KVC].reshape(bs, N_KV, HPG, HEAD_DIM),
                 ((0, 0), (0, 0), (0, HROWS - HPG), (0, 0)))
    gates8 = jnp.pad(zs[:, Z_GN:Z_GN + 3 * N_HEADS].reshape(bs, 3, N_KV, HPG),
                     ((0, 0), (0, 0), (0, 0), (0, HROWS - HPG)))[..., None]
    kvc_new = zs[:, OFF_KVC:OFF_KVS].reshape(bs, KV_ROWS, HEAD_DIM)
    kvs_new = zs[:, OFF_KVS:OFF_KVW].reshape(bs, KV_ROWS, HEAD_DIM)
    kvw_new = zs[:, OFF_KVW:MAIN_W].reshape(bs, KV_ROWS, HEAD_DIM)
    cache_s = cache_sel_kv.reshape(n_pool * per_page, CMP_BLOCK, KV_ROWS, HEAD_DIM)
    win = cache_win_kv.reshape(bs, wc, KV_ROWS, HEAD_DIM)

    oc_s, idx = _cmp_sample(rel_bias, q8, kvc, kvc_new, w8, gates8, past_len)
    os_s = _sel_sample(page_table, idx, rel_bias, q8, cache_s, kvs_new, gates8, past_len)
    yb8, win_next = _win_sample(rel_bias, q8, win, kvw_new, gates8, oc_s, os_s)
    yb_s = yb8[:, :, :HPG].reshape(bs, Q_W).astype(BF16)
    y_sample = _out_and_ffn(xs, ya_s, yb_s, zs, *tail_w, tm=bs, tm_norm=bs)

    row_shape = (1, bs, 1, 2, N_KV, HEAD_DIM)
    cmp_s = kvc_new.reshape(row_shape)
    sel_s = kvs_new.reshape(row_shape)
    win_s = win_next.reshape(1, bs, wc, 2, N_KV, HEAD_DIM)
    conv_s = jnp.concatenate([state_conv[0], zs[:, None, OFF_XR:OFF_GR]], axis=1)[:, 1:][None]

    return (y_prompt.reshape(bp, tp, D_MODEL), y_sample.reshape(bs, ts, D_MODEL),
            cmp_p, sel_p, win_p, h_p.reshape(1, bp, D_RNN), conv_p[None],
            cmp_s, sel_s, win_s, h_s[None], conv_s)
```

```python
import functools
import math

import jax
import jax.numpy as jnp
from jax import lax
from jax.experimental import pallas as pl
from jax.experimental.pallas import tpu as pltpu

F32 = jnp.float32
BF16 = jnp.bfloat16
I32 = jnp.int32

D_MODEL = 4096
D_RNN = 2048
N_RNN_BLOCKS = 16
RNN_BLOCK = D_RNN // N_RNN_BLOCKS
CONV_W = 4
LRU_C = 8.0
N_HEADS = 16
HEAD_DIM = 128
N_KV = 4
HPG = N_HEADS // N_KV
CMP_BLOCK = 64
N_SEL = 16
WINDOW = 512
N_BUCKETS = 32
MAX_DISTANCE = 128
PAGE_SIZE = 128
Q_W = N_HEADS * HEAD_DIM
KV_W = 2 * N_KV * HEAD_DIM
EPS = 1e-6
NEG_INF = -1e30
FORCE = 1e9
SCALE = HEAD_DIM ** -0.5
LOG2E = math.log2(math.e)
DENOM_ROWS = 16

OFF_XR = 0
OFF_GR = D_RNN
OFF_Q = 2 * D_RNN
OFF_KVC = OFF_Q + Q_W
OFF_KVS = OFF_KVC + KV_W
OFF_KVW = OFF_KVS + KV_W
OFF_GN = OFF_KVW + KV_W
OFF_GA = OFF_GN + 3 * N_HEADS
MAIN_W = OFF_GN
IN_TN = 512
Z_GN = MAIN_W
Z_GA = MAIN_W + IN_TN
Z_GB = Z_GA + D_MODEL

ATT_T = 256
HROWS = 8
KV_ROWS = 2 * N_KV
PAGES_PER_STEP = 8
SEL_BLOCKS_PER_STEP = 4
VMEM_LIMIT = 56 * 1024 * 1024


def _cparams(n_axes):
    return pltpu.CompilerParams(dimension_semantics=("arbitrary",) * n_axes,
                                vmem_limit_bytes=VMEM_LIMIT)


def _sigmoid(x):
    return 1.0 / (1.0 + jnp.exp(-x))


def _gelu_tanh(x):
    return 0.5 * x * (1.0 + jnp.tanh(math.sqrt(2.0 / math.pi) * (x + 0.044715 * (x * x * x))))


def _dot(a, b):
    return jnp.dot(a, b, preferred_element_type=F32)


def _dot_nt(a, b):
    return lax.dot_general(a, b, (((1,), (1,)), ((), ())), preferred_element_type=F32)


def _bucket(dist):
    n = jnp.maximum(dist, 0)
    exact = N_BUCKETS // 2
    nf = jnp.maximum(n, 1).astype(F32)
    large = exact + (jnp.log(nf / exact) / math.log(MAX_DISTANCE / exact)
                     * (N_BUCKETS - exact)).astype(I32)
    return jnp.where(n < exact, n, jnp.minimum(large, N_BUCKETS - 1))


def _bias_lookup(bkt, tab_ref, cols):
    outs = [jnp.zeros(bkt.shape, F32) for _ in cols]
    for b in range(N_BUCKETS):
        eq = bkt == b
        outs = [jnp.where(eq, tab_ref[b, c], o) for c, o in zip(cols, outs)]
    return outs


def _block_scores(imp, blk, cur):
    score = jnp.where(blk == 0, FORCE, jnp.where(blk == cur, FORCE,
                                                 jnp.where(blk == cur - 1, FORCE, imp)))
    return jnp.where(blk <= cur, score, -FORCE)


def _ahead(col, score, later):
    return jnp.where(later, jnp.where(col >= score, 1.0, 0.0), jnp.where(col > score, 1.0, 0.0))


def _masked_softmax(s, valid):
    s = jnp.where(valid, s, NEG_INF)
    m = jnp.max(s, axis=-1, keepdims=True)
    e = jnp.where(valid, jnp.exp(s - m), 0.0)
    return e / jnp.maximum(jnp.sum(e, axis=-1, keepdims=True), 1e-30)


def _rmsnorm_kernel(x_ref, g_ref, o_ref):
    x = x_ref[...]
    y = x * lax.rsqrt(jnp.mean(x * x, axis=-1, keepdims=True) + EPS)
    o_ref[...] = (y * g_ref[...]).astype(o_ref.dtype)


def _rmsnorm(x, g, out_dtype, tm):
    m, d = x.shape
    return pl.pallas_call(
        _rmsnorm_kernel,
        out_shape=jax.ShapeDtypeStruct((m, d), out_dtype),
        grid=(m // tm,),
        in_specs=[pl.BlockSpec((tm, d), lambda i: (i, 0)),
                  pl.BlockSpec((1, d), lambda i: (0, 0))],
        out_specs=pl.BlockSpec((tm, d), lambda i: (i, 0)),
        compiler_params=_cparams(1),
        name="rmsnorm",
    )(x, g.reshape(1, d))


def _mm_kernel(a_ref, w_ref, o_ref):
    o_ref[...] = _dot(a_ref[...], w_ref[...].astype(BF16)).astype(o_ref.dtype)


def _mm_res_kernel(a_ref, w_ref, r_ref, o_ref):
    o_ref[...] = r_ref[...] + _dot(a_ref[...], w_ref[...].astype(BF16))


def _mm(a, w, *, n_cols, tm, tn, out_dtype=F32, res=None, k_block=0, tk=None, name="mm"):
    m = a.shape[0]
    tk = a.shape[1] if tk is None else tk
    grid = (m // tm, n_cols // tn)
    in_specs = [pl.BlockSpec((tm, tk), lambda i, j: (i, k_block)),
                pl.BlockSpec((tk, tn), lambda i, j: (k_block, j))]
    args = [a, w]
    kern = _mm_kernel
    if res is not None:
        in_specs.append(pl.BlockSpec((tm, tn), lambda i, j: (i, j)))
        args.append(res)
        kern = _mm_res_kernel
    return pl.pallas_call(
        kern,
        out_shape=jax.ShapeDtypeStruct((m, n_cols), out_dtype),
        grid=grid,
        in_specs=in_specs,
        out_specs=pl.BlockSpec((tm, tn), lambda i, j: (i, j)),
        compiler_params=_cparams(2),
        name=name,
    )(*args)


def _in_proj_kernel(a_ref, wt_ref, o_ref):
    o_ref[...] = _dot_nt(a_ref[...], wt_ref[...].astype(BF16))


def _compress_pages(page_refs, w_ref, o_ref):
    per_page = PAGE_SIZE // CMP_BLOCK
    w = w_ref[...]
    for k, page_ref in enumerate(page_refs):
        x = page_ref[0].reshape(per_page, CMP_BLOCK, KV_ROWS, HEAD_DIM)
        o_ref[0, per_page * k:per_page * (k + 1)] = jnp.sum(x * w[None], axis=1)


def _in_proj_pages_kernel(pt_ref, a_ref, wt_ref, *refs):
    page_refs, w8_ref, o_ref, kvc_ref = refs[:-3], refs[-3], refs[-2], refs[-1]
    o_ref[...] = _dot_nt(a_ref[...], wt_ref[...].astype(BF16))
    _compress_pages(page_refs, w8_ref, kvc_ref)


def _in_proj_all(a, wt, *, tm, tn, pages=None):
    m, k = a.shape
    assert MAIN_W % tn == 0 and Z_GN == MAIN_W and Z_GA == MAIN_W + tn and 3 * N_HEADS <= tn
    n_gate_blocks = 2 * D_MODEL // tn
    nb = MAIN_W // tn + 1 + n_gate_blocks
    assert OFF_GA + n_gate_blocks * tn == wt.shape[0]

    def w_index(i, j, *_):
        row = jnp.where(j <= MAIN_W // tn, j * tn, OFF_GA + (j - MAIN_W // tn - 1) * tn)
        return (pl.multiple_of(row, 8), 0)

    grid = (m // tm, nb)
    in_specs = [pl.BlockSpec((tm, k), lambda i, j, *_: (i, 0)),
                pl.BlockSpec((pl.Element(tn), pl.Element(k)), w_index)]
    out_spec = pl.BlockSpec((tm, tn), lambda i, j, *_: (i, j))
    out_shape = jax.ShapeDtypeStruct((m, nb * tn), F32)
    if pages is None:
        return pl.pallas_call(
            _in_proj_kernel, out_shape=out_shape, grid=grid, in_specs=in_specs, out_specs=out_spec,
            compiler_params=_cparams(2), name="in_proj",
        )(a, wt)

    cache, page_table, w8 = pages
    b, n_pages = page_table.shape
    per_page = PAGE_SIZE // CMP_BLOCK
    pps = PAGES_PER_STEP
    chunks = n_pages // pps
    assert grid[0] * grid[1] >= b * chunks

    def chunk(i, j):
        t = jnp.minimum(i * nb + j, b * chunks - 1)
        return t // chunks, t % chunks

    def page_index(k):
        def index(i, j, pt):
            seq, c = chunk(i, j)
            return (pt[seq, c * pps + k], 0, 0, 0)
        return index

    def kvc_index(i, j, pt):
        seq, c = chunk(i, j)
        return (seq, c, 0, 0)

    grid_spec = pltpu.PrefetchScalarGridSpec(
        num_scalar_prefetch=1,
        grid=grid,
        in_specs=in_specs
        + [pl.BlockSpec((1, PAGE_SIZE, KV_ROWS, HEAD_DIM), page_index(k)) for k in range(pps)]
        + [pl.BlockSpec((CMP_BLOCK, KV_ROWS, HEAD_DIM), lambda i, j, pt: (0, 0, 0))],
        out_specs=(out_spec,
                   pl.BlockSpec((1, pps * per_page, KV_ROWS, HEAD_DIM), kvc_index)),
    )
    return pl.pallas_call(
        _in_proj_pages_kernel,
        out_shape=(out_shape,
                   jax.ShapeDtypeStruct((b, n_pages * per_page, KV_ROWS, HEAD_DIM), F32)),
        grid_spec=grid_spec,
        compiler_params=_cparams(2),
        name="in_proj_pages",
    )(page_table, a, wt, *([cache] * pps), w8)


def _swiglu_kernel(a_ref, wg_ref, wu_ref, o_ref):
    a = a_ref[...]
    g = _dot(a, wg_ref[...].astype(BF16))
    u = _dot(a, wu_ref[...].astype(BF16))
    o_ref[...] = ((g * _sigmoid(g)) * u).astype(o_ref.dtype)


def _swiglu(a, wg, wu, *, tm, tn):
    m, k = a.shape
    n = wg.shape[1]
    return pl.pallas_call(
        _swiglu_kernel,
        out_shape=jax.ShapeDtypeStruct((m, n), BF16),
        grid=(m // tm, n // tn),
        in_specs=[pl.BlockSpec((tm, k), lambda i, j: (i, 0)),
                  pl.BlockSpec((k, tn), lambda i, j: (0, j)),
                  pl.BlockSpec((k, tn), lambda i, j: (0, j))],
        out_specs=pl.BlockSpec((tm, tn), lambda i, j: (i, j)),
        compiler_params=_cparams(2),
        name="ffn_gate_up",
    )(a, wg, wu)


def _mix_kernel(ya_ref, yb_ref, wa_ref, wb_ref, ga_ref, gb_ref, o_ref):
    pa = _dot(ya_ref[...], wa_ref[...].astype(BF16))
    pb = _dot(yb_ref[...], wb_ref[...].astype(BF16))
    o_ref[...] = (_sigmoid(ga_ref[...]) * pa + _sigmoid(gb_ref[...]) * pb).astype(o_ref.dtype)


def _mix(ya, yb, wa, wb, z, *, tm, tn):
    m, k = ya.shape
    n = wa.shape[1]
    assert Z_GA % tn == 0 and Z_GB % tn == 0
    gate = lambda off: pl.BlockSpec((tm, tn), lambda i, j: (i, j + off // tn))
    return pl.pallas_call(
        _mix_kernel,
        out_shape=jax.ShapeDtypeStruct((m, n), BF16),
        grid=(m // tm, n // tn),
        in_specs=[pl.BlockSpec((tm, k), lambda i, j: (i, 0)),
                  pl.BlockSpec((tm, k), lambda i, j: (i, 0)),
                  pl.BlockSpec((k, tn), lambda i, j: (0, j)),
                  pl.BlockSpec((k, tn), lambda i, j: (0, j)),
                  gate(Z_GA), gate(Z_GB)],
        out_specs=pl.BlockSpec((tm, tn), lambda i, j: (i, j)),
        compiler_params=_cparams(2),
        name="branch_merge",
    )(ya, yb, wa, wb, z, z)


def _lru_gates(xc, wa_ref, ba_ref, wi_ref, bi_ref, lam_ref, a_ref, u_ref):
    for n in range(N_RNN_BLOCKS):
        sl = slice(n * RNN_BLOCK, (n + 1) * RNN_BLOCK)
        xn = xc[:, sl]
        xb = xn.astype(BF16)
        r = _sigmoid(_dot(xb, wa_ref[n].astype(BF16)) + ba_ref[:, sl])
        i = _sigmoid(_dot(xb, wi_ref[n].astype(BF16)) + bi_ref[:, sl])
        neg_lam = -lam_ref[:, sl]
        softplus = jnp.maximum(neg_lam, 0.0) + jnp.log1p(jnp.exp(-jnp.abs(neg_lam)))
        log_a = (-LRU_C * r) * softplus
        a = jnp.exp(log_a)
        a_ref[:, sl] = a
        u_ref[:, sl] = jnp.sqrt(-jnp.tanh(log_a) * (a * a + 1.0)) * (i * xn)


def _rglru_prompt_kernel(xr_ref, gr_ref, cw_ref, cb_ref, wa_ref, ba_ref, wi_ref, bi_ref, lam_ref,
                         ya_ref, conv_ref, hl_ref, xp_ref, a_ref, u_ref, h_ref, carry_ref):
    ti = pl.program_id(1)
    tt, c = xr_ref.shape

    @pl.when(ti == 0)
    def _():
        xp_ref[0:8, :] = jnp.zeros((8, c), F32)
        carry_ref[...] = jnp.zeros((1, c), F32)

    x = xr_ref[...]
    xp_ref[8:8 + tt, :] = x
    xc = cb_ref[...]
    for k in range(CONV_W):
        xc = xc + cw_ref[k:k + 1, :] * xp_ref[5 + k:5 + k + tt, :]
    xp_ref[0:8, :] = xp_ref[tt:tt + 8, :]

    _lru_gates(xc, wa_ref, ba_ref, wi_ref, bi_ref, lam_ref, a_ref, u_ref)

    row = lax.broadcasted_iota(I32, (8, c), 0)

    def chunk(ci, carry):
        off = pl.multiple_of(ci * 8, 8)
        a = a_ref[pl.ds(off, 8), :]
        b = u_ref[pl.ds(off, 8), :]
        for s in (1, 2, 4):
            keep = row >= s
            b = jnp.where(keep, a * pltpu.roll(b, s, 0) + b, b)
            a = jnp.where(keep, a * pltpu.roll(a, s, 0), a)
        h = a * carry + b
        h_ref[pl.ds(off, 8), :] = h
        return h[7:8, :]

    carry = lax.fori_loop(0, tt // 8, chunk, carry_ref[...])
    carry_ref[...] = carry
    ya_ref[...] = (h_ref[...] * _gelu_tanh(gr_ref[...])).astype(ya_ref.dtype)

    @pl.when(ti == pl.num_programs(1) - 1)
    def _():
        conv_ref[0] = x[tt - (CONV_W - 1):, :]
        hl_ref[0] = carry


def _rglru_prompt(z, batch, seq, conv_w, conv_b, lru_wa, lru_ba, lru_wi, lru_bi, lam, *, tt):
    nt = seq // tt
    c = D_RNN
    row = lambda b, t: (b * nt + t, 0)
    full2 = lambda b, t: (0, 0)
    full3 = lambda b, t: (0, 0, 0)
    return pl.pallas_call(
        _rglru_prompt_kernel,
        out_shape=(jax.ShapeDtypeStruct((batch * seq, c), BF16),
                   jax.ShapeDtypeStruct((batch, CONV_W - 1, c), F32),
                   jax.ShapeDtypeStruct((batch, 1, c), F32)),
        grid=(batch, nt),
        in_specs=[pl.BlockSpec((tt, c), row),
                  pl.BlockSpec((tt, c), lambda b, t: (b * nt + t, 1)),
                  pl.BlockSpec((CONV_W, c), full2),
                  pl.BlockSpec((1, c), full2),
                  pl.BlockSpec((N_RNN_BLOCKS, RNN_BLOCK, RNN_BLOCK), full3),
                  pl.BlockSpec((1, c), full2),
                  pl.BlockSpec((N_RNN_BLOCKS, RNN_BLOCK, RNN_BLOCK), full3),
                  pl.BlockSpec((1, c), full2),
                  pl.BlockSpec((1, c), full2)],
        out_specs=(pl.BlockSpec((tt, c), row),
                   pl.BlockSpec((1, CONV_W - 1, c), lambda b, t: (b, 0, 0)),
                   pl.BlockSpec((1, 1, c), lambda b, t: (b, 0, 0))),
        scratch_shapes=[pltpu.VMEM((tt + 8, c), F32), pltpu.VMEM((tt, c), F32),
                        pltpu.VMEM((tt, c), F32), pltpu.VMEM((tt, c), F32),
                        pltpu.VMEM((1, c), F32)],
        compiler_params=_cparams(2),
        name="rglru_prompt",
    )(z, z, conv_w, conv_b.reshape(1, c), lru_wa, lru_ba.reshape(1, c), lru_wi,
      lru_bi.reshape(1, c), lam.reshape(1, c))


def _rglru_step_kernel(xr_ref, gr_ref, buf_ref, h0_ref, cw_ref, cb_ref, wa_ref, ba_ref, wi_ref,
                       bi_ref, lam_ref, ya_ref, hn_ref, a_ref, u_ref):
    x = xr_ref[...]
    xc = cb_ref[...]
    for k in range(CONV_W - 1):
        xc = xc + cw_ref[k:k + 1, :] * buf_ref[k]
    xc = xc + cw_ref[CONV_W - 1:CONV_W, :] * x
    _lru_gates(xc, wa_ref, ba_ref, wi_ref, bi_ref, lam_ref, a_ref, u_ref)
    h = a_ref[...] * h0_ref[...] + u_ref[...]
    hn_ref[...] = h
    ya_ref[...] = (h * _gelu_tanh(gr_ref[...])).astype(ya_ref.dtype)


def _rglru_step(z, buf_t, h0, conv_w, conv_b, lru_wa, lru_ba, lru_wi, lru_bi, lam):
    b, c = h0.shape
    full2 = lambda i: (0, 0)
    full3 = lambda i: (0, 0, 0)
    return pl.pallas_call(
        _rglru_step_kernel,
        out_shape=(jax.ShapeDtypeStruct((b, c), BF16), jax.ShapeDtypeStruct((b, c), F32)),
        grid=(1,),
        in_specs=[pl.BlockSpec((b, c), full2),
                  pl.BlockSpec((b, c), lambda i: (0, 1)),
                  pl.BlockSpec((CONV_W - 1, b, c), full3),
                  pl.BlockSpec((b, c), full2),
                  pl.BlockSpec((CONV_W, c), full2),
                  pl.BlockSpec((1, c), full2),
                  pl.BlockSpec((N_RNN_BLOCKS, RNN_BLOCK, RNN_BLOCK), full3),
                  pl.BlockSpec((1, c), full2),
                  pl.BlockSpec((N_RNN_BLOCKS, RNN_BLOCK, RNN_BLOCK), full3),
                  pl.BlockSpec((1, c), full2),
                  pl.BlockSpec((1, c), full2)],
        out_specs=(pl.BlockSpec((b, c), full2), pl.BlockSpec((b, c), full2)),
        scratch_shapes=[pltpu.VMEM((b, c), F32), pltpu.VMEM((b, c), F32)],
        compiler_params=_cparams(1),
        name="rglru_step",
    )(z, z, buf_t, h0, conv_w, conv_b.reshape(1, c), lru_wa, lru_ba.reshape(1, c), lru_wi,
      lru_bi.reshape(1, c), lam.reshape(1, c))


def _bias_tiles_kernel(tab_ref, bd_ref, bo_ref):
    h = pl.program_id(0)
    t = bd_ref.shape[1]
    d0 = lax.broadcasted_iota(I32, (t, t), 1) - lax.broadcasted_iota(I32, (t, t), 0)
    bd_ref[0] = jnp.where(d0 >= 0, _bias_lookup(_bucket(d0), tab_ref, [h])[0] * LOG2E, NEG_INF)
    bo_ref[0] = _bias_lookup(_bucket(d0 + t), tab_ref, [h])[0] * LOG2E


def _bias_tiles(rel_bias, t):
    shp = jax.ShapeDtypeStruct((N_HEADS, t, t), F32)
    spec = pl.BlockSpec((1, t, t), lambda h: (h, 0, 0))
    return pl.pallas_call(
        _bias_tiles_kernel,
        out_shape=(shp, shp),
        grid=(N_HEADS,),
        in_specs=[pl.BlockSpec(memory_space=pltpu.SMEM)],
        out_specs=(spec, spec),
        compiler_params=_cparams(1),
        name="bias_tiles",
    )(rel_bias)


def _cmp_prompt_kernel(tab_ref, q_ref, kc_ref, vc_ref, wk_ref, wv_ref, gate_ref,
                       oc_ref, sel_ref, kcmp_ref, vcmp_ref, vcmp_t_ref):
    g = pl.program_id(1)
    qi = pl.program_id(2)
    tq = q_ref.shape[0]
    seq = kc_ref.shape[0]
    nblk = seq // CMP_BLOCK

    @pl.when(qi == 0)
    def _():
        k3 = kc_ref[...].reshape(nblk, CMP_BLOCK, HEAD_DIM)
        v3 = vc_ref[...].reshape(nblk, CMP_BLOCK, HEAD_DIM)
        kcmp_ref[...] = jnp.sum(k3 * wk_ref[0, 0][None], axis=1).astype(BF16)
        vcmp_ref[...] = jnp.zeros(vcmp_ref.shape, F32)
        vcmp_ref[0:nblk, :] = jnp.sum(v3 * wv_ref[0, 0][None], axis=1)
        vcmp_t_ref[...] = jnp.transpose(vcmp_ref[...]).astype(BF16)

    tpos = qi * tq + lax.broadcasted_iota(I32, (nblk, tq), 1)
    blk = lax.broadcasted_iota(I32, (nblk, tq), 0)
    dist = tpos - (blk * CMP_BLOCK + (CMP_BLOCK - 1))
    valid = dist >= 0
    biases = _bias_lookup(_bucket(dist), tab_ref, [g * HPG + h for h in range(HPG)])
    kb = kcmp_ref[...]
    vt = vcmp_t_ref[:, 0:nblk]
    gates = _sigmoid(gate_ref[0, 0])
    heads = range(HPG)
    cols = [slice(h * HEAD_DIM, (h + 1) * HEAD_DIM) for h in heads]
    scores = [_dot_nt(kb, q_ref[:, cols[h]].astype(BF16)) for h in heads]
    scores = [jnp.where(valid, s * SCALE + b, NEG_INF) for s, b in zip(scores, biases)]
    e = [jnp.where(valid, jnp.exp(s - jnp.max(s, axis=0, keepdims=True)), 0.0) for s in scores]
    probs = [x / jnp.maximum(jnp.sum(x, axis=0, keepdims=True), 1e-30) for x in e]
    imp = probs[0]
    for h in range(1, HPG):
        imp = imp + probs[h]
    outs = [_dot(vt, p.astype(BF16)) for p in probs]
    for h in heads:
        oc_ref[:, cols[h]] = jnp.transpose(outs[h] * gates[h:h + 1, :])

    cur = jnp.right_shift(tpos, 6)
    score = _block_scores(imp, blk, cur)
    rank = jnp.zeros((nblk, tq), F32)
    for j in range(nblk):
        rank = rank + _ahead(score[j:j + 1, :], score, blk > j)
    sel_ref[0, 0] = jnp.where(rank < float(min(N_SEL, nblk)), 0.0, NEG_INF)


def _cmp_prompt(z, rel_bias, wexp, gn_t, batch, seq):
    tq = ATT_T
    nq = seq // tq
    nblk = seq // CMP_BLOCK
    m = batch * seq
    qcol = OFF_Q // (HPG * HEAD_DIM)
    kcol = OFF_KVC // HEAD_DIM
    return pl.pallas_call(
        _cmp_prompt_kernel,
        out_shape=(jax.ShapeDtypeStruct((m, Q_W), F32),
                   jax.ShapeDtypeStruct((batch, N_KV, nblk, seq), F32)),
        grid=(batch, N_KV, nq),
        in_specs=[pl.BlockSpec(memory_space=pltpu.SMEM),
                  pl.BlockSpec((tq, HPG * HEAD_DIM), lambda b, g, i: (b * nq + i, qcol + g)),
                  pl.BlockSpec((seq, HEAD_DIM), lambda b, g, i: (b, kcol + g)),
                  pl.BlockSpec((seq, HEAD_DIM), lambda b, g, i: (b, kcol + N_KV + g)),
                  pl.BlockSpec((1, 1, CMP_BLOCK, HEAD_DIM), lambda b, g, i: (0, g, 0, 0)),
                  pl.BlockSpec((1, 1, CMP_BLOCK, HEAD_DIM), lambda b, g, i: (1, g, 0, 0)),
                  pl.BlockSpec((1, 1, HPG, tq), lambda b, g, i: (0, g, 0, b * nq + i))],
        out_specs=(pl.BlockSpec((tq, HPG * HEAD_DIM), lambda b, g, i: (b * nq + i, g)),
                   pl.BlockSpec((1, 1, nblk, tq), lambda b, g, i: (b, g, 0, i))),
        scratch_shapes=[pltpu.VMEM((nblk, HEAD_DIM), BF16), pltpu.VMEM((HEAD_DIM, HEAD_DIM), F32),
                        pltpu.VMEM((HEAD_DIM, HEAD_DIM), BF16)],
        compiler_params=_cparams(3),
        name="cmp_prompt",
    )(rel_bias, z, z, z, wexp, wexp, gn_t)


def _attn_init(m_ref, l_ref, acc_ref):
    m_ref[...] = jnp.full(m_ref.shape, NEG_INF, F32)
    l_ref[...] = jnp.zeros(l_ref.shape, F32)
    acc_ref[...] = jnp.zeros(acc_ref.shape, F32)


def _attn_update(h, qh, k, v, bias, valid, m_ref, l_ref, acc_ref):
    s = _dot_nt(qh, k) * SCALE + bias
    if valid is not None:
        s = jnp.where(valid, s, NEG_INF)
    m_old = m_ref[h]
    m_new = jnp.maximum(m_old, jnp.max(s, axis=-1, keepdims=True))
    alpha = jnp.exp(m_old - m_new)
    e = jnp.exp(s - m_new)
    if valid is not None:
        e = jnp.where(valid, e, 0.0)
    l_ref[h] = alpha * l_ref[h] + jnp.sum(e, axis=-1, keepdims=True)
    acc_ref[h] = alpha * acc_ref[h] + _dot(e.astype(BF16), v)
    m_ref[h] = m_new


def _attn_prologue(qi, q_ref, k_ref, v_ref, kb_ref, vt_ref, qb_ref, m_ref, acc_ref):
    n_tiles, t = kb_ref.shape[0], kb_ref.shape[1]

    @pl.when(qi == 0)
    def _():
        for kj in range(n_tiles):
            rows = slice(kj * t, (kj + 1) * t)
            kb_ref[kj] = k_ref[rows, :].astype(BF16)
            vt_ref[kj, 0:HEAD_DIM] = jnp.transpose(v_ref[rows, :]).astype(BF16)
            vt_ref[kj, HEAD_DIM:] = jnp.ones((vt_ref.shape[1] - HEAD_DIM, t), BF16)

    for h in range(HPG):
        qb_ref[h] = q_ref[:, h * HEAD_DIM:(h + 1) * HEAD_DIM].astype(BF16)
    m_ref[...] = jnp.full(m_ref.shape, NEG_INF, F32)
    acc_ref[...] = jnp.zeros(acc_ref.shape, F32)


def _attn_update_t(tiles, qb_ref, m_ref, acc_ref):
    heads = range(HPG)
    scores = [[_dot_nt(k, qb_ref[h]) for h in heads] for k, _, _, _ in tiles]
    scores = [[s * (SCALE * LOG2E) + b for s, b in zip(ss, tile[2])] for ss, tile in zip(scores, tiles)]
    scores = [ss if tile[3] is None else [jnp.where(tile[3], s, NEG_INF) for s in ss]
              for ss, tile in zip(scores, tiles)]
    m_old = [m_ref[h] for h in heads]
    m_new = list(m_old)
    for ss in scores:
        m_new = [jnp.maximum(mn, jnp.max(s, axis=0, keepdims=True)) for mn, s in zip(m_new, ss)]
    alpha = [jnp.exp2(mo - mn) for mo, mn in zip(m_old, m_new)]
    e = [[jnp.exp2(s - mn).astype(BF16) for s, mn in zip(ss, m_new)] for ss in scores]
    for h in heads:
        m_ref[h] = m_new[h]
    pv = [[_dot(tile[1], es[h]) for h in heads] for es, tile in zip(e, tiles)]
    for h in heads:
        total = pv[0][h]
        for t in range(1, len(tiles)):
            total = total + pv[t][h]
        acc_ref[h] = alpha[h] * acc_ref[h] + total


def _attn_output_t(h, gates, acc_ref):
    denom = jnp.maximum(acc_ref[h, HEAD_DIM:HEAD_DIM + 1, :], 1e-30)
    o_t = acc_ref[h, 0:HEAD_DIM, :] / denom * gates[h:h + 1, :]
    return jnp.transpose(o_t)


def _sel_prompt_kernel(tab_ref, q_ref, k_ref, v_ref, bd_ref, bo_ref, gate_ref, sel_ref,
                       os_ref, kb_ref, vt_ref, qb_ref, m_ref, acc_ref):
    g = pl.program_id(1)
    qi = pl.program_id(2)
    t = q_ref.shape[0]
    per_tile = t // CMP_BLOCK
    _attn_prologue(qi, q_ref, k_ref, v_ref, kb_ref, vt_ref, qb_ref, m_ref, acc_ref)

    def tile(kj, kind):
        rows = [jnp.broadcast_to(sel_ref[0, 0, pl.ds(kj * per_tile + i, 1), :], (CMP_BLOCK, t))
                for i in range(per_tile)]
        mask = jnp.concatenate(rows, axis=0)
        if kind == "far":
            biases = [mask + tab_ref[N_BUCKETS - 1, g * HPG + h] * LOG2E for h in range(HPG)]
        elif kind == "off":
            biases = [mask + bo_ref[h] for h in range(HPG)]
        else:
            biases = [mask + bd_ref[h] for h in range(HPG)]
        return (kb_ref[kj], vt_ref[kj], biases, None)

    def update(*tiles):
        _attn_update_t(tiles, qb_ref, m_ref, acc_ref)

    n_far = jnp.maximum(qi - 1, 0)

    def pair_body(p, carry):
        update(tile(2 * p, "far"), tile(2 * p + 1, "far"))
        return carry

    lax.fori_loop(0, n_far // 2, pair_body, 0)

    @pl.when(qi == 0)
    def _():
        update(tile(qi, "diag"))

    @pl.when(jnp.logical_and(qi >= 1, n_far % 2 == 0))
    def _():
        update(tile(qi - 1, "off"), tile(qi, "diag"))

    @pl.when(n_far % 2 == 1)
    def _():
        update(tile(qi - 2, "far"), tile(qi - 1, "off"), tile(qi, "diag"))

    gates = _sigmoid(gate_ref[0, 0])
    for h in range(HPG):
        os_ref[:, h * HEAD_DIM:(h + 1) * HEAD_DIM] = _attn_output_t(h, gates, acc_ref)


def _win_prompt_kernel(tab_ref, q_ref, k_ref, v_ref, bd_ref, bo_ref, gate_ref, oc_ref, os_ref,
                       yb_ref, kb_ref, vt_ref, qb_ref, m_ref, acc_ref):
    g = pl.program_id(1)
    qi = pl.program_id(2)
    t = q_ref.shape[0]
    _attn_prologue(qi, q_ref, k_ref, v_ref, kb_ref, vt_ref, qb_ref, m_ref, acc_ref)
    key = lax.broadcasted_iota(I32, (t, t), 0)
    qry = lax.broadcasted_iota(I32, (t, t), 1)

    def tile(kj, kind):
        if kind == "far":
            biases = [tab_ref[N_BUCKETS - 1, g * HPG + h] * LOG2E for h in range(HPG)]
            valid = qry < key
        elif kind == "off":
            biases, valid = [bo_ref[h] for h in range(HPG)], None
        else:
            biases, valid = [bd_ref[h] for h in range(HPG)], None
        return (kb_ref[kj], vt_ref[kj], biases, valid)

    def update(*tiles):
        _attn_update_t(tiles, qb_ref, m_ref, acc_ref)

    @pl.when(qi == 0)
    def _():
        update(tile(qi, "diag"))

    @pl.when(qi == 1)
    def _():
        update(tile(qi, "diag"), tile(qi - 1, "off"))

    @pl.when(qi >= 2)
    def _():
        update(tile(qi, "diag"), tile(qi - 1, "off"), tile(qi - 2, "far"))

    gates = _sigmoid(gate_ref[0, 0])
    for h in range(HPG):
        sl = slice(h * HEAD_DIM, (h + 1) * HEAD_DIM)
        o = _attn_output_t(h, gates, acc_ref)
        yb_ref[:, sl] = (oc_ref[:, sl] + os_ref[:, sl] + o).astype(yb_ref.dtype)


def _attn_prompt_specs(batch, seq, kv_off, gate_idx):
    t = ATT_T
    nq = seq // t
    qcol = OFF_Q // (HPG * HEAD_DIM)
    kcol = kv_off // HEAD_DIM
    rowblk = lambda b, g, i: (b * nq + i, g)
    specs = [pl.BlockSpec(memory_space=pltpu.SMEM),
             pl.BlockSpec((t, HPG * HEAD_DIM), lambda b, g, i: (b * nq + i, qcol + g)),
             pl.BlockSpec((seq, HEAD_DIM), lambda b, g, i: (b, kcol + g)),
             pl.BlockSpec((seq, HEAD_DIM), lambda b, g, i: (b, kcol + N_KV + g)),
             pl.BlockSpec((HPG, t, t), lambda b, g, i: (g, 0, 0)),
             pl.BlockSpec((HPG, t, t), lambda b, g, i: (g, 0, 0)),
             pl.BlockSpec((1, 1, HPG, t), lambda b, g, i: (gate_idx, g, 0, b * nq + i))]
    scratch = [pltpu.VMEM((seq // t, t, HEAD_DIM), BF16),
               pltpu.VMEM((seq // t, HEAD_DIM + DENOM_ROWS, t), BF16),
               pltpu.VMEM((HPG, t, HEAD_DIM), BF16),
               pltpu.VMEM((HPG, 1, t), F32),
               pltpu.VMEM((HPG, HEAD_DIM + DENOM_ROWS, t), F32)]
    return specs, scratch, pl.BlockSpec((t, HPG * HEAD_DIM), rowblk), (batch, N_KV, nq)


def _sel_prompt(z, rel_bias, bd, bo, gn_t, sel, batch, seq):
    specs, scratch, out_spec, grid = _attn_prompt_specs(batch, seq, OFF_KVS, 1)
    t = ATT_T
    nblk = seq // CMP_BLOCK
    specs += [pl.BlockSpec((1, 1, nblk, t), lambda b, g, i: (b, g, 0, i))]
    return pl.pallas_call(
        _sel_prompt_kernel,
        out_shape=jax.ShapeDtypeStruct((batch * seq, Q_W), F32),
        grid=grid, in_specs=specs, out_specs=out_spec, scratch_shapes=scratch,
        compiler_params=_cparams(3), name="sel_prompt",
    )(rel_bias, z, z, z, bd, bo, gn_t, sel)


def _win_prompt(z, rel_bias, bd, bo, gn_t, oc, osel, batch, seq):
    specs, scratch, out_spec, grid = _attn_prompt_specs(batch, seq, OFF_KVW, 2)
    specs += [out_spec, out_spec]
    return pl.pallas_call(
        _win_prompt_kernel,
        out_shape=jax.ShapeDtypeStruct((batch * seq, Q_W), BF16),
        grid=grid, in_specs=specs, out_specs=out_spec, scratch_shapes=scratch,
        compiler_params=_cparams(3), name="win_prompt",
    )(rel_bias, z, z, z, bd, bo, gn_t, oc, osel)


def _cmp_sample_kernel(tab_ref, q_ref, kvc_ref, new_ref, w_ref, gate_ref,
                       oc_ref, idx_ref, kall_ref, vall_ref, imp_ref, *, q_pos):
    nblk = kvc_ref.shape[1]
    nk = kall_ref.shape[0]
    blk = lax.broadcasted_iota(I32, (1, nk), 1)
    dist = q_pos - (blk * CMP_BLOCK + (CMP_BLOCK - 1))
    valid = dist >= 0
    bkt = _bucket(dist)
    new_c = new_ref[0] * w_ref[0]
    kall_ref[...] = jnp.zeros(kall_ref.shape, F32)
    vall_ref[...] = jnp.zeros(vall_ref.shape, F32)
    for g in range(N_KV):
        kall_ref[0:nblk, :] = kvc_ref[0, :, g, :]
        vall_ref[0:nblk, :] = kvc_ref[0, :, N_KV + g, :]
        kall_ref[nblk:nblk + 1, :] = new_c[g:g + 1, :]
        vall_ref[nblk:nblk + 1, :] = new_c[N_KV + g:N_KV + g + 1, :]
        kb = kall_ref[...].astype(BF16)
        vb = vall_ref[...].astype(BF16)
        biases = _bias_lookup(bkt, tab_ref, [g * HPG + h for h in range(HPG)])
        rows = lax.broadcasted_iota(I32, (HROWS, nk), 0)
        bias = jnp.zeros((HROWS, nk), F32)
        for h in range(HPG):
            bias = jnp.where(rows == h, biases[h], bias)
        s = _dot_nt(q_ref[0, g].astype(BF16), kb) * SCALE + bias
        p = _masked_softmax(s, valid)
        gate = _sigmoid(gate_ref[0, 0, g])
        oc_ref[0, g] = gate * _dot(p.astype(BF16), vb)
        head_row = jnp.where(rows < HPG, p, 0.0)
        imp_ref[g:g + 1, :] = jnp.sum(head_row, axis=0, keepdims=True)

    imp = imp_ref[0:N_KV, :]
    blk4 = lax.broadcasted_iota(I32, (N_KV, nk), 1)
    cur = q_pos // CMP_BLOCK
    score = _block_scores(imp, blk4, cur)
    rank = jnp.zeros((N_KV, nk), F32)
    for j in range(nblk + 1):
        col = score[:, j:j + 1]
        rank = rank + _ahead(col, score, blk4 > j)
    rank = jnp.where(blk4 <= nblk, rank, float(nk))
    blkf = blk4.astype(F32)
    slot = lax.broadcasted_iota(I32, (N_KV, N_SEL), 1)
    idx = jnp.zeros((N_KV, N_SEL), F32)
    for r in range(N_SEL):
        pick = jnp.sum(jnp.where(rank == float(r), blkf, 0.0), axis=-1, keepdims=True)
        idx = jnp.where(slot == r, pick, idx)
    idx_ref[0] = idx.astype(I32)


def _cmp_sample(rel_bias, q8, kvc, kvc_new, w8, gates8, q_pos):
    b, nblk = kvc.shape[:2]
    nk = 2 * HEAD_DIM
    assert nblk + 1 <= nk and q_pos // CMP_BLOCK == nblk
    return pl.pallas_call(
        functools.partial(_cmp_sample_kernel, q_pos=q_pos),
        out_shape=(jax.ShapeDtypeStruct((b, N_KV, HROWS, HEAD_DIM), F32),
                   jax.ShapeDtypeStruct((b, N_KV, N_SEL), I32)),
        grid=(b,),
        in_specs=[pl.BlockSpec(memory_space=pltpu.SMEM),
                  pl.BlockSpec((1, N_KV, HROWS, HEAD_DIM), lambda i: (i, 0, 0, 0)),
                  pl.BlockSpec((1, nblk, KV_ROWS, HEAD_DIM), lambda i: (i, 0, 0, 0)),
                  pl.BlockSpec((1, KV_ROWS, HEAD_DIM), lambda i: (i, 0, 0)),
                  pl.BlockSpec((CMP_BLOCK, KV_ROWS, HEAD_DIM), lambda i: (0, 0, 0)),
                  pl.BlockSpec((1, 1, N_KV, HROWS, 1), lambda i: (i, 0, 0, 0, 0))],
        out_specs=(pl.BlockSpec((1, N_KV, HROWS, HEAD_DIM), lambda i: (i, 0, 0, 0)),
                   pl.BlockSpec((1, N_KV, N_SEL), lambda i: (i, 0, 0))),
        scratch_shapes=[pltpu.VMEM((nk, HEAD_DIM), F32), pltpu.VMEM((nk, HEAD_DIM), F32),
                        pltpu.VMEM((HROWS, nk), F32)],
        compiler_params=_cparams(1),
        name="cmp_sample",
    )(rel_bias, q8, kvc, kvc_new, w8, gates8)


def _sel_bias_kernel(tab_ref, o_ref, *, q_pos):
    g = pl.program_id(0)
    shape = o_ref.shape[1:]
    blk = lax.broadcasted_iota(I32, shape, 0)
    rows = lax.broadcasted_iota(I32, shape, 1)
    offs = lax.broadcasted_iota(I32, shape, 2)
    dist = q_pos - (blk * CMP_BLOCK + offs)
    biases = _bias_lookup(_bucket(dist), tab_ref, [g * HPG + h for h in range(HPG)])
    bias = jnp.zeros(shape, F32)
    for h in range(HPG):
        bias = jnp.where(rows == h, biases[h], bias)
    o_ref[0] = jnp.where(dist >= 0, bias, NEG_INF)


def _sel_bias(rel_bias, q_pos, n_blocks):
    return pl.pallas_call(
        functools.partial(_sel_bias_kernel, q_pos=q_pos),
        out_shape=jax.ShapeDtypeStruct((N_KV, n_blocks, HROWS, CMP_BLOCK), F32),
        grid=(N_KV,),
        in_specs=[pl.BlockSpec(memory_space=pltpu.SMEM)],
        out_specs=pl.BlockSpec((1, n_blocks, HROWS, CMP_BLOCK), lambda g: (g, 0, 0, 0)),
        compiler_params=_cparams(1),
        name="sel_bias",
    )(rel_bias)


def _sel_sample_kernel(pt_ref, idx_ref, bias_ref, q_ref, *refs, n_past):
    nb = SEL_BLOCKS_PER_STEP
    blk_refs = refs[:N_KV * nb]
    new_ref, gate_ref, os_ref, kcat_ref, vcat_ref, bcat_ref, m_ref, l_ref, acc_ref = refs[N_KV * nb:]
    b = pl.program_id(0)
    step = pl.program_id(1)

    @pl.when(step == 0)
    def _():
        _attn_init(m_ref, l_ref, acc_ref)

    first = lax.broadcasted_iota(I32, (CMP_BLOCK, HEAD_DIM), 0) == 0
    for g in range(N_KV):
        kn = jnp.where(first, new_ref[0][g:g + 1, :], 0.0)
        vn = jnp.where(first, new_ref[0][N_KV + g:N_KV + g + 1, :], 0.0)
        for j in range(nb):
            n = idx_ref[b, g, step * nb + j]
            is_new = n >= n_past
            blk_ref = blk_refs[g * nb + j]
            keys = slice(j * CMP_BLOCK, (j + 1) * CMP_BLOCK)
            kcat_ref[keys, :] = jnp.where(is_new, kn, blk_ref[0, :, g, :]).astype(BF16)
            vcat_ref[keys, :] = jnp.where(is_new, vn, blk_ref[0, :, N_KV + g, :]).astype(BF16)
            bcat_ref[:, keys] = bias_ref[g, n]
        _attn_update(g, q_ref[0, g].astype(BF16), kcat_ref[...], vcat_ref[...], bcat_ref[...], None,
                     m_ref, l_ref, acc_ref)

    @pl.when(step == pl.num_programs(1) - 1)
    def _():
        for g in range(N_KV):
            o = acc_ref[g] / jnp.maximum(l_ref[g], 1e-30)
            os_ref[0, g] = _sigmoid(gate_ref[0, 0, g]) * o


def _sel_sample(page_table, idx, rel_bias, q8, cache_half, kvs_new, gates8, q_pos):
    b = q8.shape[0]
    nb = SEL_BLOCKS_PER_STEP
    n_past = q_pos // CMP_BLOCK
    per_page = PAGE_SIZE // CMP_BLOCK
    bias = _sel_bias(rel_bias, q_pos, n_past + 1)

    def blk_spec(g, j):
        def index(i, s, pt, ix):
            n = jnp.minimum(ix[i, g, s * nb + j], n_past - 1)
            return (pt[i, n // per_page] * per_page + n % per_page, 0, 0, 0)
        return pl.BlockSpec((1, CMP_BLOCK, KV_ROWS, HEAD_DIM), index)

    hspec = pl.BlockSpec((1, N_KV, HROWS, HEAD_DIM), lambda i, s, pt, ix: (i, 0, 0, 0))
    grid_spec = pltpu.PrefetchScalarGridSpec(
        num_scalar_prefetch=2,
        grid=(b, N_SEL // nb),
        in_specs=[pl.BlockSpec(bias.shape, lambda i, s, pt, ix: (0, 0, 0, 0)), hspec]
        + [blk_spec(g, j) for g in range(N_KV) for j in range(nb)]
        + [pl.BlockSpec((1, KV_ROWS, HEAD_DIM), lambda i, s, pt, ix: (i, 0, 0)),
           pl.BlockSpec((1, 1, N_KV, HROWS, 1), lambda i, s, pt, ix: (i, 1, 0, 0, 0))],
        out_specs=hspec,
        scratch_shapes=[pltpu.VMEM((nb * CMP_BLOCK, HEAD_DIM), BF16),
                        pltpu.VMEM((nb * CMP_BLOCK, HEAD_DIM), BF16),
                        pltpu.VMEM((HROWS, nb * CMP_BLOCK), F32),
                        pltpu.VMEM((N_KV, HROWS, 1), F32), pltpu.VMEM((N_KV, HROWS, 1), F32),
                        pltpu.VMEM((N_KV, HROWS, HEAD_DIM), F32)],
    )
    return pl.pallas_call(
        functools.partial(_sel_sample_kernel, n_past=n_past),
        out_shape=jax.ShapeDtypeStruct((b, N_KV, HROWS, HEAD_DIM), F32),
        grid_spec=grid_spec,
        compiler_params=_cparams(2),
        name="sel_sample",
    )(page_table, idx, bias, q8, *([cache_half] * (N_KV * nb)), kvs_new, gates8)


def _win_sample_kernel(tab_ref, q_ref, win_ref, new_ref, gate_ref, oc_ref, os_ref,
                       yb_ref, nwin_ref, kall_ref, vall_ref):
    wc = win_ref.shape[1]
    nk = kall_ref.shape[0]
    nwin_ref[0, 0:wc - 1] = win_ref[0, 1:wc]
    nwin_ref[0, wc - 1] = new_ref[0]
    kidx = lax.broadcasted_iota(I32, (1, nk), 1)
    dist = wc - kidx
    valid = jnp.where(dist >= 0, dist, WINDOW) < WINDOW
    bkt = _bucket(dist)
    kall_ref[...] = jnp.zeros(kall_ref.shape, F32)
    vall_ref[...] = jnp.zeros(vall_ref.shape, F32)
    for g in range(N_KV):
        kall_ref[0:wc, :] = win_ref[0, :, g, :]
        vall_ref[0:wc, :] = win_ref[0, :, N_KV + g, :]
        kall_ref[wc:wc + 1, :] = new_ref[0][g:g + 1, :]
        vall_ref[wc:wc + 1, :] = new_ref[0][N_KV + g:N_KV + g + 1, :]
        biases = _bias_lookup(bkt, tab_ref, [g * HPG + h for h in range(HPG)])
        rows = lax.broadcasted_iota(I32, (HROWS, nk), 0)
        bias = jnp.zeros((HROWS, nk), F32)
        for h in range(HPG):
            bias = jnp.where(rows == h, biases[h], bias)
        s = _dot_nt(q_ref[0, g].astype(BF16), kall_ref[...].astype(BF16)) * SCALE + bias
        p = _masked_softmax(s, valid)
        o = _dot(p.astype(BF16), vall_ref[...].astype(BF16))
        yb_ref[0, g] = oc_ref[0, g] + os_ref[0, g] + _sigmoid(gate_ref[0, 0, g]) * o


def _win_sample(rel_bias, q8, win, kvw_new, gates8, oc, osel):
    b, wc = win.shape[:2]
    nk = wc + HEAD_DIM
    hspec = pl.BlockSpec((1, N_KV, HROWS, HEAD_DIM), lambda i: (i, 0, 0, 0))
    wspec = pl.BlockSpec((1, wc, KV_ROWS, HEAD_DIM), lambda i: (i, 0, 0, 0))
    return pl.pallas_call(
        _win_sample_kernel,
        out_shape=(jax.ShapeDtypeStruct((b, N_KV, HROWS, HEAD_DIM), F32),
                   jax.ShapeDtypeStruct(win.shape, F32)),
        grid=(b,),
        in_specs=[pl.BlockSpec(memory_space=pltpu.SMEM),
                  hspec,
                  wspec,
                  pl.BlockSpec((1, KV_ROWS, HEAD_DIM), lambda i: (i, 0, 0)),
                  pl.BlockSpec((1, 1, N_KV, HROWS, 1), lambda i: (i, 2, 0, 0, 0)),
                  hspec, hspec],
        out_specs=(hspec, wspec),
        scratch_shapes=[pltpu.VMEM((nk, HEAD_DIM), F32), pltpu.VMEM((nk, HEAD_DIM), F32)],
        compiler_params=_cparams(1),
        name="win_sample",
    )(rel_bias, q8, win, kvw_new, gates8, oc, osel)


def _in_proj(x, norm_w, w_in_t, *, tm, tm_norm, pages=None):
    xn = _rmsnorm(x, norm_w, BF16, tm_norm)
    return _in_proj_all(xn, w_in_t, tm=tm, tn=IN_TN, pages=pages)


def _out_and_ffn(x, ya, yb, z, w_proj_a, w_proj_b, w_out, norm_ffn, w_gate, w_up, w_down,
                 norm_final, *, tm, tm_norm):
    d_ff = w_gate.shape[1]
    mix = _mix(ya, yb, w_proj_a, w_proj_b, z, tm=tm, tn=512)
    h = _mm(mix, w_out, n_cols=D_MODEL, tm=tm, tn=512, res=x, name="out_proj")
    hn = _rmsnorm(h, norm_ffn, BF16, tm_norm)
    ff = _swiglu(hn, w_gate, w_up, tm=tm, tn=256)
    half = d_ff // 2
    y = _mm(ff, w_down, n_cols=D_MODEL, tm=tm, tn=256, res=h, k_block=0, tk=half, name="ffn_down0")
    y = _mm(ff, w_down, n_cols=D_MODEL, tm=tm, tn=256, res=y, k_block=1, tk=half, name="ffn_down1")
    return _rmsnorm(y, norm_final, F32, tm_norm)


def kernel(x_prompt, x_sample, cache_cmp_kv, cache_sel_kv, cache_win_kv, state_rglru_h, state_conv,
           page_table, rel_bias, norm_mix, w_in, conv_w, conv_b, lru_wa, lru_ba, lru_wi, lru_bi,
           lru_lambda, nsa_w_cmp, w_proj_a, w_proj_b, w_out, norm_ffn, w_gate, w_up, w_down,
           norm_final):
    depth = w_in.shape[0]
    assert depth == 1, "single-layer trunk"
    bp, tp, _ = x_prompt.shape
    bs, ts, _ = x_sample.shape
    assert ts == 1
    n_pool = cache_cmp_kv.shape[1]
    past_len = page_table.shape[1] * PAGE_SIZE
    wc = cache_win_kv.shape[2]

    w_in0 = jnp.swapaxes(w_in[0], 0, 1)
    wexp = jnp.broadcast_to(nsa_w_cmp[0][..., None], (2, N_KV, CMP_BLOCK, HEAD_DIM))
    w8 = jnp.broadcast_to(nsa_w_cmp[0].reshape(KV_ROWS, CMP_BLOCK).T[..., None],
                          (CMP_BLOCK, KV_ROWS, HEAD_DIM))
    layer_w = (conv_w[0], conv_b[0], lru_wa[0], lru_ba[0], lru_wi[0], lru_bi[0], lru_lambda[0])
    tail_w = (w_proj_a[0], w_proj_b[0], w_out[0], norm_ffn[0], w_gate[0], w_up[0], w_down[0],
              norm_final)

    mp = bp * tp
    xp = x_prompt.reshape(mp, D_MODEL)
    per_page = PAGE_SIZE // CMP_BLOCK
    cache_c = cache_cmp_kv.reshape(n_pool, PAGE_SIZE, KV_ROWS, HEAD_DIM)
    z, kvc = _in_proj(xp, norm_mix[0], w_in0, tm=1024, tm_norm=256, pages=(cache_c, page_table, w8))
    ya, conv_p, h_p = _rglru_prompt(z, bp, tp, *layer_w, tt=128)
    gn_t = z[:, Z_GN:Z_GN + 3 * N_HEADS].reshape(mp, 3, N_KV, HPG).transpose(1, 2, 3, 0)
    bd, bo = _bias_tiles(rel_bias, ATT_T)
    oc, sel = _cmp_prompt(z, rel_bias, wexp, gn_t, bp, tp)
    osel = _sel_prompt(z, rel_bias, bd, bo, gn_t, sel, bp, tp)
    yb = _win_prompt(z, rel_bias, bd, bo, gn_t, oc, osel, bp, tp)
    y_prompt = _out_and_ffn(xp, ya, yb, z, *tail_w, tm=1024, tm_norm=256)

    kv_shape = (1, bp, tp // PAGE_SIZE, PAGE_SIZE, 2, N_KV, HEAD_DIM)
    cmp_p = z[:, OFF_KVC:OFF_KVS].reshape(kv_shape)
    sel_p = z[:, OFF_KVS:OFF_KVW].reshape(kv_shape)
    wlen = min(WINDOW, tp)
    win_p = z[:, OFF_KVW:MAIN_W].reshape(bp, tp, 2, N_KV, HEAD_DIM)[None, :, tp - wlen:]

    xs = x_sample.reshape(bs, D_MODEL)
    zs = _in_proj(xs, norm_mix[0], w_in0, tm=bs, tm_norm=bs)
    buf_t = state_conv[0].transpose(1, 0, 2)
    ya_s, h_s = _rglru_step(zs, buf_t, state_rglru_h[0], *layer_w)

    q8 = jnp.pad(zs[:, OFF_Q:OFF_KVC].reshape(bs, N_KV, HPG, HEAD_DIM),
                 ((0, 0), (0, 0), (0, HROWS - HPG), (0, 0)))
    gates8 = jnp.pad(zs[:, Z_GN:Z_GN + 3 * N_HEADS].reshape(bs, 3, N_KV, HPG),
                     ((0, 0), (0, 0), (0, 0), (0, HROWS - HPG)))[..., None]
    kvc_new = zs[:, OFF_KVC:OFF_KVS].reshape(bs, KV_ROWS, HEAD_DIM)
    kvs_new = zs[:, OFF_KVS:OFF_KVW].reshape(bs, KV_ROWS, HEAD_DIM)
    kvw_new = zs[:, OFF_KVW:MAIN_W].reshape(bs, KV_ROWS, HEAD_DIM)
    cache_s = cache_sel_kv.reshape(n_pool * per_page, CMP_BLOCK, KV_ROWS, HEAD_DIM)
    win = cache_win_kv.reshape(bs, wc, KV_ROWS, HEAD_DIM)

    oc_s, idx = _cmp_sample(rel_bias, q8, kvc, kvc_new, w8, gates8, past_len)
    os_s = _sel_sample(page_table, idx, rel_bias, q8, cache_s, kvs_new, gates8, past_len)
    yb8, win_next = _win_sample(rel_bias, q8, win, kvw_new, gates8, oc_s, os_s)
    yb_s = yb8[:, :, :HPG].reshape(bs, Q_W).astype(BF16)
    y_sample = _out_and_ffn(xs, ya_s, yb_s, zs, *tail_w, tm=bs, tm_norm=bs)

    row_shape = (1, bs, 1, 2, N_KV, HEAD_DIM)
    cmp_s = kvc_new.reshape(row_shape)
    sel_s = kvs_new.reshape(row_shape)
    win_s = win_next.reshape(1, bs, wc, 2, N_KV, HEAD_DIM)
    conv_s = jnp.concatenate([state_conv[0], zs[:, None, OFF_XR:OFF_GR]], axis=1)[:, 1:][None]

    return (y_prompt.reshape(bp, tp, D_MODEL), y_sample.reshape(bs, ts, D_MODEL),
            cmp_p, sel_p, win_p, h_p.reshape(1, bp, D_RNN), conv_p[None],
            cmp_s, sel_s, win_s, h_s[None], conv_s)
```

```python
import functools
import math

import jax
import jax.numpy as jnp
from jax import lax
from jax.experimental import pallas as pl
from jax.experimental.pallas import tpu as pltpu

F32 = jnp.float32
BF16 = jnp.bfloat16
I32 = jnp.int32

D_MODEL = 4096
D_RNN = 2048
N_RNN_BLOCKS = 16
RNN_BLOCK = D_RNN // N_RNN_BLOCKS
CONV_W = 4
LRU_C = 8.0
N_HEADS = 16
HEAD_DIM = 128
N_KV = 4
HPG = N_HEADS // N_KV
CMP_BLOCK = 64
N_SEL = 16
WINDOW = 512
N_BUCKETS = 32
MAX_DISTANCE = 128
PAGE_SIZE = 128
Q_W = N_HEADS * HEAD_DIM
KV_W = 2 * N_KV * HEAD_DIM
EPS = 1e-6
NEG_INF = -1e30
FORCE = 1e9
SCALE = HEAD_DIM ** -0.5
LOG2E = math.log2(math.e)
DENOM_ROWS = 16

OFF_XR = 0
OFF_GR = D_RNN
OFF_Q = 2 * D_RNN
OFF_KVC = OFF_Q + Q_W
OFF_KVS = OFF_KVC + KV_W
OFF_KVW = OFF_KVS + KV_W
OFF_GN = OFF_KVW + KV_W
OFF_GA = OFF_GN + 3 * N_HEADS
MAIN_W = OFF_GN
IN_TN = 512
Z_GN = MAIN_W
Z_GA = MAIN_W + IN_TN
Z_GB = Z_GA + D_MODEL

ATT_T = 256
HROWS = 8
KV_ROWS = 2 * N_KV
PAGES_PER_STEP = 8
SEL_BLOCKS_PER_STEP = 4
VMEM_LIMIT = 56 * 1024 * 1024


def _cparams(n_axes):
    return pltpu.CompilerParams(dimension_semantics=("arbitrary",) * n_axes,
                                vmem_limit_bytes=VMEM_LIMIT)


def _sigmoid(x):
    return 1.0 / (1.0 + jnp.exp(-x))


def _gelu_tanh(x):
    return 0.5 * x * (1.0 + jnp.tanh(math.sqrt(2.0 / math.pi) * (x + 0.044715 * (x * x * x))))


def _dot(a, b):
    return jnp.dot(a, b, preferred_element_type=F32)


def _dot_nt(a, b):
    return lax.dot_general(a, b, (((1,), (1,)), ((), ())), preferred_element_type=F32)


def _bucket(dist):
    n = jnp.maximum(dist, 0)
    exact = N_BUCKETS // 2
    nf = jnp.maximum(n, 1).astype(F32)
    large = exact + (jnp.log(nf / exact) / math.log(MAX_DISTANCE / exact)
                     * (N_BUCKETS - exact)).astype(I32)
    return jnp.where(n < exact, n, jnp.minimum(large, N_BUCKETS - 1))


def _bias_lookup(bkt, tab_ref, cols):
    outs = [jnp.zeros(bkt.shape, F32) for _ in cols]
    for b in range(N_BUCKETS):
        eq = bkt == b
        outs = [jnp.where(eq, tab_ref[b, c], o) for c, o in zip(cols, outs)]
    return outs


def _block_scores(imp, blk, cur):
    score = jnp.where(blk == 0, FORCE, jnp.where(blk == cur, FORCE,
                                                 jnp.where(blk == cur - 1, FORCE, imp)))
    return jnp.where(blk <= cur, score, -FORCE)


def _ahead(col, score, later):
    return jnp.where(later, jnp.where(col >= score, 1.0, 0.0), jnp.where(col > score, 1.0, 0.0))


def _masked_softmax(s, valid):
    s = jnp.where(valid, s, NEG_INF)
    m = jnp.max(s, axis=-1, keepdims=True)
    e = jnp.where(valid, jnp.exp(s - m), 0.0)
    return e / jnp.maximum(jnp.sum(e, axis=-1, keepdims=True), 1e-30)


def _rmsnorm_kernel(x_ref, g_ref, o_ref):
    x = x_ref[...]
    y = x * lax.rsqrt(jnp.mean(x * x, axis=-1, keepdims=True) + EPS)
    o_ref[...] = (y * g_ref[...]).astype(o_ref.dtype)


def _rmsnorm(x, g, out_dtype, tm):
    m, d = x.shape
    return pl.pallas_call(
        _rmsnorm_kernel,
        out_shape=jax.ShapeDtypeStruct((m, d), out_dtype),
        grid=(m // tm,),
        in_specs=[pl.BlockSpec((tm, d), lambda i: (i, 0)),
                  pl.BlockSpec((1, d), lambda i: (0, 0))],
        out_specs=pl.BlockSpec((tm, d), lambda i: (i, 0)),
        compiler_params=_cparams(1),
        name="rmsnorm",
    )(x, g.reshape(1, d))


def _rider_store(os_ref, compute):
    last = pl.program_id(0) == pl.num_programs(0) - 1

    @pl.when(last)
    def _():
        os_ref[0, 0] = compute().astype(os_ref.dtype)

    @pl.when(jnp.logical_not(last))
    def _():
        os_ref[0, 0] = jnp.zeros(os_ref.shape[2:], os_ref.dtype)


def _rider_spec(rows, tn):
    return pl.BlockSpec((1, 1, rows, tn), lambda i, j, *_: (i, j, 0, 0))


def _rider_rows(os):
    _, nj, rows, tn = os.shape
    return os[-1].transpose(1, 0, 2).reshape(rows, nj * tn)


def _mm_res_kernel(a_ref, w_ref, r_ref, as_ref, rs_ref, o_ref, os_ref):
    w = w_ref[...].astype(BF16)
    o_ref[...] = r_ref[...] + _dot(a_ref[...], w)
    _rider_store(os_ref, lambda: rs_ref[...] + _dot(as_ref[...], w))


def _mm_res(a, w, res, a_s, res_s, *, tm, tn, k_block=0, tk=None, name="mm"):
    m = a.shape[0]
    bs = a_s.shape[0]
    n_cols = w.shape[1]
    tk = a.shape[1] if tk is None else tk
    grid = (m // tm, n_cols // tn)
    out, out_s = pl.pallas_call(
        _mm_res_kernel,
        out_shape=(jax.ShapeDtypeStruct((m, n_cols), F32),
                   jax.ShapeDtypeStruct(grid + (bs, tn), F32)),
        grid=grid,
        in_specs=[pl.BlockSpec((tm, tk), lambda i, j: (i, k_block)),
                  pl.BlockSpec((tk, tn), lambda i, j: (k_block, j)),
                  pl.BlockSpec((tm, tn), lambda i, j: (i, j)),
                  pl.BlockSpec((bs, tk), lambda i, j: (0, k_block)),
                  pl.BlockSpec((bs, tn), lambda i, j: (0, j))],
        out_specs=(pl.BlockSpec((tm, tn), lambda i, j: (i, j)), _rider_spec(bs, tn)),
        compiler_params=_cparams(2),
        name=name,
    )(a, w, res, a_s, res_s)
    return out, _rider_rows(out_s)


def _compress_pages(page_refs, w_ref, o_ref):
    per_page = PAGE_SIZE // CMP_BLOCK
    w = w_ref[...]
    for k, page_ref in enumerate(page_refs):
        x = page_ref[0].reshape(per_page, CMP_BLOCK, KV_ROWS, HEAD_DIM)
        o_ref[0, per_page * k:per_page * (k + 1)] = jnp.sum(x * w[None], axis=1)


def _in_proj_pages_kernel(pt_ref, a_ref, wt_ref, as_ref, *refs):
    page_refs, w8_ref, o_ref, os_ref, kvc_ref = refs[:-4], refs[-4], refs[-3], refs[-2], refs[-1]
    w = wt_ref[...].astype(BF16)
    o_ref[...] = _dot_nt(a_ref[...], w)
    _rider_store(os_ref, lambda: _dot_nt(as_ref[...], w))
    _compress_pages(page_refs, w8_ref, kvc_ref)


def _in_proj_all(a, wt, a_s, pages, *, tm, tn):
    m, k = a.shape
    bs = a_s.shape[0]
    assert MAIN_W % tn == 0 and Z_GN == MAIN_W and Z_GA == MAIN_W + tn and 3 * N_HEADS <= tn
    n_gate_blocks = 2 * D_MODEL // tn
    nb = MAIN_W // tn + 1 + n_gate_blocks
    assert OFF_GA + n_gate_blocks * tn == wt.shape[0]

    def w_index(i, j, *_):
        row = jnp.where(j <= MAIN_W // tn, j * tn, OFF_GA + (j - MAIN_W // tn - 1) * tn)
        return (pl.multiple_of(row, 8), 0)

    grid = (m // tm, nb)
    in_specs = [pl.BlockSpec((tm, k), lambda i, j, *_: (i, 0)),
                pl.BlockSpec((pl.Element(tn), pl.Element(k)), w_index),
                pl.BlockSpec((bs, k), lambda i, j, *_: (0, 0))]
    out_spec = pl.BlockSpec((tm, tn), lambda i, j, *_: (i, j))
    out_shape = jax.ShapeDtypeStruct((m, nb * tn), F32)

    cache, page_table, w8 = pages
    b, n_pages = page_table.shape
    per_page = PAGE_SIZE // CMP_BLOCK
    pps = PAGES_PER_STEP
    chunks = n_pages // pps
    assert grid[0] * grid[1] >= b * chunks

    def chunk(i, j):
        t = jnp.minimum(i * nb + j, b * chunks - 1)
        return t // chunks, t % chunks

    def page_index(k):
        def index(i, j, pt):
            seq, c = chunk(i, j)
            return (pt[seq, c * pps + k], 0, 0, 0)
        return index

    def kvc_index(i, j, pt):
        seq, c = chunk(i, j)
        return (seq, c, 0, 0)

    grid_spec = pltpu.PrefetchScalarGridSpec(
        num_scalar_prefetch=1,
        grid=grid,
        in_specs=in_specs
        + [pl.BlockSpec((1, PAGE_SIZE, KV_ROWS, HEAD_DIM), page_index(k)) for k in range(pps)]
        + [pl.BlockSpec((CMP_BLOCK, KV_ROWS, HEAD_DIM), lambda i, j, pt: (0, 0, 0))],
        out_specs=(out_spec, _rider_spec(bs, tn),
                   pl.BlockSpec((1, pps * per_page, KV_ROWS, HEAD_DIM), kvc_index)),
    )
    z, z_s, kvc = pl.pallas_call(
        _in_proj_pages_kernel,
        out_shape=(out_shape,
                   jax.ShapeDtypeStruct(grid + (bs, tn), F32),
                   jax.ShapeDtypeStruct((b, n_pages * per_page, KV_ROWS, HEAD_DIM), F32)),
        grid_spec=grid_spec,
        compiler_params=_cparams(2),
        name="in_proj_pages",
    )(page_table, a, wt, a_s, *([cache] * pps), w8)
    return z, _rider_rows(z_s), kvc


def _swiglu_kernel(a_ref, wg_ref, wu_ref, as_ref, o_ref, os_ref):
    wg = wg_ref[...].astype(BF16)
    wu = wu_ref[...].astype(BF16)

    def swiglu(a):
        g = _dot(a, wg)
        return (g * _sigmoid(g)) * _dot(a, wu)

    o_ref[...] = swiglu(a_ref[...]).astype(o_ref.dtype)
    _rider_store(os_ref, lambda: swiglu(as_ref[...]))


def _swiglu(a, wg, wu, a_s, *, tm, tn):
    m, k = a.shape
    bs = a_s.shape[0]
    n = wg.shape[1]
    grid = (m // tm, n // tn)
    out, out_s = pl.pallas_call(
        _swiglu_kernel,
        out_shape=(jax.ShapeDtypeStruct((m, n), BF16), jax.ShapeDtypeStruct(grid + (bs, tn), BF16)),
        grid=grid,
        in_specs=[pl.BlockSpec((tm, k), lambda i, j: (i, 0)),
                  pl.BlockSpec((k, tn), lambda i, j: (0, j)),
                  pl.BlockSpec((k, tn), lambda i, j: (0, j)),
                  pl.BlockSpec((bs, k), lambda i, j: (0, 0))],
        out_specs=(pl.BlockSpec((tm, tn), lambda i, j: (i, j)), _rider_spec(bs, tn)),
        compiler_params=_cparams(2),
        name="ffn_gate_up",
    )(a, wg, wu, a_s)
    return out, _rider_rows(out_s)


def _mix_kernel(ya_ref, yb_ref, wa_ref, wb_ref, ga_ref, gb_ref, yas_ref, ybs_ref, gas_ref, gbs_ref,
                o_ref, os_ref):
    wa = wa_ref[...].astype(BF16)
    wb = wb_ref[...].astype(BF16)

    def merge(ya, yb, ga, gb):
        return _sigmoid(ga) * _dot(ya, wa) + _sigmoid(gb) * _dot(yb, wb)

    o_ref[...] = merge(ya_ref[...], yb_ref[...], ga_ref[...], gb_ref[...]).astype(o_ref.dtype)
    _rider_store(os_ref, lambda: merge(yas_ref[...], ybs_ref[...], gas_ref[...], gbs_ref[...]))


def _mix(ya, yb, wa, wb, z, ya_s, yb_s, z_s, *, tm, tn):
    m, k = ya.shape
    bs = ya_s.shape[0]
    n = wa.shape[1]
    assert Z_GA % tn == 0 and Z_GB % tn == 0
    grid = (m // tm, n // tn)
    gate = lambda off: pl.BlockSpec((tm, tn), lambda i, j: (i, j + off // tn))
    gate_s = lambda off: pl.BlockSpec((bs, tn), lambda i, j: (0, j + off // tn))
    out, out_s = pl.pallas_call(
        _mix_kernel,
        out_shape=(jax.ShapeDtypeStruct((m, n), BF16), jax.ShapeDtypeStruct(grid + (bs, tn), BF16)),
        grid=grid,
        in_specs=[pl.BlockSpec((tm, k), lambda i, j: (i, 0)),
                  pl.BlockSpec((tm, k), lambda i, j: (i, 0)),
                  pl.BlockSpec((k, tn), lambda i, j: (0, j)),
                  pl.BlockSpec((k, tn), lambda i, j: (0, j)),
                  gate(Z_GA), gate(Z_GB),
                  pl.BlockSpec((bs, k), lambda i, j: (0, 0)),
                  pl.BlockSpec((bs, k), lambda i, j: (0, 0)),
                  gate_s(Z_GA), gate_s(Z_GB)],
        out_specs=(pl.BlockSpec((tm, tn), lambda i, j: (i, j)), _rider_spec(bs, tn)),
        compiler_params=_cparams(2),
        name="branch_merge",
    )(ya, yb, wa, wb, z, z, ya_s, yb_s, z_s, z_s)
    return out, _rider_rows(out_s)


def _lru_gates(xc, wa_ref, ba_ref, wi_ref, bi_ref, lam_ref, a_ref, u_ref):
    for n in range(N_RNN_BLOCKS):
        sl = slice(n * RNN_BLOCK, (n + 1) * RNN_BLOCK)
        xn = xc[:, sl]
        xb = xn.astype(BF16)
        r = _sigmoid(_dot(xb, wa_ref[n].astype(BF16)) + ba_ref[:, sl])
        i = _sigmoid(_dot(xb, wi_ref[n].astype(BF16)) + bi_ref[:, sl])
        neg_lam = -lam_ref[:, sl]
        softplus = jnp.maximum(neg_lam, 0.0) + jnp.log1p(jnp.exp(-jnp.abs(neg_lam)))
        log_a = (-LRU_C * r) * softplus
        a = jnp.exp(log_a)
        a_ref[:, sl] = a
        u_ref[:, sl] = jnp.sqrt(-jnp.tanh(log_a) * (a * a + 1.0)) * (i * xn)


def _rglru_prompt_kernel(xr_ref, gr_ref, cw_ref, cb_ref, wa_ref, ba_ref, wi_ref, bi_ref, lam_ref,
                         ya_ref, conv_ref, hl_ref, xp_ref, a_ref, u_ref, h_ref, carry_ref):
    ti = pl.program_id(1)
    tt, c = xr_ref.shape

    @pl.when(ti == 0)
    def _():
        xp_ref[0:8, :] = jnp.zeros((8, c), F32)
        carry_ref[...] = jnp.zeros((1, c), F32)

    x = xr_ref[...]
    xp_ref[8:8 + tt, :] = x
    xc = cb_ref[...]
    for k in range(CONV_W):
        xc = xc + cw_ref[k:k + 1, :] * xp_ref[5 + k:5 + k + tt, :]
    xp_ref[0:8, :] = xp_ref[tt:tt + 8, :]

    _lru_gates(xc, wa_ref, ba_ref, wi_ref, bi_ref, lam_ref, a_ref, u_ref)

    row = lax.broadcasted_iota(I32, (8, c), 0)

    def chunk(ci, carry):
        off = pl.multiple_of(ci * 8, 8)
        a = a_ref[pl.ds(off, 8), :]
        b = u_ref[pl.ds(off, 8), :]
        for s in (1, 2, 4):
            keep = row >= s
            b = jnp.where(keep, a * pltpu.roll(b, s, 0) + b, b)
            a = jnp.where(keep, a * pltpu.roll(a, s, 0), a)
        h = a * carry + b
        h_ref[pl.ds(off, 8), :] = h
        return h[7:8, :]

    carry = lax.fori_loop(0, tt // 8, chunk, carry_ref[...])
    carry_ref[...] = carry
    ya_ref[...] = (h_ref[...] * _gelu_tanh(gr_ref[...])).astype(ya_ref.dtype)

    @pl.when(ti == pl.num_programs(1) - 1)
    def _():
        conv_ref[0] = x[tt - (CONV_W - 1):, :]
        hl_ref[0] = carry


def _rglru_prompt(z, batch, seq, conv_w, conv_b, lru_wa, lru_ba, lru_wi, lru_bi, lam, *, tt):
    nt = seq // tt
    c = D_RNN
    row = lambda b, t: (b * nt + t, 0)
    full2 = lambda b, t: (0, 0)
    full3 = lambda b, t: (0, 0, 0)
    return pl.pallas_call(
        _rglru_prompt_kernel,
        out_shape=(jax.ShapeDtypeStruct((batch * seq, c), BF16),
                   jax.ShapeDtypeStruct((batch, CONV_W - 1, c), F32),
                   jax.ShapeDtypeStruct((batch, 1, c), F32)),
        grid=(batch, nt),
        in_specs=[pl.BlockSpec((tt, c), row),
                  pl.BlockSpec((tt, c), lambda b, t: (b * nt + t, 1)),
                  pl.BlockSpec((CONV_W, c), full2),
                  pl.BlockSpec((1, c), full2),
                  pl.BlockSpec((N_RNN_BLOCKS, RNN_BLOCK, RNN_BLOCK), full3),
                  pl.BlockSpec((1, c), full2),
                  pl.BlockSpec((N_RNN_BLOCKS, RNN_BLOCK, RNN_BLOCK), full3),
                  pl.BlockSpec((1, c), full2),
                  pl.BlockSpec((1, c), full2)],
        out_specs=(pl.BlockSpec((tt, c), row),
                   pl.BlockSpec((1, CONV_W - 1, c), lambda b, t: (b, 0, 0)),
                   pl.BlockSpec((1, 1, c), lambda b, t: (b, 0, 0))),
        scratch_shapes=[pltpu.VMEM((tt + 8, c), F32), pltpu.VMEM((tt, c), F32),
                        pltpu.VMEM((tt, c), F32), pltpu.VMEM((tt, c), F32),
                        pltpu.VMEM((1, c), F32)],
        compiler_params=_cparams(2),
        name="rglru_prompt",
    )(z, z, conv_w, conv_b.reshape(1, c), lru_wa, lru_ba.reshape(1, c), lru_wi,
      lru_bi.reshape(1, c), lam.reshape(1, c))


def _rglru_step_kernel(xr_ref, gr_ref, buf_ref, h0_ref, cw_ref, cb_ref, wa_ref, ba_ref, wi_ref,
                       bi_ref, lam_ref, ya_ref, hn_ref, a_ref, u_ref):
    x = xr_ref[...]
    xc = cb_ref[...]
    for k in range(CONV_W - 1):
        xc = xc + cw_ref[k:k + 1, :] * buf_ref[k]
    xc = xc + cw_ref[CONV_W - 1:CONV_W, :] * x
    _lru_gates(xc, wa_ref, ba_ref, wi_ref, bi_ref, lam_ref, a_ref, u_ref)
    h = a_ref[...] * h0_ref[...] + u_ref[...]
    hn_ref[...] = h
    ya_ref[...] = (h * _gelu_tanh(gr_ref[...])).astype(ya_ref.dtype)


def _rglru_step(z, buf_t, h0, conv_w, conv_b, lru_wa, lru_ba, lru_wi, lru_bi, lam):
    b, c = h0.shape
    full2 = lambda i: (0, 0)
    full3 = lambda i: (0, 0, 0)
    return pl.pallas_call(
        _rglru_step_kernel,
        out_shape=(jax.ShapeDtypeStruct((b, c), BF16), jax.ShapeDtypeStruct((b, c), F32)),
        grid=(1,),
        in_specs=[pl.BlockSpec((b, c), full2),
                  pl.BlockSpec((b, c), lambda i: (0, 1)),
                  pl.BlockSpec((CONV_W - 1, b, c), full3),
                  pl.BlockSpec((b, c), full2),
                  pl.BlockSpec((CONV_W, c), full2),
                  pl.BlockSpec((1, c), full2),
                  pl.BlockSpec((N_RNN_BLOCKS, RNN_BLOCK, RNN_BLOCK), full3),
                  pl.BlockSpec((1, c), full2),
                  pl.BlockSpec((N_RNN_BLOCKS, RNN_BLOCK, RNN_BLOCK), full3),
                  pl.BlockSpec((1, c), full2),
                  pl.BlockSpec((1, c), full2)],
        out_specs=(pl.BlockSpec((b, c), full2), pl.BlockSpec((b, c), full2)),
        scratch_shapes=[pltpu.VMEM((b, c), F32), pltpu.VMEM((b, c), F32)],
        compiler_params=_cparams(1),
        name="rglru_step",
    )(z, z, buf_t, h0, conv_w, conv_b.reshape(1, c), lru_wa, lru_ba.reshape(1, c), lru_wi,
      lru_bi.reshape(1, c), lam.reshape(1, c))


def _bias_tiles_kernel(tab_ref, bd_ref, bo_ref):
    h = pl.program_id(0)
    t = bd_ref.shape[1]
    d0 = lax.broadcasted_iota(I32, (t, t), 1) - lax.broadcasted_iota(I32, (t, t), 0)
    bd_ref[0] = jnp.where(d0 >= 0, _bias_lookup(_bucket(d0), tab_ref, [h])[0] * LOG2E, NEG_INF)
    bo_ref[0] = _bias_lookup(_bucket(d0 + t), tab_ref, [h])[0] * LOG2E


def _bias_tiles(rel_bias, t):
    shp = jax.ShapeDtypeStruct((N_HEADS, t, t), F32)
    spec = pl.BlockSpec((1, t, t), lambda h: (h, 0, 0))
    return pl.pallas_call(
        _bias_tiles_kernel,
        out_shape=(shp, shp),
        grid=(N_HEADS,),
        in_specs=[pl.BlockSpec(memory_space=pltpu.SMEM)],
        out_specs=(spec, spec),
        compiler_params=_cparams(1),
        name="bias_tiles",
    )(rel_bias)


def _cmp_prompt_kernel(tab_ref, q_ref, kc_ref, vc_ref, wk_ref, wv_ref, gate_ref,
                       oc_ref, sel_ref, kcmp_ref, vcmp_ref, vcmp_t_ref):
    g = pl.program_id(1)
    qi = pl.program_id(2)
    tq = q_ref.shape[0]
    seq = kc_ref.shape[0]
    nblk = seq // CMP_BLOCK

    @pl.when(qi == 0)
    def _():
        k3 = kc_ref[...].reshape(nblk, CMP_BLOCK, HEAD_DIM)
        v3 = vc_ref[...].reshape(nblk, CMP_BLOCK, HEAD_DIM)
        kcmp_ref[...] = jnp.sum(k3 * wk_ref[0, 0][None], axis=1).astype(BF16)
        vcmp_ref[...] = jnp.zeros(vcmp_ref.shape, F32)
        vcmp_ref[0:nblk, :] = jnp.sum(v3 * wv_ref[0, 0][None], axis=1)
        vcmp_t_ref[...] = jnp.transpose(vcmp_ref[...]).astype(BF16)

    tpos = qi * tq + lax.broadcasted_iota(I32, (nblk, tq), 1)
    blk = lax.broadcasted_iota(I32, (nblk, tq), 0)
    dist = tpos - (blk * CMP_BLOCK + (CMP_BLOCK - 1))
    valid = dist >= 0
    biases = _bias_lookup(_bucket(dist), tab_ref, [g * HPG + h for h in range(HPG)])
    kb = kcmp_ref[...]
    vt = vcmp_t_ref[:, 0:nblk]
    gates = _sigmoid(gate_ref[0, 0])
    heads = range(HPG)
    cols = [slice(h * HEAD_DIM, (h + 1) * HEAD_DIM) for h in heads]
    scores = [_dot_nt(kb, q_ref[:, cols[h]].astype(BF16)) for h in heads]
    scores = [jnp.where(valid, s * SCALE + b, NEG_INF) for s, b in zip(scores, biases)]
    e = [jnp.where(valid, jnp.exp(s - jnp.max(s, axis=0, keepdims=True)), 0.0) for s in scores]
    probs = [x / jnp.maximum(jnp.sum(x, axis=0, keepdims=True), 1e-30) for x in e]
    imp = probs[0]
    for h in range(1, HPG):
        imp = imp + probs[h]
    outs = [_dot(vt, p.astype(BF16)) for p in probs]
    for h in heads:
        oc_ref[:, cols[h]] = jnp.transpose(outs[h] * gates[h:h + 1, :])

    cur = jnp.right_shift(tpos, 6)
    score = _block_scores(imp, blk, cur)
    rank = jnp.zeros((nblk, tq), F32)
    for j in range(nblk):
        rank = rank + _ahead(score[j:j + 1, :], score, blk > j)
    sel_ref[0, 0] = jnp.where(rank < float(min(N_SEL, nblk)), 0.0, NEG_INF)


def _cmp_prompt(z, rel_bias, wexp, gn_t, batch, seq):
    tq = ATT_T
    nq = seq // tq
    nblk = seq // CMP_BLOCK
    m = batch * seq
    qcol = OFF_Q // (HPG * HEAD_DIM)
    kcol = OFF_KVC // HEAD_DIM
    return pl.pallas_call(
        _cmp_prompt_kernel,
        out_shape=(jax.ShapeDtypeStruct((m, Q_W), F32),
                   jax.ShapeDtypeStruct((batch, N_KV, nblk, seq), F32)),
        grid=(batch, N_KV, nq),
        in_specs=[pl.BlockSpec(memory_space=pltpu.SMEM),
                  pl.BlockSpec((tq, HPG * HEAD_DIM), lambda b, g, i: (b * nq + i, qcol + g)),
                  pl.BlockSpec((seq, HEAD_DIM), lambda b, g, i: (b, kcol + g)),
                  pl.BlockSpec((seq, HEAD_DIM), lambda b, g, i: (b, kcol + N_KV + g)),
                  pl.BlockSpec((1, 1, CMP_BLOCK, HEAD_DIM), lambda b, g, i: (0, g, 0, 0)),
                  pl.BlockSpec((1, 1, CMP_BLOCK, HEAD_DIM), lambda b, g, i: (1, g, 0, 0)),
                  pl.BlockSpec((1, 1, HPG, tq), lambda b, g, i: (0, g, 0, b * nq + i))],
        out_specs=(pl.BlockSpec((tq, HPG * HEAD_DIM), lambda b, g, i: (b * nq + i, g)),
                   pl.BlockSpec((1, 1, nblk, tq), lambda b, g, i: (b, g, 0, i))),
        scratch_shapes=[pltpu.VMEM((nblk, HEAD_DIM), BF16), pltpu.VMEM((HEAD_DIM, HEAD_DIM), F32),
                        pltpu.VMEM((HEAD_DIM, HEAD_DIM), BF16)],
        compiler_params=_cparams(3),
        name="cmp_prompt",
    )(rel_bias, z, z, z, wexp, wexp, gn_t)


def _attn_init(m_ref, l_ref, acc_ref):
    m_ref[...] = jnp.full(m_ref.shape, NEG_INF, F32)
    l_ref[...] = jnp.zeros(l_ref.shape, F32)
    acc_ref[...] = jnp.zeros(acc_ref.shape, F32)


def _attn_update(h, qh, k, v, bias, valid, m_ref, l_ref, acc_ref):
    s = _dot_nt(qh, k) * SCALE + bias
    if valid is not None:
        s = jnp.where(valid, s, NEG_INF)
    m_old = m_ref[h]
    m_new = jnp.maximum(m_old, jnp.max(s, axis=-1, keepdims=True))
    alpha = jnp.exp(m_old - m_new)
    e = jnp.exp(s - m_new)
    if valid is not None:
        e = jnp.where(valid, e, 0.0)
    l_ref[h] = alpha * l_ref[h] + jnp.sum(e, axis=-1, keepdims=True)
    acc_ref[h] = alpha * acc_ref[h] + _dot(e.astype(BF16), v)
    m_ref[h] = m_new


def _attn_prologue(qi, q_ref, k_ref, v_ref, kb_ref, vt_ref, qb_ref, m_ref, acc_ref):
    n_tiles, t = kb_ref.shape[0], kb_ref.shape[1]

    @pl.when(qi == 0)
    def _():
        for kj in range(n_tiles):
            rows = slice(kj * t, (kj + 1) * t)
            kb_ref[kj] = k_ref[rows, :].astype(BF16)
            vt_ref[kj, 0:HEAD_DIM] = jnp.transpose(v_ref[rows, :]).astype(BF16)
            vt_ref[kj, HEAD_DIM:] = jnp.ones((vt_ref.shape[1] - HEAD_DIM, t), BF16)

    for h in range(HPG):
        qb_ref[h] = q_ref[:, h * HEAD_DIM:(h + 1) * HEAD_DIM].astype(BF16)
    m_ref[...] = jnp.full(m_ref.shape, NEG_INF, F32)
    acc_ref[...] = jnp.zeros(acc_ref.shape, F32)


def _attn_update_t(tiles, qb_ref, m_ref, acc_ref):
    heads = range(HPG)
    scores = [[_dot_nt(k, qb_ref[h]) for h in heads] for k, _, _, _ in tiles]
    scores = [[s * (SCALE * LOG2E) + b for s, b in zip(ss, tile[2])] for ss, tile in zip(scores, tiles)]
    scores = [ss if tile[3] is None else [jnp.where(tile[3], s, NEG_INF) for s in ss]
              for ss, tile in zip(scores, tiles)]
    m_old = [m_ref[h] for h in heads]
    m_new = list(m_old)
    for ss in scores:
        m_new = [jnp.maximum(mn, jnp.max(s, axis=0, keepdims=True)) for mn, s in zip(m_new, ss)]
    alpha = [jnp.exp2(mo - mn) for mo, mn in zip(m_old, m_new)]
    e = [[jnp.exp2(s - mn).astype(BF16) for s, mn in zip(ss, m_new)] for ss in scores]
    for h in heads:
        m_ref[h] = m_new[h]
    pv = [[_dot(tile[1], es[h]) for h in heads] for es, tile in zip(e, tiles)]
    for h in heads:
        total = pv[0][h]
        for t in range(1, len(tiles)):
            total = total + pv[t][h]
        acc_ref[h] = alpha[h] * acc_ref[h] + total


def _attn_output_t(h, gates, acc_ref):
    denom = jnp.maximum(acc_ref[h, HEAD_DIM:HEAD_DIM + 1, :], 1e-30)
    o_t = acc_ref[h, 0:HEAD_DIM, :] / denom * gates[h:h + 1, :]
    return jnp.transpose(o_t)


def _sel_prompt_kernel(tab_ref, q_ref, k_ref, v_ref, bd_ref, bo_ref, gate_ref, sel_ref,
                       os_ref, kb_ref, vt_ref, qb_ref, m_ref, acc_ref):
    g = pl.program_id(1)
    qi = pl.program_id(2)
    t = q_ref.shape[0]
    per_tile = t // CMP_BLOCK
    _attn_prologue(qi, q_ref, k_ref, v_ref, kb_ref, vt_ref, qb_ref, m_ref, acc_ref)

    def tile(kj, kind):
        rows = [jnp.broadcast_to(sel_ref[0, 0, pl.ds(kj * per_tile + i, 1), :], (CMP_BLOCK, t))
                for i in range(per_tile)]
        mask = jnp.concatenate(rows, axis=0)
        if kind == "far":
            biases = [mask + tab_ref[N_BUCKETS - 1, g * HPG + h] * LOG2E for h in range(HPG)]
        elif kind == "off":
            biases = [mask + bo_ref[h] for h in range(HPG)]
        else:
            biases = [mask + bd_ref[h] for h in range(HPG)]
        return (kb_ref[kj], vt_ref[kj], biases, None)

    def update(*tiles):
        _attn_update_t(tiles, qb_ref, m_ref, acc_ref)

    n_far = jnp.maximum(qi - 1, 0)

    def pair_body(p, carry):
        update(tile(2 * p, "far"), tile(2 * p + 1, "far"))
        return carry

    lax.fori_loop(0, n_far // 2, pair_body, 0)

    @pl.when(qi == 0)
    def _():
        update(tile(qi, "diag"))

    @pl.when(jnp.logical_and(qi >= 1, n_far % 2 == 0))
    def _():
        update(tile(qi - 1, "off"), tile(qi, "diag"))

    @pl.when(n_far % 2 == 1)
    def _():
        update(tile(qi - 2, "far"), tile(qi - 1, "off"), tile(qi, "diag"))

    gates = _sigmoid(gate_ref[0, 0])
    for h in range(HPG):
        os_ref[:, h * HEAD_DIM:(h + 1) * HEAD_DIM] = _attn_output_t(h, gates, acc_ref)


def _win_prompt_kernel(tab_ref, q_ref, k_ref, v_ref, bd_ref, bo_ref, gate_ref, oc_ref, os_ref,
                       yb_ref, kb_ref, vt_ref, qb_ref, m_ref, acc_ref):
    g = pl.program_id(1)
    qi = pl.program_id(2)
    t = q_ref.shape[0]
    _attn_prologue(qi, q_ref, k_ref, v_ref, kb_ref, vt_ref, qb_ref, m_ref, acc_ref)
    key = lax.broadcasted_iota(I32, (t, t), 0)
    qry = lax.broadcasted_iota(I32, (t, t), 1)

    def tile(kj, kind):
        if kind == "far":
            biases = [tab_ref[N_BUCKETS - 1, g * HPG + h] * LOG2E for h in range(HPG)]
            valid = qry < key
        elif kind == "off":
            biases, valid = [bo_ref[h] for h in range(HPG)], None
        else:
            biases, valid = [bd_ref[h] for h in range(HPG)], None
        return (kb_ref[kj], vt_ref[kj], biases, valid)

    def update(*tiles):
        _attn_update_t(tiles, qb_ref, m_ref, acc_ref)

    @pl.when(qi == 0)
    def _():
        update(tile(qi, "diag"))

    @pl.when(qi == 1)
    def _():
        update(tile(qi, "diag"), tile(qi - 1, "off"))

    @pl.when(qi >= 2)
    def _():
        update(tile(qi, "diag"), tile(qi - 1, "off"), tile(qi - 2, "far"))

    gates = _sigmoid(gate_ref[0, 0])
    for h in range(HPG):
        sl = slice(h * HEAD_DIM, (h + 1) * HEAD_DIM)
        o = _attn_output_t(h, gates, acc_ref)
        yb_ref[:, sl] = (oc_ref[:, sl] + os_ref[:, sl] + o).astype(yb_ref.dtype)


def _attn_prompt_specs(batch, seq, kv_off, gate_idx):
    t = ATT_T
    nq = seq // t
    qcol = OFF_Q // (HPG * HEAD_DIM)
    kcol = kv_off // HEAD_DIM
    rowblk = lambda b, g, i: (b * nq + i, g)
    specs = [pl.BlockSpec(memory_space=pltpu.SMEM),
             pl.BlockSpec((t, HPG * HEAD_DIM), lambda b, g, i: (b * nq + i, qcol + g)),
             pl.BlockSpec((seq, HEAD_DIM), lambda b, g, i: (b, kcol + g)),
             pl.BlockSpec((seq, HEAD_DIM), lambda b, g, i: (b, kcol + N_KV + g)),
             pl.BlockSpec((HPG, t, t), lambda b, g, i: (g, 0, 0)),
             pl.BlockSpec((HPG, t, t), lambda b, g, i: (g, 0, 0)),
             pl.BlockSpec((1, 1, HPG, t), lambda b, g, i: (gate_idx, g, 0, b * nq + i))]
    scratch = [pltpu.VMEM((seq // t, t, HEAD_DIM), BF16),
               pltpu.VMEM((seq // t, HEAD_DIM + DENOM_ROWS, t), BF16),
               pltpu.VMEM((HPG, t, HEAD_DIM), BF16),
               pltpu.VMEM((HPG, 1, t), F32),
               pltpu.VMEM((HPG, HEAD_DIM + DENOM_ROWS, t), F32)]
    return specs, scratch, pl.BlockSpec((t, HPG * HEAD_DIM), rowblk), (batch, N_KV, nq)


def _sel_prompt(z, rel_bias, bd, bo, gn_t, sel, batch, seq):
    specs, scratch, out_spec, grid = _attn_prompt_specs(batch, seq, OFF_KVS, 1)
    t = ATT_T
    nblk = seq // CMP_BLOCK
    specs += [pl.BlockSpec((1, 1, nblk, t), lambda b, g, i: (b, g, 0, i))]
    return pl.pallas_call(
        _sel_prompt_kernel,
        out_shape=jax.ShapeDtypeStruct((batch * seq, Q_W), F32),
        grid=grid, in_specs=specs, out_specs=out_spec, scratch_shapes=scratch,
        compiler_params=_cparams(3), name="sel_prompt",
    )(rel_bias, z, z, z, bd, bo, gn_t, sel)


def _win_prompt(z, rel_bias, bd, bo, gn_t, oc, osel, batch, seq):
    specs, scratch, out_spec, grid = _attn_prompt_specs(batch, seq, OFF_KVW, 2)
    specs += [out_spec, out_spec]
    return pl.pallas_call(
        _win_prompt_kernel,
        out_shape=jax.ShapeDtypeStruct((batch * seq, Q_W), BF16),
        grid=grid, in_specs=specs, out_specs=out_spec, scratch_shapes=scratch,
        compiler_params=_cparams(3), name="win_prompt",
    )(rel_bias, z, z, z, bd, bo, gn_t, oc, osel)


def _cmp_sample_kernel(tab_ref, q_ref, kvc_ref, new_ref, w_ref, gate_ref,
                       oc_ref, idx_ref, kall_ref, vall_ref, imp_ref, *, q_pos):
    nblk = kvc_ref.shape[1]
    nk = kall_ref.shape[0]
    blk = lax.broadcasted_iota(I32, (1, nk), 1)
    dist = q_pos - (blk * CMP_BLOCK + (CMP_BLOCK - 1))
    valid = dist >= 0
    bkt = _bucket(dist)
    new_c = new_ref[0] * w_ref[0]
    kall_ref[...] = jnp.zeros(kall_ref.shape, F32)
    vall_ref[...] = jnp.zeros(vall_ref.shape, F32)
    for g in range(N_KV):
        kall_ref[0:nblk, :] = kvc_ref[0, :, g, :]
        vall_ref[0:nblk, :] = kvc_ref[0, :, N_KV + g, :]
        kall_ref[nblk:nblk + 1, :] = new_c[g:g + 1, :]
        vall_ref[nblk:nblk + 1, :] = new_c[N_KV + g:N_KV + g + 1, :]
        kb = kall_ref[...].astype(BF16)
        vb = vall_ref[...].astype(BF16)
        biases = _bias_lookup(bkt, tab_ref, [g * HPG + h for h in range(HPG)])
        rows = lax.broadcasted_iota(I32, (HROWS, nk), 0)
        bias = jnp.zeros((HROWS, nk), F32)
        for h in range(HPG):
            bias = jnp.where(rows == h, biases[h], bias)
        s = _dot_nt(q_ref[0, g].astype(BF16), kb) * SCALE + bias
        p = _masked_softmax(s, valid)
        gate = _sigmoid(gate_ref[0, 0, g])
        oc_ref[0, g] = gate * _dot(p.astype(BF16), vb)
        head_row = jnp.where(rows < HPG, p, 0.0)
        imp_ref[g:g + 1, :] = jnp.sum(head_row, axis=0, keepdims=True)

    imp = imp_ref[0:N_KV, :]
    blk4 = lax.broadcasted_iota(I32, (N_KV, nk), 1)
    cur = q_pos // CMP_BLOCK
    score = _block_scores(imp, blk4, cur)
    rank = jnp.zeros((N_KV, nk), F32)
    for j in range(nblk + 1):
        col = score[:, j:j + 1]
        rank = rank + _ahead(col, score, blk4 > j)
    rank = jnp.where(blk4 <= nblk, rank, float(nk))
    blkf = blk4.astype(F32)
    slot = lax.broadcasted_iota(I32, (N_KV, N_SEL), 1)
    idx = jnp.zeros((N_KV, N_SEL), F32)
    for r in range(N_SEL):
        pick = jnp.sum(jnp.where(rank == float(r), blkf, 0.0), axis=-1, keepdims=True)
        idx = jnp.where(slot == r, pick, idx)
    idx_ref[0] = idx.astype(I32)


def _cmp_sample(rel_bias, q8, kvc, kvc_new, w8, gates8, q_pos):
    b, nblk = kvc.shape[:2]
    nk = 2 * HEAD_DIM
    assert nblk + 1 <= nk and q_pos // CMP_BLOCK == nblk
    return pl.pallas_call(
        functools.partial(_cmp_sample_kernel, q_pos=q_pos),
        out_shape=(jax.ShapeDtypeStruct((b, N_KV, HROWS, HEAD_DIM), F32),
                   jax.ShapeDtypeStruct((b, N_KV, N_SEL), I32)),
        grid=(b,),
        in_specs=[pl.BlockSpec(memory_space=pltpu.SMEM),
                  pl.BlockSpec((1, N_KV, HROWS, HEAD_DIM), lambda i: (i, 0, 0, 0)),
                  pl.BlockSpec((1, nblk, KV_ROWS, HEAD_DIM), lambda i: (i, 0, 0, 0)),
                  pl.BlockSpec((1, KV_ROWS, HEAD_DIM), lambda i: (i, 0, 0)),
                  pl.BlockSpec((CMP_BLOCK, KV_ROWS, HEAD_DIM), lambda i: (0, 0, 0)),
                  pl.BlockSpec((1, 1, N_KV, HROWS, 1), lambda i: (i, 0, 0, 0, 0))],
        out_specs=(pl.BlockSpec((1, N_KV, HROWS, HEAD_DIM), lambda i: (i, 0, 0, 0)),
                   pl.BlockSpec((1, N_KV, N_SEL), lambda i: (i, 0, 0))),
        scratch_shapes=[pltpu.VMEM((nk, HEAD_DIM), F32), pltpu.VMEM((nk, HEAD_DIM), F32),
                        pltpu.VMEM((HROWS, nk), F32)],
        compiler_params=_cparams(1),
        name="cmp_sample",
    )(rel_bias, q8, kvc, kvc_new, w8, gates8)


def _sel_bias_kernel(tab_ref, o_ref, *, q_pos):
    g = pl.program_id(0)
    shape = o_ref.shape[1:]
    blk = lax.broadcasted_iota(I32, shape, 0)
    rows = lax.broadcasted_iota(I32, shape, 1)
    offs = lax.broadcasted_iota(I32, shape, 2)
    dist = q_pos - (blk * CMP_BLOCK + offs)
    biases = _bias_lookup(_bucket(dist), tab_ref, [g * HPG + h for h in range(HPG)])
    bias = jnp.zeros(shape, F32)
    for h in range(HPG):
        bias = jnp.where(rows == h, biases[h], bias)
    o_ref[0] = jnp.where(dist >= 0, bias, NEG_INF)


def _sel_bias(rel_bias, q_pos, n_blocks):
    return pl.pallas_call(
        functools.partial(_sel_bias_kernel, q_pos=q_pos),
        out_shape=jax.ShapeDtypeStruct((N_KV, n_blocks, HROWS, CMP_BLOCK), F32),
        grid=(N_KV,),
        in_specs=[pl.BlockSpec(memory_space=pltpu.SMEM)],
        out_specs=pl.BlockSpec((1, n_blocks, HROWS, CMP_BLOCK), lambda g: (g, 0, 0, 0)),
        compiler_params=_cparams(1),
        name="sel_bias",
    )(rel_bias)


def _sel_sample_kernel(pt_ref, idx_ref, bias_ref, q_ref, *refs, n_past):
    nb = SEL_BLOCKS_PER_STEP
    blk_refs = refs[:N_KV * nb]
    new_ref, gate_ref, os_ref, kcat_ref, vcat_ref, bcat_ref, m_ref, l_ref, acc_ref = refs[N_KV * nb:]
    b = pl.program_id(0)
    step = pl.program_id(1)

    @pl.when(step == 0)
    def _():
        _attn_init(m_ref, l_ref, acc_ref)

    first = lax.broadcasted_iota(I32, (CMP_BLOCK, HEAD_DIM), 0) == 0
    for g in range(N_KV):
        kn = jnp.where(first, new_ref[0][g:g + 1, :], 0.0)
        vn = jnp.where(first, new_ref[0][N_KV + g:N_KV + g + 1, :], 0.0)
        for j in range(nb):
            n = idx_ref[b, g, step * nb + j]
            is_new = n >= n_past
            blk_ref = blk_refs[g * nb + j]
            keys = slice(j * CMP_BLOCK, (j + 1) * CMP_BLOCK)
            kcat_ref[keys, :] = jnp.where(is_new, kn, blk_ref[0, :, g, :]).astype(BF16)
            vcat_ref[keys, :] = jnp.where(is_new, vn, blk_ref[0, :, N_KV + g, :]).astype(BF16)
            bcat_ref[:, keys] = bias_ref[g, n]
        _attn_update(g, q_ref[0, g].astype(BF16), kcat_ref[...], vcat_ref[...], bcat_ref[...], None,
                     m_ref, l_ref, acc_ref)

    @pl.when(step == pl.num_programs(1) - 1)
    def _():
        for g in range(N_KV):
            o = acc_ref[g] / jnp.maximum(l_ref[g], 1e-30)
            os_ref[0, g] = _sigmoid(gate_ref[0, 0, g]) * o


def _sel_sample(page_table, idx, rel_bias, q8, cache_half, kvs_new, gates8, q_pos):
    b = q8.shape[0]
    nb = SEL_BLOCKS_PER_STEP
    n_past = q_pos // CMP_BLOCK
    per_page = PAGE_SIZE // CMP_BLOCK
    bias = _sel_bias(rel_bias, q_pos, n_past + 1)

    def blk_spec(g, j):
        def index(i, s, pt, ix):
            n = jnp.minimum(ix[i, g, s * nb + j], n_past - 1)
            return (pt[i, n // per_page] * per_page + n % per_page, 0, 0, 0)
        return pl.BlockSpec((1, CMP_BLOCK, KV_ROWS, HEAD_DIM), index)

    hspec = pl.BlockSpec((1, N_KV, HROWS, HEAD_DIM), lambda i, s, pt, ix: (i, 0, 0, 0))
    grid_spec = pltpu.PrefetchScalarGridSpec(
        num_scalar_prefetch=2,
        grid=(b, N_SEL // nb),
        in_specs=[pl.BlockSpec(bias.shape, lambda i, s, pt, ix: (0, 0, 0, 0)), hspec]
        + [blk_spec(g, j) for g in range(N_KV) for j in range(nb)]
        + [pl.BlockSpec((1, KV_ROWS, HEAD_DIM), lambda i, s, pt, ix: (i, 0, 0)),
           pl.BlockSpec((1, 1, N_KV, HROWS, 1), lambda i, s, pt, ix: (i, 1, 0, 0, 0))],
        out_specs=hspec,
        scratch_shapes=[pltpu.VMEM((nb * CMP_BLOCK, HEAD_DIM), BF16),
                        pltpu.VMEM((nb * CMP_BLOCK, HEAD_DIM), BF16),
                        pltpu.VMEM((HROWS, nb * CMP_BLOCK), F32),
                        pltpu.VMEM((N_KV, HROWS, 1), F32), pltpu.VMEM((N_KV, HROWS, 1), F32),
                        pltpu.VMEM((N_KV, HROWS, HEAD_DIM), F32)],
    )
    return pl.pallas_call(
        functools.partial(_sel_sample_kernel, n_past=n_past),
        out_shape=jax.ShapeDtypeStruct((b, N_KV, HROWS, HEAD_DIM), F32),
        grid_spec=grid_spec,
        compiler_params=_cparams(2),
        name="sel_sample",
    )(page_table, idx, bias, q8, *([cache_half] * (N_KV * nb)), kvs_new, gates8)


def _win_sample_kernel(tab_ref, q_ref, win_ref, new_ref, gate_ref, oc_ref, os_ref,
                       yb_ref, nwin_ref, kall_ref, vall_ref):
    wc = win_ref.shape[1]
    nk = kall_ref.shape[0]
    nwin_ref[0, 0:wc - 1] = win_ref[0, 1:wc]
    nwin_ref[0, wc - 1] = new_ref[0]
    kidx = lax.broadcasted_iota(I32, (1, nk), 1)
    dist = wc - kidx
    valid = jnp.where(dist >= 0, dist, WINDOW) < WINDOW
    bkt = _bucket(dist)
    kall_ref[...] = jnp.zeros(kall_ref.shape, F32)
    vall_ref[...] = jnp.zeros(vall_ref.shape, F32)
    for g in range(N_KV):
        kall_ref[0:wc, :] = win_ref[0, :, g, :]
        vall_ref[0:wc, :] = win_ref[0, :, N_KV + g, :]
        kall_ref[wc:wc + 1, :] = new_ref[0][g:g + 1, :]
        vall_ref[wc:wc + 1, :] = new_ref[0][N_KV + g:N_KV + g + 1, :]
        biases = _bias_lookup(bkt, tab_ref, [g * HPG + h for h in range(HPG)])
        rows = lax.broadcasted_iota(I32, (HROWS, nk), 0)
        bias = jnp.zeros((HROWS, nk), F32)
        for h in range(HPG):
            bias = jnp.where(rows == h, biases[h], bias)
        s = _dot_nt(q_ref[0, g].astype(BF16), kall_ref[...].astype(BF16)) * SCALE + bias
        p = _masked_softmax(s, valid)
        o = _dot(p.astype(BF16), vall_ref[...].astype(BF16))
        yb_ref[0, g] = oc_ref[0, g] + os_ref[0, g] + _sigmoid(gate_ref[0, 0, g]) * o


def _win_sample(rel_bias, q8, win, kvw_new, gates8, oc, osel):
    b, wc = win.shape[:2]
    nk = wc + HEAD_DIM
    hspec = pl.BlockSpec((1, N_KV, HROWS, HEAD_DIM), lambda i: (i, 0, 0, 0))
    wspec = pl.BlockSpec((1, wc, KV_ROWS, HEAD_DIM), lambda i: (i, 0, 0, 0))
    return pl.pallas_call(
        _win_sample_kernel,
        out_shape=(jax.ShapeDtypeStruct((b, N_KV, HROWS, HEAD_DIM), F32),
                   jax.ShapeDtypeStruct(win.shape, F32)),
        grid=(b,),
        in_specs=[pl.BlockSpec(memory_space=pltpu.SMEM),
                  hspec,
                  wspec,
                  pl.BlockSpec((1, KV_ROWS, HEAD_DIM), lambda i: (i, 0, 0)),
                  pl.BlockSpec((1, 1, N_KV, HROWS, 1), lambda i: (i, 2, 0, 0, 0)),
                  hspec, hspec],
        out_specs=(hspec, wspec),
        scratch_shapes=[pltpu.VMEM((nk, HEAD_DIM), F32), pltpu.VMEM((nk, HEAD_DIM), F32)],
        compiler_params=_cparams(1),
        name="win_sample",
    )(rel_bias, q8, win, kvw_new, gates8, oc, osel)


def _out_and_ffn(x, ya, yb, z, x_s, ya_s, yb_s, z_s, w_proj_a, w_proj_b, w_out, norm_ffn,
                 w_gate, w_up, w_down, norm_final, *, tm, tm_norm):
    bs = x_s.shape[0]
    half = w_gate.shape[1] // 2
    mix, mix_s = _mix(ya, yb, w_proj_a, w_proj_b, z, ya_s, yb_s, z_s, tm=tm, tn=512)
    h, h_s = _mm_res(mix, w_out, x, mix_s, x_s, tm=tm, tn=512, name="out_proj")
    hn = _rmsnorm(h, norm_ffn, BF16, tm_norm)
    hn_s = _rmsnorm(h_s, norm_ffn, BF16, bs)
    ff, ff_s = _swiglu(hn, w_gate, w_up, hn_s, tm=tm, tn=256)
    y, y_s = _mm_res(ff, w_down, h, ff_s, h_s, tm=tm, tn=256, k_block=0, tk=half, name="ffn_down0")
    y, y_s = _mm_res(ff, w_down, y, ff_s, y_s, tm=tm, tn=256, k_block=1, tk=half, name="ffn_down1")
    return _rmsnorm(y, norm_final, F32, tm_norm), _rmsnorm(y_s, norm_final, F32, bs)


def kernel(x_prompt, x_sample, cache_cmp_kv, cache_sel_kv, cache_win_kv, state_rglru_h, state_conv,
           page_table, rel_bias, norm_mix, w_in, conv_w, conv_b, lru_wa, lru_ba, lru_wi, lru_bi,
           lru_lambda, nsa_w_cmp, w_proj_a, w_proj_b, w_out, norm_ffn, w_gate, w_up, w_down,
           norm_final):
    depth = w_in.shape[0]
    assert depth == 1, "single-layer trunk"
    bp, tp, _ = x_prompt.shape
    bs, ts, _ = x_sample.shape
    assert ts == 1
    n_pool = cache_cmp_kv.shape[1]
    past_len = page_table.shape[1] * PAGE_SIZE
    wc = cache_win_kv.shape[2]

    w_in0 = jnp.swapaxes(w_in[0], 0, 1)
    wexp = jnp.broadcast_to(nsa_w_cmp[0][..., None], (2, N_KV, CMP_BLOCK, HEAD_DIM))
    w8 = jnp.broadcast_to(nsa_w_cmp[0].reshape(KV_ROWS, CMP_BLOCK).T[..., None],
                          (CMP_BLOCK, KV_ROWS, HEAD_DIM))
    layer_w = (conv_w[0], conv_b[0], lru_wa[0], lru_ba[0], lru_wi[0], lru_bi[0], lru_lambda[0])
    tail_w = (w_proj_a[0], w_proj_b[0], w_out[0], norm_ffn[0], w_gate[0], w_up[0], w_down[0],
              norm_final)

    mp = bp * tp
    xp = x_prompt.reshape(mp, D_MODEL)
    xs = x_sample.reshape(bs, D_MODEL)
    per_page = PAGE_SIZE // CMP_BLOCK
    cache_c = cache_cmp_kv.reshape(n_pool, PAGE_SIZE, KV_ROWS, HEAD_DIM)
    xn = _rmsnorm(xp, norm_mix[0], BF16, 256)
    xn_s = _rmsnorm(xs, norm_mix[0], BF16, bs)
    z, zs, kvc = _in_proj_all(xn, w_in0, xn_s, (cache_c, page_table, w8), tm=1024, tn=IN_TN)

    ya, conv_p, h_p = _rglru_prompt(z, bp, tp, *layer_w, tt=128)
    gn_t = z[:, Z_GN:Z_GN + 3 * N_HEADS].reshape(mp, 3, N_KV, HPG).transpose(1, 2, 3, 0)
    bd, bo = _bias_tiles(rel_bias, ATT_T)
    oc, sel = _cmp_prompt(z, rel_bias, wexp, gn_t, bp, tp)
    osel = _sel_prompt(z, rel_bias, bd, bo, gn_t, sel, bp, tp)
    yb = _win_prompt(z, rel_bias, bd, bo, gn_t, oc, osel, bp, tp)

    kv_shape = (1, bp, tp // PAGE_SIZE, PAGE_SIZE, 2, N_KV, HEAD_DIM)
    cmp_p = z[:, OFF_KVC:OFF_KVS].reshape(kv_shape)
    sel_p = z[:, OFF_KVS:OFF_KVW].reshape(kv_shape)
    wlen = min(WINDOW, tp)
    win_p = z[:, OFF_KVW:MAIN_W].reshape(bp, tp, 2, N_KV, HEAD_DIM)[None, :, tp - wlen:]

    buf_t = state_conv[0].transpose(1, 0, 2)
    ya_s, h_s = _rglru_step(zs, buf_t, state_rglru_h[0], *layer_w)

    q8 = jnp.pad(zs[:, OFF_Q:OFF_KVC].reshape(bs, N_KV, HPG, HEAD_DIM),
                 ((0, 0), (0, 0), (0, HROWS - HPG), (0, 0)))
    gates8 = jnp.pad(zs[:, Z_GN:Z_GN + 3 * N_HEADS].reshape(bs, 3, N_KV, HPG),
                     ((0, 0), (0, 0), (0, 0), (0, HROWS - HPG)))[..., None]
    kvc_new = zs[:, OFF_KVC:OFF_KVS].reshape(bs, KV_ROWS, HEAD_DIM)
    kvs_new = zs[:, OFF_KVS:OFF_KVW].reshape(bs, KV_ROWS, HEAD_DIM)
    kvw_new = zs[:, OFF_KVW:MAIN_W].reshape(bs, KV_ROWS, HEAD_DIM)
    cache_s = cache_sel_kv.reshape(n_pool * per_page, CMP_BLOCK, KV_ROWS, HEAD_DIM)
    win = cache_win_kv.reshape(bs, wc, KV_ROWS, HEAD_DIM)

    oc_s, idx = _cmp_sample(rel_bias, q8, kvc, kvc_new, w8, gates8, past_len)
    os_s = _sel_sample(page_table, idx, rel_bias, q8, cache_s, kvs_new, gates8, past_len)
    yb8, win_next = _win_sample(rel_bias, q8, win, kvw_new, gates8, oc_s, os_s)
    yb_s = yb8[:, :, :HPG].reshape(bs, Q_W).astype(BF16)

    y_prompt, y_sample = _out_and_ffn(xp, ya, yb, z, xs, ya_s, yb_s, zs, *tail_w, tm=1024, tm_norm=256)

    row_shape = (1, bs, 1, 2, N_KV, HEAD_DIM)
    cmp_s = kvc_new.reshape(row_shape)
    sel_s = kvs_new.reshape(row_shape)
    win_s = win_next.reshape(1, bs, wc, 2, N_KV, HEAD_DIM)
    conv_s = jnp.concatenate([state_conv[0], zs[:, None, OFF_XR:OFF_GR]], axis=1)[:, 1:][None]

    return (y_prompt.reshape(bp, tp, D_MODEL), y_sample.reshape(bs, ts, D_MODEL),
            cmp_p, sel_p, win_p, h_p.reshape(1, bp, D_RNN), conv_p[None],
            cmp_s, sel_s, win_s, h_s[None], conv_s)
```

```python
import functools
import math

import jax
import jax.numpy as jnp
from jax import lax
from jax.experimental import pallas as pl
from jax.experimental.pallas import tpu as pltpu

F32 = jnp.float32
BF16 = jnp.bfloat16
I32 = jnp.int32

D_MODEL = 4096
D_RNN = 2048
N_RNN_BLOCKS = 16
RNN_BLOCK = D_RNN // N_RNN_BLOCKS
CONV_W = 4
LRU_C = 8.0
N_HEADS = 16
HEAD_DIM = 128
N_KV = 4
HPG = N_HEADS // N_KV
CMP_BLOCK = 64
N_SEL = 16
WINDOW = 512
N_BUCKETS = 32
MAX_DISTANCE = 128
PAGE_SIZE = 128
Q_W = N_HEADS * HEAD_DIM
KV_W = 2 * N_KV * HEAD_DIM
EPS = 1e-6
NEG_INF = -1e30
FORCE = 1e9
SCALE = HEAD_DIM ** -0.5
LOG2E = math.log2(math.e)
DENOM_ROWS = 16

OFF_XR = 0
OFF_GR = D_RNN
OFF_Q = 2 * D_RNN
OFF_KVC = OFF_Q + Q_W
OFF_KVS = OFF_KVC + KV_W
OFF_KVW = OFF_KVS + KV_W
OFF_GN = OFF_KVW + KV_W
OFF_GA = OFF_GN + 3 * N_HEADS
MAIN_W = OFF_GN
IN_TN = 512
Z_GN = MAIN_W
Z_GA = MAIN_W + IN_TN
Z_GB = Z_GA + D_MODEL

ATT_T = 256
HROWS = 8
KV_ROWS = 2 * N_KV
PAGES_PER_STEP = 8
SEL_BLOCKS_PER_STEP = 4
VMEM_LIMIT = 56 * 1024 * 1024


def _cparams(n_axes):
    return pltpu.CompilerParams(dimension_semantics=("arbitrary",) * n_axes,
                                vmem_limit_bytes=VMEM_LIMIT)


def _sigmoid(x):
    return 1.0 / (1.0 + jnp.exp(-x))


def _gelu_tanh(x):
    return 0.5 * x * (1.0 + jnp.tanh(math.sqrt(2.0 / math.pi) * (x + 0.044715 * (x * x * x))))


def _dot(a, b):
    return jnp.dot(a, b, preferred_element_type=F32)


def _dot_nt(a, b):
    return lax.dot_general(a, b, (((1,), (1,)), ((), ())), preferred_element_type=F32)


def _bucket(dist):
    n = jnp.maximum(dist, 0)
    exact = N_BUCKETS // 2
    nf = jnp.maximum(n, 1).astype(F32)
    large = exact + (jnp.log(nf / exact) / math.log(MAX_DISTANCE / exact)
                     * (N_BUCKETS - exact)).astype(I32)
    return jnp.where(n < exact, n, jnp.minimum(large, N_BUCKETS - 1))


def _bias_lookup(bkt, tab_ref, cols):
    outs = [jnp.zeros(bkt.shape, F32) for _ in cols]
    for b in range(N_BUCKETS):
        eq = bkt == b
        outs = [jnp.where(eq, tab_ref[b, c], o) for c, o in zip(cols, outs)]
    return outs


def _block_scores(imp, blk, cur):
    score = jnp.where(blk == 0, FORCE, jnp.where(blk == cur, FORCE,
                                                 jnp.where(blk == cur - 1, FORCE, imp)))
    return jnp.where(blk <= cur, score, -FORCE)


def _ahead(col, score, later):
    return jnp.where(later, jnp.where(col >= score, 1.0, 0.0), jnp.where(col > score, 1.0, 0.0))


def _masked_softmax(s, valid):
    s = jnp.where(valid, s, NEG_INF)
    m = jnp.max(s, axis=-1, keepdims=True)
    e = jnp.where(valid, jnp.exp(s - m), 0.0)
    return e / jnp.maximum(jnp.sum(e, axis=-1, keepdims=True), 1e-30)


def _rmsnorm_kernel(x_ref, g_ref, o_ref):
    x = x_ref[...]
    y = x * lax.rsqrt(jnp.mean(x * x, axis=-1, keepdims=True) + EPS)
    o_ref[...] = (y * g_ref[...]).astype(o_ref.dtype)


def _rmsnorm(x, g, out_dtype, tm):
    m, d = x.shape
    return pl.pallas_call(
        _rmsnorm_kernel,
        out_shape=jax.ShapeDtypeStruct((m, d), out_dtype),
        grid=(m // tm,),
        in_specs=[pl.BlockSpec((tm, d), lambda i: (i, 0)),
                  pl.BlockSpec((1, d), lambda i: (0, 0))],
        out_specs=pl.BlockSpec((tm, d), lambda i: (i, 0)),
        compiler_params=_cparams(1),
        name="rmsnorm",
    )(x, g.reshape(1, d))


def _rider_store(os_ref, compute):
    last = pl.program_id(0) == pl.num_programs(0) - 1

    @pl.when(last)
    def _():
        os_ref[0] = compute().astype(os_ref.dtype)

    @pl.when(jnp.logical_not(last))
    def _():
        os_ref[0] = jnp.zeros(os_ref.shape[1:], os_ref.dtype)


def _rider_spec(rows, tn, n_row_tiles):
    return pl.BlockSpec((1, rows, tn),
                        lambda i, j, *_: (jnp.where(i == n_row_tiles - 1, j + 1, 0), 0, 0))


def _rider_in_spec(rows, tn, n_row_tiles, col_off=0):
    return pl.BlockSpec((rows, tn),
                        lambda i, j, *_: (0, jnp.where(i == n_row_tiles - 1, j, 0) + col_off))


def _rider_shape(grid, rows, tn, dtype):
    return jax.ShapeDtypeStruct((grid[1] + 1, rows, tn), dtype)


def _rider_rows(os):
    nj, rows, tn = os.shape[0] - 1, os.shape[1], os.shape[2]
    return os[1:].transpose(1, 0, 2).reshape(rows, nj * tn)


def _mm_res_kernel(a_ref, w_ref, r_ref, as_ref, rs_ref, o_ref, os_ref):
    o_ref[...] = r_ref[...] + _dot(a_ref[...], w_ref[...].astype(BF16))
    _rider_store(os_ref, lambda: rs_ref[...] + _dot(as_ref[...], w_ref[...].astype(BF16)))


def _mm_res(a, w, res, a_s, res_s, *, tm, tn, k_block=0, tk=None, name="mm"):
    m = a.shape[0]
    bs = a_s.shape[0]
    n_cols = w.shape[1]
    tk = a.shape[1] if tk is None else tk
    grid = (m // tm, n_cols // tn)
    out, out_s = pl.pallas_call(
        _mm_res_kernel,
        out_shape=(jax.ShapeDtypeStruct((m, n_cols), F32),
                   _rider_shape(grid, bs, tn, F32)),
        grid=grid,
        in_specs=[pl.BlockSpec((tm, tk), lambda i, j: (i, k_block)),
                  pl.BlockSpec((tk, tn), lambda i, j: (k_block, j)),
                  pl.BlockSpec((tm, tn), lambda i, j: (i, j)),
                  pl.BlockSpec((bs, tk), lambda i, j: (0, k_block)),
                  _rider_in_spec(bs, tn, grid[0])],
        out_specs=(pl.BlockSpec((tm, tn), lambda i, j: (i, j)), _rider_spec(bs, tn, grid[0])),
        compiler_params=_cparams(2),
        name=name,
    )(a, w, res, a_s, res_s)
    return out, _rider_rows(out_s)


def _compress_pages(page_refs, w_ref, o_ref):
    per_page = PAGE_SIZE // CMP_BLOCK
    w = w_ref[...]
    for k, page_ref in enumerate(page_refs):
        x = page_ref[0].reshape(per_page, CMP_BLOCK, KV_ROWS, HEAD_DIM)
        o_ref[0, per_page * k:per_page * (k + 1)] = jnp.sum(x * w[None], axis=1)


def _in_proj_pages_kernel(pt_ref, a_ref, wt_ref, as_ref, *refs):
    page_refs, w8_ref, o_ref, os_ref, kvc_ref = refs[:-4], refs[-4], refs[-3], refs[-2], refs[-1]
    o_ref[...] = _dot_nt(a_ref[...], wt_ref[...].astype(BF16))
    _rider_store(os_ref, lambda: _dot_nt(as_ref[...], wt_ref[...].astype(BF16)))
    _compress_pages(page_refs, w8_ref, kvc_ref)


def _in_proj_all(a, wt, a_s, pages, *, tm, tn):
    m, k = a.shape
    bs = a_s.shape[0]
    assert MAIN_W % tn == 0 and Z_GN == MAIN_W and Z_GA == MAIN_W + tn and 3 * N_HEADS <= tn
    n_gate_blocks = 2 * D_MODEL // tn
    nb = MAIN_W // tn + 1 + n_gate_blocks
    assert OFF_GA + n_gate_blocks * tn == wt.shape[0]

    def w_index(i, j, *_):
        row = jnp.where(j <= MAIN_W // tn, j * tn, OFF_GA + (j - MAIN_W // tn - 1) * tn)
        return (pl.multiple_of(row, 8), 0)

    grid = (m // tm, nb)
    in_specs = [pl.BlockSpec((tm, k), lambda i, j, *_: (i, 0)),
                pl.BlockSpec((pl.Element(tn), pl.Element(k)), w_index),
                pl.BlockSpec((bs, k), lambda i, j, *_: (0, 0))]
    out_spec = pl.BlockSpec((tm, tn), lambda i, j, *_: (i, j))
    out_shape = jax.ShapeDtypeStruct((m, nb * tn), F32)

    cache, page_table, w8 = pages
    b, n_pages = page_table.shape
    per_page = PAGE_SIZE // CMP_BLOCK
    pps = PAGES_PER_STEP
    chunks = n_pages // pps
    assert grid[0] * grid[1] >= b * chunks

    def chunk(i, j):
        t = jnp.minimum(i * nb + j, b * chunks - 1)
        return t // chunks, t % chunks

    def page_index(k):
        def index(i, j, pt):
            seq, c = chunk(i, j)
            return (pt[seq, c * pps + k], 0, 0, 0)
        return index

    def kvc_index(i, j, pt):
        seq, c = chunk(i, j)
        return (seq, c, 0, 0)

    grid_spec = pltpu.PrefetchScalarGridSpec(
        num_scalar_prefetch=1,
        grid=grid,
        in_specs=in_specs
        + [pl.BlockSpec((1, PAGE_SIZE, KV_ROWS, HEAD_DIM), page_index(k)) for k in range(pps)]
        + [pl.BlockSpec((CMP_BLOCK, KV_ROWS, HEAD_DIM), lambda i, j, pt: (0, 0, 0))],
        out_specs=(out_spec, _rider_spec(bs, tn, grid[0]),
                   pl.BlockSpec((1, pps * per_page, KV_ROWS, HEAD_DIM), kvc_index)),
    )
    z, z_s, kvc = pl.pallas_call(
        _in_proj_pages_kernel,
        out_shape=(out_shape,
                   _rider_shape(grid, bs, tn, F32),
                   jax.ShapeDtypeStruct((b, n_pages * per_page, KV_ROWS, HEAD_DIM), F32)),
        grid_spec=grid_spec,
        compiler_params=_cparams(2),
        name="in_proj_pages",
    )(page_table, a, wt, a_s, *([cache] * pps), w8)
    return z, _rider_rows(z_s), kvc


def _swiglu_kernel(a_ref, wg_ref, wu_ref, as_ref, o_ref, os_ref):
    def swiglu(a):
        g = _dot(a, wg_ref[...].astype(BF16))
        return (g * _sigmoid(g)) * _dot(a, wu_ref[...].astype(BF16))

    o_ref[...] = swiglu(a_ref[...]).astype(o_ref.dtype)
    _rider_store(os_ref, lambda: swiglu(as_ref[...]))


def _swiglu(a, wg, wu, a_s, *, tm, tn):
    m, k = a.shape
    bs = a_s.shape[0]
    n = wg.shape[1]
    grid = (m // tm, n // tn)
    out, out_s = pl.pallas_call(
        _swiglu_kernel,
        out_shape=(jax.ShapeDtypeStruct((m, n), BF16), _rider_shape(grid, bs, tn, BF16)),
        grid=grid,
        in_specs=[pl.BlockSpec((tm, k), lambda i, j: (i, 0)),
                  pl.BlockSpec((k, tn), lambda i, j: (0, j)),
                  pl.BlockSpec((k, tn), lambda i, j: (0, j)),
                  pl.BlockSpec((bs, k), lambda i, j: (0, 0))],
        out_specs=(pl.BlockSpec((tm, tn), lambda i, j: (i, j)), _rider_spec(bs, tn, grid[0])),
        compiler_params=_cparams(2),
        name="ffn_gate_up",
    )(a, wg, wu, a_s)
    return out, _rider_rows(out_s)


def _mix_kernel(ya_ref, yb_ref, wa_ref, wb_ref, ga_ref, gb_ref, yas_ref, ybs_ref, gas_ref, gbs_ref,
                o_ref, os_ref):
    def merge(ya, yb, ga, gb):
        return (_sigmoid(ga) * _dot(ya, wa_ref[...].astype(BF16))
                + _sigmoid(gb) * _dot(yb, wb_ref[...].astype(BF16)))

    o_ref[...] = merge(ya_ref[...], yb_ref[...], ga_ref[...], gb_ref[...]).astype(o_ref.dtype)
    _rider_store(os_ref, lambda: merge(yas_ref[...], ybs_ref[...], gas_ref[...], gbs_ref[...]))


def _mix(ya, yb, wa, wb, z, ya_s, yb_s, z_s, *, tm, tn):
    m, k = ya.shape
    bs = ya_s.shape[0]
    n = wa.shape[1]
    assert Z_GA % tn == 0 and Z_GB % tn == 0
    grid = (m // tm, n // tn)
    gate = lambda off: pl.BlockSpec((tm, tn), lambda i, j: (i, j + off // tn))
    gate_s = lambda off: _rider_in_spec(bs, tn, grid[0], off // tn)
    out, out_s = pl.pallas_call(
        _mix_kernel,
        out_shape=(jax.ShapeDtypeStruct((m, n), BF16), _rider_shape(grid, bs, tn, BF16)),
        grid=grid,
        in_specs=[pl.BlockSpec((tm, k), lambda i, j: (i, 0)),
                  pl.BlockSpec((tm, k), lambda i, j: (i, 0)),
                  pl.BlockSpec((k, tn), lambda i, j: (0, j)),
                  pl.BlockSpec((k, tn), lambda i, j: (0, j)),
                  gate(Z_GA), gate(Z_GB),
                  pl.BlockSpec((bs, k), lambda i, j: (0, 0)),
                  pl.BlockSpec((bs, k), lambda i, j: (0, 0)),
                  gate_s(Z_GA), gate_s(Z_GB)],
        out_specs=(pl.BlockSpec((tm, tn), lambda i, j: (i, j)), _rider_spec(bs, tn, grid[0])),
        compiler_params=_cparams(2),
        name="branch_merge",
    )(ya, yb, wa, wb, z, z, ya_s, yb_s, z_s, z_s)
    return out, _rider_rows(out_s)


def _lru_gates(xc, wa_ref, ba_ref, wi_ref, bi_ref, lam_ref, a_ref, u_ref):
    for n in range(N_RNN_BLOCKS):
        sl = slice(n * RNN_BLOCK, (n + 1) * RNN_BLOCK)
        xn = xc[:, sl]
        xb = xn.astype(BF16)
        r = _sigmoid(_dot(xb, wa_ref[n].astype(BF16)) + ba_ref[:, sl])
        i = _sigmoid(_dot(xb, wi_ref[n].astype(BF16)) + bi_ref[:, sl])
        neg_lam = -lam_ref[:, sl]
        softplus = jnp.maximum(neg_lam, 0.0) + jnp.log1p(jnp.exp(-jnp.abs(neg_lam)))
        log_a = (-LRU_C * r) * softplus
        a = jnp.exp(log_a)
        a_ref[:, sl] = a
        u_ref[:, sl] = jnp.sqrt(-jnp.tanh(log_a) * (a * a + 1.0)) * (i * xn)


def _rglru_prompt_kernel(xr_ref, gr_ref, cw_ref, cb_ref, wa_ref, ba_ref, wi_ref, bi_ref, lam_ref,
                         ya_ref, conv_ref, hl_ref, xp_ref, a_ref, u_ref, h_ref, carry_ref):
    ti = pl.program_id(1)
    tt, c = xr_ref.shape

    @pl.when(ti == 0)
    def _():
        xp_ref[0:8, :] = jnp.zeros((8, c), F32)
        carry_ref[...] = jnp.zeros((1, c), F32)

    x = xr_ref[...]
    xp_ref[8:8 + tt, :] = x
    xc = cb_ref[...]
    for k in range(CONV_W):
        xc = xc + cw_ref[k:k + 1, :] * xp_ref[5 + k:5 + k + tt, :]
    xp_ref[0:8, :] = xp_ref[tt:tt + 8, :]

    _lru_gates(xc, wa_ref, ba_ref, wi_ref, bi_ref, lam_ref, a_ref, u_ref)

    row = lax.broadcasted_iota(I32, (8, c), 0)

    def chunk(ci, carry):
        off = pl.multiple_of(ci * 8, 8)
        a = a_ref[pl.ds(off, 8), :]
        b = u_ref[pl.ds(off, 8), :]
        for s in (1, 2, 4):
            keep = row >= s
            b = jnp.where(keep, a * pltpu.roll(b, s, 0) + b, b)
            a = jnp.where(keep, a * pltpu.roll(a, s, 0), a)
        h = a * carry + b
        h_ref[pl.ds(off, 8), :] = h
        return h[7:8, :]

    carry = lax.fori_loop(0, tt // 8, chunk, carry_ref[...])
    carry_ref[...] = carry
    ya_ref[...] = (h_ref[...] * _gelu_tanh(gr_ref[...])).astype(ya_ref.dtype)

    @pl.when(ti == pl.num_programs(1) - 1)
    def _():
        conv_ref[0] = x[tt - (CONV_W - 1):, :]
        hl_ref[0] = carry


def _rglru_prompt(z, batch, seq, conv_w, conv_b, lru_wa, lru_ba, lru_wi, lru_bi, lam, *, tt):
    nt = seq // tt
    c = D_RNN
    row = lambda b, t: (b * nt + t, 0)
    full2 = lambda b, t: (0, 0)
    full3 = lambda b, t: (0, 0, 0)
    return pl.pallas_call(
        _rglru_prompt_kernel,
        out_shape=(jax.ShapeDtypeStruct((batch * seq, c), BF16),
                   jax.ShapeDtypeStruct((batch, CONV_W - 1, c), F32),
                   jax.ShapeDtypeStruct((batch, 1, c), F32)),
        grid=(batch, nt),
        in_specs=[pl.BlockSpec((tt, c), row),
                  pl.BlockSpec((tt, c), lambda b, t: (b * nt + t, 1)),
                  pl.BlockSpec((CONV_W, c), full2),
                  pl.BlockSpec((1, c), full2),
                  pl.BlockSpec((N_RNN_BLOCKS, RNN_BLOCK, RNN_BLOCK), full3),
                  pl.BlockSpec((1, c), full2),
                  pl.BlockSpec((N_RNN_BLOCKS, RNN_BLOCK, RNN_BLOCK), full3),
                  pl.BlockSpec((1, c), full2),
                  pl.BlockSpec((1, c), full2)],
        out_specs=(pl.BlockSpec((tt, c), row),
                   pl.BlockSpec((1, CONV_W - 1, c), lambda b, t: (b, 0, 0)),
                   pl.BlockSpec((1, 1, c), lambda b, t: (b, 0, 0))),
        scratch_shapes=[pltpu.VMEM((tt + 8, c), F32), pltpu.VMEM((tt, c), F32),
                        pltpu.VMEM((tt, c), F32), pltpu.VMEM((tt, c), F32),
                        pltpu.VMEM((1, c), F32)],
        compiler_params=_cparams(2),
        name="rglru_prompt",
    )(z, z, conv_w, conv_b.reshape(1, c), lru_wa, lru_ba.reshape(1, c), lru_wi,
      lru_bi.reshape(1, c), lam.reshape(1, c))


def _rglru_step_kernel(xr_ref, gr_ref, buf_ref, h0_ref, cw_ref, cb_ref, wa_ref, ba_ref, wi_ref,
                       bi_ref, lam_ref, ya_ref, hn_ref, a_ref, u_ref):
    x = xr_ref[...]
    xc = cb_ref[...]
    for k in range(CONV_W - 1):
        xc = xc + cw_ref[k:k + 1, :] * buf_ref[k]
    xc = xc + cw_ref[CONV_W - 1:CONV_W, :] * x
    _lru_gates(xc, wa_ref, ba_ref, wi_ref, bi_ref, lam_ref, a_ref, u_ref)
    h = a_ref[...] * h0_ref[...] + u_ref[...]
    hn_ref[...] = h
    ya_ref[...] = (h * _gelu_tanh(gr_ref[...])).astype(ya_ref.dtype)


def _rglru_step(z, buf_t, h0, conv_w, conv_b, lru_wa, lru_ba, lru_wi, lru_bi, lam):
    b, c = h0.shape
    full2 = lambda i: (0, 0)
    full3 = lambda i: (0, 0, 0)
    return pl.pallas_call(
        _rglru_step_kernel,
        out_shape=(jax.ShapeDtypeStruct((b, c), BF16), jax.ShapeDtypeStruct((b, c), F32)),
        grid=(1,),
        in_specs=[pl.BlockSpec((b, c), full2),
                  pl.BlockSpec((b, c), lambda i: (0, 1)),
                  pl.BlockSpec((CONV_W - 1, b, c), full3),
                  pl.BlockSpec((b, c), full2),
                  pl.BlockSpec((CONV_W, c), full2),
                  pl.BlockSpec((1, c), full2),
                  pl.BlockSpec((N_RNN_BLOCKS, RNN_BLOCK, RNN_BLOCK), full3),
                  pl.BlockSpec((1, c), full2),
                  pl.BlockSpec((N_RNN_BLOCKS, RNN_BLOCK, RNN_BLOCK), full3),
                  pl.BlockSpec((1, c), full2),
                  pl.BlockSpec((1, c), full2)],
        out_specs=(pl.BlockSpec((b, c), full2), pl.BlockSpec((b, c), full2)),
        scratch_shapes=[pltpu.VMEM((b, c), F32), pltpu.VMEM((b, c), F32)],
        compiler_params=_cparams(1),
        name="rglru_step",
    )(z, z, buf_t, h0, conv_w, conv_b.reshape(1, c), lru_wa, lru_ba.reshape(1, c), lru_wi,
      lru_bi.reshape(1, c), lam.reshape(1, c))


def _bias_tiles_kernel(tab_ref, bd_ref, bo_ref):
    h = pl.program_id(0)
    t = bd_ref.shape[1]
    d0 = lax.broadcasted_iota(I32, (t, t), 1) - lax.broadcasted_iota(I32, (t, t), 0)
    bd_ref[0] = jnp.where(d0 >= 0, _bias_lookup(_bucket(d0), tab_ref, [h])[0] * LOG2E, NEG_INF)
    bo_ref[0] = _bias_lookup(_bucket(d0 + t), tab_ref, [h])[0] * LOG2E


def _bias_tiles(rel_bias, t):
    shp = jax.ShapeDtypeStruct((N_HEADS, t, t), F32)
    spec = pl.BlockSpec((1, t, t), lambda h: (h, 0, 0))
    return pl.pallas_call(
        _bias_tiles_kernel,
        out_shape=(shp, shp),
        grid=(N_HEADS,),
        in_specs=[pl.BlockSpec(memory_space=pltpu.SMEM)],
        out_specs=(spec, spec),
        compiler_params=_cparams(1),
        name="bias_tiles",
    )(rel_bias)


def _cmp_prompt_kernel(tab_ref, q_ref, kc_ref, vc_ref, wk_ref, wv_ref, gate_ref,
                       oc_ref, sel_ref, kcmp_ref, vcmp_ref, vcmp_t_ref):
    g = pl.program_id(1)
    qi = pl.program_id(2)
    tq = q_ref.shape[0]
    seq = kc_ref.shape[0]
    nblk = seq // CMP_BLOCK

    @pl.when(qi == 0)
    def _():
        k3 = kc_ref[...].reshape(nblk, CMP_BLOCK, HEAD_DIM)
        v3 = vc_ref[...].reshape(nblk, CMP_BLOCK, HEAD_DIM)
        kcmp_ref[...] = jnp.sum(k3 * wk_ref[0, 0][None], axis=1).astype(BF16)
        vcmp_ref[...] = jnp.zeros(vcmp_ref.shape, F32)
        vcmp_ref[0:nblk, :] = jnp.sum(v3 * wv_ref[0, 0][None], axis=1)
        vcmp_t_ref[...] = jnp.transpose(vcmp_ref[...]).astype(BF16)

    tpos = qi * tq + lax.broadcasted_iota(I32, (nblk, tq), 1)
    blk = lax.broadcasted_iota(I32, (nblk, tq), 0)
    dist = tpos - (blk * CMP_BLOCK + (CMP_BLOCK - 1))
    valid = dist >= 0
    biases = _bias_lookup(_bucket(dist), tab_ref, [g * HPG + h for h in range(HPG)])
    kb = kcmp_ref[...]
    vt = vcmp_t_ref[:, 0:nblk]
    gates = _sigmoid(gate_ref[0, 0])
    heads = range(HPG)
    cols = [slice(h * HEAD_DIM, (h + 1) * HEAD_DIM) for h in heads]
    scores = [_dot_nt(kb, q_ref[:, cols[h]].astype(BF16)) for h in heads]
    scores = [jnp.where(valid, s * SCALE + b, NEG_INF) for s, b in zip(scores, biases)]
    e = [jnp.where(valid, jnp.exp(s - jnp.max(s, axis=0, keepdims=True)), 0.0) for s in scores]
    probs = [x / jnp.maximum(jnp.sum(x, axis=0, keepdims=True), 1e-30) for x in e]
    imp = probs[0]
    for h in range(1, HPG):
        imp = imp + probs[h]
    outs = [_dot(vt, p.astype(BF16)) for p in probs]
    for h in heads:
        oc_ref[:, cols[h]] = jnp.transpose(outs[h] * gates[h:h + 1, :])

    cur = jnp.right_shift(tpos, 6)
    score = _block_scores(imp, blk, cur)
    rank = jnp.zeros((nblk, tq), F32)
    for j in range(nblk):
        rank = rank + _ahead(score[j:j + 1, :], score, blk > j)
    sel_ref[0, 0] = jnp.where(rank < float(min(N_SEL, nblk)), 0.0, NEG_INF)


def _cmp_prompt(z, rel_bias, wexp, gn_t, batch, seq):
    tq = ATT_T
    nq = seq // tq
    nblk = seq // CMP_BLOCK
    m = batch * seq
    qcol = OFF_Q // (HPG * HEAD_DIM)
    kcol = OFF_KVC // HEAD_DIM
    return pl.pallas_call(
        _cmp_prompt_kernel,
        out_shape=(jax.ShapeDtypeStruct((m, Q_W), F32),
                   jax.ShapeDtypeStruct((batch, N_KV, nblk, seq), F32)),
        grid=(batch, N_KV, nq),
        in_specs=[pl.BlockSpec(memory_space=pltpu.SMEM),
                  pl.BlockSpec((tq, HPG * HEAD_DIM), lambda b, g, i: (b * nq + i, qcol + g)),
                  pl.BlockSpec((seq, HEAD_DIM), lambda b, g, i: (b, kcol + g)),
                  pl.BlockSpec((seq, HEAD_DIM), lambda b, g, i: (b, kcol + N_KV + g)),
                  pl.BlockSpec((1, 1, CMP_BLOCK, HEAD_DIM), lambda b, g, i: (0, g, 0, 0)),
                  pl.BlockSpec((1, 1, CMP_BLOCK, HEAD_DIM), lambda b, g, i: (1, g, 0, 0)),
                  pl.BlockSpec((1, 1, HPG, tq), lambda b, g, i: (0, g, 0, b * nq + i))],
        out_specs=(pl.BlockSpec((tq, HPG * HEAD_DIM), lambda b, g, i: (b * nq + i, g)),
                   pl.BlockSpec((1, 1, nblk, tq), lambda b, g, i: (b, g, 0, i))),
        scratch_shapes=[pltpu.VMEM((nblk, HEAD_DIM), BF16), pltpu.VMEM((HEAD_DIM, HEAD_DIM), F32),
                        pltpu.VMEM((HEAD_DIM, HEAD_DIM), BF16)],
        compiler_params=_cparams(3),
        name="cmp_prompt",
    )(rel_bias, z, z, z, wexp, wexp, gn_t)


def _attn_init(m_ref, l_ref, acc_ref):
    m_ref[...] = jnp.full(m_ref.shape, NEG_INF, F32)
    l_ref[...] = jnp.zeros(l_ref.shape, F32)
    acc_ref[...] = jnp.zeros(acc_ref.shape, F32)


def _attn_update(h, qh, k, v, bias, valid, m_ref, l_ref, acc_ref):
    s = _dot_nt(qh, k) * SCALE + bias
    if valid is not None:
        s = jnp.where(valid, s, NEG_INF)
    m_old = m_ref[h]
    m_new = jnp.maximum(m_old, jnp.max(s, axis=-1, keepdims=True))
    alpha = jnp.exp(m_old - m_new)
    e = jnp.exp(s - m_new)
    if valid is not None:
        e = jnp.where(valid, e, 0.0)
    l_ref[h] = alpha * l_ref[h] + jnp.sum(e, axis=-1, keepdims=True)
    acc_ref[h] = alpha * acc_ref[h] + _dot(e.astype(BF16), v)
    m_ref[h] = m_new


def _attn_prologue(qi, q_ref, k_ref, v_ref, kb_ref, vt_ref, qb_ref, m_ref, acc_ref):
    n_tiles, t = kb_ref.shape[0], kb_ref.shape[1]

    @pl.when(qi == 0)
    def _():
        for kj in range(n_tiles):
            rows = slice(kj * t, (kj + 1) * t)
            kb_ref[kj] = k_ref[rows, :].astype(BF16)
            vt_ref[kj, 0:HEAD_DIM] = jnp.transpose(v_ref[rows, :]).astype(BF16)
            vt_ref[kj, HEAD_DIM:] = jnp.ones((vt_ref.shape[1] - HEAD_DIM, t), BF16)

    for h in range(HPG):
        qb_ref[h] = q_ref[:, h * HEAD_DIM:(h + 1) * HEAD_DIM].astype(BF16)
    m_ref[...] = jnp.full(m_ref.shape, NEG_INF, F32)
    acc_ref[...] = jnp.zeros(acc_ref.shape, F32)


def _attn_update_t(tiles, qb_ref, m_ref, acc_ref):
    heads = range(HPG)
    scores = [[_dot_nt(k, qb_ref[h]) for h in heads] for k, _, _, _ in tiles]
    scores = [[s * (SCALE * LOG2E) + b for s, b in zip(ss, tile[2])] for ss, tile in zip(scores, tiles)]
    scores = [ss if tile[3] is None else [jnp.where(tile[3], s, NEG_INF) for s in ss]
              for ss, tile in zip(scores, tiles)]
    m_old = [m_ref[h] for h in heads]
    m_new = list(m_old)
    for ss in scores:
        m_new = [jnp.maximum(mn, jnp.max(s, axis=0, keepdims=True)) for mn, s in zip(m_new, ss)]
    alpha = [jnp.exp2(mo - mn) for mo, mn in zip(m_old, m_new)]
    e = [[jnp.exp2(s - mn).astype(BF16) for s, mn in zip(ss, m_new)] for ss in scores]
    for h in heads:
        m_ref[h] = m_new[h]
    pv = [[_dot(tile[1], es[h]) for h in heads] for es, tile in zip(e, tiles)]
    for h in heads:
        total = pv[0][h]
        for t in range(1, len(tiles)):
            total = total + pv[t][h]
        acc_ref[h] = alpha[h] * acc_ref[h] + total


def _attn_output_t(h, gates, acc_ref):
    denom = jnp.maximum(acc_ref[h, HEAD_DIM:HEAD_DIM + 1, :], 1e-30)
    o_t = acc_ref[h, 0:HEAD_DIM, :] / denom * gates[h:h + 1, :]
    return jnp.transpose(o_t)


def _sel_prompt_kernel(tab_ref, q_ref, k_ref, v_ref, bd_ref, bo_ref, gate_ref, sel_ref,
                       os_ref, kb_ref, vt_ref, qb_ref, m_ref, acc_ref):
    g = pl.program_id(1)
    qi = pl.program_id(2)
    t = q_ref.shape[0]
    per_tile = t // CMP_BLOCK
    _attn_prologue(qi, q_ref, k_ref, v_ref, kb_ref, vt_ref, qb_ref, m_ref, acc_ref)

    def tile(kj, kind):
        rows = [jnp.broadcast_to(sel_ref[0, 0, pl.ds(kj * per_tile + i, 1), :], (CMP_BLOCK, t))
                for i in range(per_tile)]
        mask = jnp.concatenate(rows, axis=0)
        if kind == "far":
            biases = [mask + tab_ref[N_BUCKETS - 1, g * HPG + h] * LOG2E for h in range(HPG)]
        elif kind == "off":
            biases = [mask + bo_ref[h] for h in range(HPG)]
        else:
            biases = [mask + bd_ref[h] for h in range(HPG)]
        return (kb_ref[kj], vt_ref[kj], biases, None)

    def update(*tiles):
        _attn_update_t(tiles, qb_ref, m_ref, acc_ref)

    n_far = jnp.maximum(qi - 1, 0)

    def pair_body(p, carry):
        update(tile(2 * p, "far"), tile(2 * p + 1, "far"))
        return carry

    lax.fori_loop(0, n_far // 2, pair_body, 0)

    @pl.when(qi == 0)
    def _():
        update(tile(qi, "diag"))

    @pl.when(jnp.logical_and(qi >= 1, n_far % 2 == 0))
    def _():
        update(tile(qi - 1, "off"), tile(qi, "diag"))

    @pl.when(n_far % 2 == 1)
    def _():
        update(tile(qi - 2, "far"), tile(qi - 1, "off"), tile(qi, "diag"))

    gates = _sigmoid(gate_ref[0, 0])
    for h in range(HPG):
        os_ref[:, h * HEAD_DIM:(h + 1) * HEAD_DIM] = _attn_output_t(h, gates, acc_ref)


def _win_prompt_kernel(tab_ref, q_ref, k_ref, v_ref, bd_ref, bo_ref, gate_ref, oc_ref, os_ref,
                       yb_ref, kb_ref, vt_ref, qb_ref, m_ref, acc_ref):
    g = pl.program_id(1)
    qi = pl.program_id(2)
    t = q_ref.shape[0]
    _attn_prologue(qi, q_ref, k_ref, v_ref, kb_ref, vt_ref, qb_ref, m_ref, acc_ref)
    key = lax.broadcasted_iota(I32, (t, t), 0)
    qry = lax.broadcasted_iota(I32, (t, t), 1)

    def tile(kj, kind):
        if kind == "far":
            biases = [tab_ref[N_BUCKETS - 1, g * HPG + h] * LOG2E for h in range(HPG)]
            valid = qry < key
        elif kind == "off":
            biases, valid = [bo_ref[h] for h in range(HPG)], None
        else:
            biases, valid = [bd_ref[h] for h in range(HPG)], None
        return (kb_ref[kj], vt_ref[kj], biases, valid)

    def update(*tiles):
        _attn_update_t(tiles, qb_ref, m_ref, acc_ref)

    @pl.when(qi == 0)
    def _():
        update(tile(qi, "diag"))

    @pl.when(qi == 1)
    def _():
        update(tile(qi, "diag"), tile(qi - 1, "off"))

    @pl.when(qi >= 2)
    def _():
        update(tile(qi, "diag"), tile(qi - 1, "off"), tile(qi - 2, "far"))

    gates = _sigmoid(gate_ref[0, 0])
    for h in range(HPG):
        sl = slice(h * HEAD_DIM, (h + 1) * HEAD_DIM)
        o = _attn_output_t(h, gates, acc_ref)
        yb_ref[:, sl] = (oc_ref[:, sl] + os_ref[:, sl] + o).astype(yb_ref.dtype)


def _attn_prompt_specs(batch, seq, kv_off, gate_idx):
    t = ATT_T
    nq = seq // t
    qcol = OFF_Q // (HPG * HEAD_DIM)
    kcol = kv_off // HEAD_DIM
    rowblk = lambda b, g, i: (b * nq + i, g)
    specs = [pl.BlockSpec(memory_space=pltpu.SMEM),
             pl.BlockSpec((t, HPG * HEAD_DIM), lambda b, g, i: (b * nq + i, qcol + g)),
             pl.BlockSpec((seq, HEAD_DIM), lambda b, g, i: (b, kcol + g)),
             pl.BlockSpec((seq, HEAD_DIM), lambda b, g, i: (b, kcol + N_KV + g)),
             pl.BlockSpec((HPG, t, t), lambda b, g, i: (g, 0, 0)),
             pl.BlockSpec((HPG, t, t), lambda b, g, i: (g, 0, 0)),
             pl.BlockSpec((1, 1, HPG, t), lambda b, g, i: (gate_idx, g, 0, b * nq + i))]
    scratch = [pltpu.VMEM((seq // t, t, HEAD_DIM), BF16),
               pltpu.VMEM((seq // t, HEAD_DIM + DENOM_ROWS, t), BF16),
               pltpu.VMEM((HPG, t, HEAD_DIM), BF16),
               pltpu.VMEM((HPG, 1, t), F32),
               pltpu.VMEM((HPG, HEAD_DIM + DENOM_ROWS, t), F32)]
    return specs, scratch, pl.BlockSpec((t, HPG * HEAD_DIM), rowblk), (batch, N_KV, nq)


def _sel_prompt(z, rel_bias, bd, bo, gn_t, sel, batch, seq):
    specs, scratch, out_spec, grid = _attn_prompt_specs(batch, seq, OFF_KVS, 1)
    t = ATT_T
    nblk = seq // CMP_BLOCK
    specs += [pl.BlockSpec((1, 1, nblk, t), lambda b, g, i: (b, g, 0, i))]
    return pl.pallas_call(
        _sel_prompt_kernel,
        out_shape=jax.ShapeDtypeStruct((batch * seq, Q_W), F32),
        grid=grid, in_specs=specs, out_specs=out_spec, scratch_shapes=scratch,
        compiler_params=_cparams(3), name="sel_prompt",
    )(rel_bias, z, z, z, bd, bo, gn_t, sel)


def _win_prompt(z, rel_bias, bd, bo, gn_t, oc, osel, batch, seq):
    specs, scratch, out_spec, grid = _attn_prompt_specs(batch, seq, OFF_KVW, 2)
    specs += [out_spec, out_spec]
    return pl.pallas_call(
        _win_prompt_kernel,
        out_shape=jax.ShapeDtypeStruct((batch * seq, Q_W), BF16),
        grid=grid, in_specs=specs, out_specs=out_spec, scratch_shapes=scratch,
        compiler_params=_cparams(3), name="win_prompt",
    )(rel_bias, z, z, z, bd, bo, gn_t, oc, osel)


def _cmp_sample_kernel(tab_ref, q_ref, kvc_ref, new_ref, w_ref, gate_ref,
                       oc_ref, idx_ref, kall_ref, vall_ref, imp_ref, *, q_pos):
    nblk = kvc_ref.shape[1]
    nk = kall_ref.shape[0]
    blk = lax.broadcasted_iota(I32, (1, nk), 1)
    dist = q_pos - (blk * CMP_BLOCK + (CMP_BLOCK - 1))
    valid = dist >= 0
    bkt = _bucket(dist)
    new_c = new_ref[0] * w_ref[0]
    kall_ref[...] = jnp.zeros(kall_ref.shape, F32)
    vall_ref[...] = jnp.zeros(vall_ref.shape, F32)
    for g in range(N_KV):
        kall_ref[0:nblk, :] = kvc_ref[0, :, g, :]
        vall_ref[0:nblk, :] = kvc_ref[0, :, N_KV + g, :]
        kall_ref[nblk:nblk + 1, :] = new_c[g:g + 1, :]
        vall_ref[nblk:nblk + 1, :] = new_c[N_KV + g:N_KV + g + 1, :]
        kb = kall_ref[...].astype(BF16)
        vb = vall_ref[...].astype(BF16)
        biases = _bias_lookup(bkt, tab_ref, [g * HPG + h for h in range(HPG)])
        rows = lax.broadcasted_iota(I32, (HROWS, nk), 0)
        bias = jnp.zeros((HROWS, nk), F32)
        for h in range(HPG):
            bias = jnp.where(rows == h, biases[h], bias)
        s = _dot_nt(q_ref[0, g].astype(BF16), kb) * SCALE + bias
        p = _masked_softmax(s, valid)
        gate = _sigmoid(gate_ref[0, 0, g])
        oc_ref[0, g] = gate * _dot(p.astype(BF16), vb)
        head_row = jnp.where(rows < HPG, p, 0.0)
        imp_ref[g:g + 1, :] = jnp.sum(head_row, axis=0, keepdims=True)

    imp = imp_ref[0:N_KV, :]
    blk4 = lax.broadcasted_iota(I32, (N_KV, nk), 1)
    cur = q_pos // CMP_BLOCK
    score = _block_scores(imp, blk4, cur)
    rank = jnp.zeros((N_KV, nk), F32)
    for j in range(nblk + 1):
        col = score[:, j:j + 1]
        rank = rank + _ahead(col, score, blk4 > j)
    rank = jnp.where(blk4 <= nblk, rank, float(nk))
    blkf = blk4.astype(F32)
    slot = lax.broadcasted_iota(I32, (N_KV, N_SEL), 1)
    idx = jnp.zeros((N_KV, N_SEL), F32)
    for r in range(N_SEL):
        pick = jnp.sum(jnp.where(rank == float(r), blkf, 0.0), axis=-1, keepdims=True)
        idx = jnp.where(slot == r, pick, idx)
    idx_ref[0] = idx.astype(I32)


def _cmp_sample(rel_bias, q8, kvc, kvc_new, w8, gates8, q_pos):
    b, nblk = kvc.shape[:2]
    nk = 2 * HEAD_DIM
    assert nblk + 1 <= nk and q_pos // CMP_BLOCK == nblk
    return pl.pallas_call(
        functools.partial(_cmp_sample_kernel, q_pos=q_pos),
        out_shape=(jax.ShapeDtypeStruct((b, N_KV, HROWS, HEAD_DIM), F32),
                   jax.ShapeDtypeStruct((b, N_KV, N_SEL), I32)),
        grid=(b,),
        in_specs=[pl.BlockSpec(memory_space=pltpu.SMEM),
                  pl.BlockSpec((1, N_KV, HROWS, HEAD_DIM), lambda i: (i, 0, 0, 0)),
                  pl.BlockSpec((1, nblk, KV_ROWS, HEAD_DIM), lambda i: (i, 0, 0, 0)),
                  pl.BlockSpec((1, KV_ROWS, HEAD_DIM), lambda i: (i, 0, 0)),
                  pl.BlockSpec((CMP_BLOCK, KV_ROWS, HEAD_DIM), lambda i: (0, 0, 0)),
                  pl.BlockSpec((1, 1, N_KV, HROWS, 1), lambda i: (i, 0, 0, 0, 0))],
        out_specs=(pl.BlockSpec((1, N_KV, HROWS, HEAD_DIM), lambda i: (i, 0, 0, 0)),
                   pl.BlockSpec((1, N_KV, N_SEL), lambda i: (i, 0, 0))),
        scratch_shapes=[pltpu.VMEM((nk, HEAD_DIM), F32), pltpu.VMEM((nk, HEAD_DIM), F32),
                        pltpu.VMEM((HROWS, nk), F32)],
        compiler_params=_cparams(1),
        name="cmp_sample",
    )(rel_bias, q8, kvc, kvc_new, w8, gates8)


def _sel_bias_kernel(tab_ref, o_ref, *, q_pos):
    g = pl.program_id(0)
    shape = o_ref.shape[1:]
    blk = lax.broadcasted_iota(I32, shape, 0)
    rows = lax.broadcasted_iota(I32, shape, 1)
    offs = lax.broadcasted_iota(I32, shape, 2)
    dist = q_pos - (blk * CMP_BLOCK + offs)
    biases = _bias_lookup(_bucket(dist), tab_ref, [g * HPG + h for h in range(HPG)])
    bias = jnp.zeros(shape, F32)
    for h in range(HPG):
        bias = jnp.where(rows == h, biases[h], bias)
    o_ref[0] = jnp.where(dist >= 0, bias, NEG_INF)


def _sel_bias(rel_bias, q_pos, n_blocks):
    return pl.pallas_call(
        functools.partial(_sel_bias_kernel, q_pos=q_pos),
        out_shape=jax.ShapeDtypeStruct((N_KV, n_blocks, HROWS, CMP_BLOCK), F32),
        grid=(N_KV,),
        in_specs=[pl.BlockSpec(memory_space=pltpu.SMEM)],
        out_specs=pl.BlockSpec((1, n_blocks, HROWS, CMP_BLOCK), lambda g: (g, 0, 0, 0)),
        compiler_params=_cparams(1),
        name="sel_bias",
    )(rel_bias)


def _sel_sample_kernel(pt_ref, idx_ref, bias_ref, q_ref, *refs, n_past):
    nb = SEL_BLOCKS_PER_STEP
    blk_refs = refs[:N_KV * nb]
    new_ref, gate_ref, os_ref, kcat_ref, vcat_ref, bcat_ref, m_ref, l_ref, acc_ref = refs[N_KV * nb:]
    b = pl.program_id(0)
    step = pl.program_id(1)

    @pl.when(step == 0)
    def _():
        _attn_init(m_ref, l_ref, acc_ref)

    first = lax.broadcasted_iota(I32, (CMP_BLOCK, HEAD_DIM), 0) == 0
    for g in range(N_KV):
        kn = jnp.where(first, new_ref[0][g:g + 1, :], 0.0)
        vn = jnp.where(first, new_ref[0][N_KV + g:N_KV + g + 1, :], 0.0)
        for j in range(nb):
            n = idx_ref[b, g, step * nb + j]
            is_new = n >= n_past
            blk_ref = blk_refs[g * nb + j]
            keys = slice(j * CMP_BLOCK, (j + 1) * CMP_BLOCK)
            kcat_ref[keys, :] = jnp.where(is_new, kn, blk_ref[0, :, g, :]).astype(BF16)
            vcat_ref[keys, :] = jnp.where(is_new, vn, blk_ref[0, :, N_KV + g, :]).astype(BF16)
            bcat_ref[:, keys] = bias_ref[g, n]
        _attn_update(g, q_ref[0, g].astype(BF16), kcat_ref[...], vcat_ref[...], bcat_ref[...], None,
                     m_ref, l_ref, acc_ref)

    @pl.when(step == pl.num_programs(1) - 1)
    def _():
        for g in range(N_KV):
            o = acc_ref[g] / jnp.maximum(l_ref[g], 1e-30)
            os_ref[0, g] = _sigmoid(gate_ref[0, 0, g]) * o


def _sel_sample(page_table, idx, rel_bias, q8, cache_half, kvs_new, gates8, q_pos):
    b = q8.shape[0]
    nb = SEL_BLOCKS_PER_STEP
    n_past = q_pos // CMP_BLOCK
    per_page = PAGE_SIZE // CMP_BLOCK
    bias = _sel_bias(rel_bias, q_pos, n_past + 1)

    def blk_spec(g, j):
        def index(i, s, pt, ix):
            n = jnp.minimum(ix[i, g, s * nb + j], n_past - 1)
            return (pt[i, n // per_page] * per_page + n % per_page, 0, 0, 0)
        return pl.BlockSpec((1, CMP_BLOCK, KV_ROWS, HEAD_DIM), index)

    hspec = pl.BlockSpec((1, N_KV, HROWS, HEAD_DIM), lambda i, s, pt, ix: (i, 0, 0, 0))
    grid_spec = pltpu.PrefetchScalarGridSpec(
        num_scalar_prefetch=2,
        grid=(b, N_SEL // nb),
        in_specs=[pl.BlockSpec(bias.shape, lambda i, s, pt, ix: (0, 0, 0, 0)), hspec]
        + [blk_spec(g, j) for g in range(N_KV) for j in range(nb)]
        + [pl.BlockSpec((1, KV_ROWS, HEAD_DIM), lambda i, s, pt, ix: (i, 0, 0)),
           pl.BlockSpec((1, 1, N_KV, HROWS, 1), lambda i, s, pt, ix: (i, 1, 0, 0, 0))],
        out_specs=hspec,
        scratch_shapes=[pltpu.VMEM((nb * CMP_BLOCK, HEAD_DIM), BF16),
                        pltpu.VMEM((nb * CMP_BLOCK, HEAD_DIM), BF16),
                        pltpu.VMEM((HROWS, nb * CMP_BLOCK), F32),
                        pltpu.VMEM((N_KV, HROWS, 1), F32), pltpu.VMEM((N_KV, HROWS, 1), F32),
                        pltpu.VMEM((N_KV, HROWS, HEAD_DIM), F32)],
    )
    return pl.pallas_call(
        functools.partial(_sel_sample_kernel, n_past=n_past),
        out_shape=jax.ShapeDtypeStruct((b, N_KV, HROWS, HEAD_DIM), F32),
        grid_spec=grid_spec,
        compiler_params=_cparams(2),
        name="sel_sample",
    )(page_table, idx, bias, q8, *([cache_half] * (N_KV * nb)), kvs_new, gates8)


def _win_sample_kernel(tab_ref, q_ref, win_ref, new_ref, gate_ref, oc_ref, os_ref,
                       yb_ref, nwin_ref, kall_ref, vall_ref):
    wc = win_ref.shape[1]
    nk = kall_ref.shape[0]
    nwin_ref[0, 0:wc - 1] = win_ref[0, 1:wc]
    nwin_ref[0, wc - 1] = new_ref[0]
    kidx = lax.broadcasted_iota(I32, (1, nk), 1)
    dist = wc - kidx
    valid = jnp.where(dist >= 0, dist, WINDOW) < WINDOW
    bkt = _bucket(dist)
    kall_ref[...] = jnp.zeros(kall_ref.shape, F32)
    vall_ref[...] = jnp.zeros(vall_ref.shape, F32)
    for g in range(N_KV):
        kall_ref[0:wc, :] = win_ref[0, :, g, :]
        vall_ref[0:wc, :] = win_ref[0, :, N_KV + g, :]
        kall_ref[wc:wc + 1, :] = new_ref[0][g:g + 1, :]
        vall_ref[wc:wc + 1, :] = new_ref[0][N_KV + g:N_KV + g + 1, :]
        biases = _bias_lookup(bkt, tab_ref, [g * HPG + h for h in range(HPG)])
        rows = lax.broadcasted_iota(I32, (HROWS, nk), 0)
        bias = jnp.zeros((HROWS, nk), F32)
        for h in range(HPG):
            bias = jnp.where(rows == h, biases[h], bias)
        s = _dot_nt(q_ref[0, g].astype(BF16), kall_ref[...].astype(BF16)) * SCALE + bias
        p = _masked_softmax(s, valid)
        o = _dot(p.astype(BF16), vall_ref[...].astype(BF16))
        yb_ref[0, g] = oc_ref[0, g] + os_ref[0, g] + _sigmoid(gate_ref[0, 0, g]) * o


def _win_sample(rel_bias, q8, win, kvw_new, gates8, oc, osel):
    b, wc = win.shape[:2]
    nk = wc + HEAD_DIM
    hspec = pl.BlockSpec((1, N_KV, HROWS, HEAD_DIM), lambda i: (i, 0, 0, 0))
    wspec = pl.BlockSpec((1, wc, KV_ROWS, HEAD_DIM), lambda i: (i, 0, 0, 0))
    return pl.pallas_call(
        _win_sample_kernel,
        out_shape=(jax.ShapeDtypeStruct((b, N_KV, HROWS, HEAD_DIM), F32),
                   jax.ShapeDtypeStruct(win.shape, F32)),
        grid=(b,),
        in_specs=[pl.BlockSpec(memory_space=pltpu.SMEM),
                  hspec,
                  wspec,
                  pl.BlockSpec((1, KV_ROWS, HEAD_DIM), lambda i: (i, 0, 0)),
                  pl.BlockSpec((1, 1, N_KV, HROWS, 1), lambda i: (i, 2, 0, 0, 0)),
                  hspec, hspec],
        out_specs=(hspec, wspec),
        scratch_shapes=[pltpu.VMEM((nk, HEAD_DIM), F32), pltpu.VMEM((nk, HEAD_DIM), F32)],
        compiler_params=_cparams(1),
        name="win_sample",
    )(rel_bias, q8, win, kvw_new, gates8, oc, osel)


def _out_and_ffn(x, ya, yb, z, x_s, ya_s, yb_s, z_s, w_proj_a, w_proj_b, w_out, norm_ffn,
                 w_gate, w_up, w_down, norm_final, *, tm, tm_norm):
    bs = x_s.shape[0]
    half = w_gate.shape[1] // 2
    mix, mix_s = _mix(ya, yb, w_proj_a, w_proj_b, z, ya_s, yb_s, z_s, tm=tm, tn=512)
    h, h_s = _mm_res(mix, w_out, x, mix_s, x_s, tm=tm, tn=512, name="out_proj")
    hn = _rmsnorm(h, norm_ffn, BF16, tm_norm)
    hn_s = _rmsnorm(h_s, norm_ffn, BF16, bs)
    ff, ff_s = _swiglu(hn, w_gate, w_up, hn_s, tm=tm, tn=256)
    y, y_s = _mm_res(ff, w_down, h, ff_s, h_s, tm=tm, tn=256, k_block=0, tk=half, name="ffn_down0")
    y, y_s = _mm_res(ff, w_down, y, ff_s, y_s, tm=tm, tn=256, k_block=1, tk=half, name="ffn_down1")
    return _rmsnorm(y, norm_final, F32, tm_norm), _rmsnorm(y_s, norm_final, F32, bs)


def kernel(x_prompt, x_sample, cache_cmp_kv, cache_sel_kv, cache_win_kv, state_rglru_h, state_conv,
           page_table, rel_bias, norm_mix, w_in, conv_w, conv_b, lru_wa, lru_ba, lru_wi, lru_bi,
           lru_lambda, nsa_w_cmp, w_proj_a, w_proj_b, w_out, norm_ffn, w_gate, w_up, w_down,
           norm_final):
    depth = w_in.shape[0]
    assert depth == 1, "single-layer trunk"
    bp, tp, _ = x_prompt.shape
    bs, ts, _ = x_sample.shape
    assert ts == 1
    n_pool = cache_cmp_kv.shape[1]
    past_len = page_table.shape[1] * PAGE_SIZE
    wc = cache_win_kv.shape[2]

    w_in0 = jnp.swapaxes(w_in[0], 0, 1)
    wexp = jnp.broadcast_to(nsa_w_cmp[0][..., None], (2, N_KV, CMP_BLOCK, HEAD_DIM))
    w8 = jnp.broadcast_to(nsa_w_cmp[0].reshape(KV_ROWS, CMP_BLOCK).T[..., None],
                          (CMP_BLOCK, KV_ROWS, HEAD_DIM))
    layer_w = (conv_w[0], conv_b[0], lru_wa[0], lru_ba[0], lru_wi[0], lru_bi[0], lru_lambda[0])
    tail_w = (w_proj_a[0], w_proj_b[0], w_out[0], norm_ffn[0], w_gate[0], w_up[0], w_down[0],
              norm_final)

    mp = bp * tp
    xp = x_prompt.reshape(mp, D_MODEL)
    xs = x_sample.reshape(bs, D_MODEL)
    per_page = PAGE_SIZE // CMP_BLOCK
    cache_c = cache_cmp_kv.reshape(n_pool, PAGE_SIZE, KV_ROWS, HEAD_DIM)
    xn = _rmsnorm(xp, norm_mix[0], BF16, 256)
    xn_s = _rmsnorm(xs, norm_mix[0], BF16, bs)
    z, zs, kvc = _in_proj_all(xn, w_in0, xn_s, (cache_c, page_table, w8), tm=1024, tn=IN_TN)

    ya, conv_p, h_p = _rglru_prompt(z, bp, tp, *layer_w, tt=128)
    gn_t = z[:, Z_GN:Z_GN + 3 * N_HEADS].reshape(mp, 3, N_KV, HPG).transpose(1, 2, 3, 0)
    bd, bo = _bias_tiles(rel_bias, ATT_T)
    oc, sel = _cmp_prompt(z, rel_bias, wexp, gn_t, bp, tp)
    osel = _sel_prompt(z, rel_bias, bd, bo, gn_t, sel, bp, tp)
    yb = _win_prompt(z, rel_bias, bd, bo, gn_t, oc, osel, bp, tp)

    kv_shape = (1, bp, tp // PAGE_SIZE, PAGE_SIZE, 2, N_KV, HEAD_DIM)
    cmp_p = z[:, OFF_KVC:OFF_KVS].reshape(kv_shape)
    sel_p = z[:, OFF_KVS:OFF_KVW].reshape(kv_shape)
    wlen = min(WINDOW, tp)
    win_p = z[:, OFF_KVW:MAIN_W].reshape(bp, tp, 2, N_KV, HEAD_DIM)[None, :, tp - wlen:]

    buf_t = state_conv[0].transpose(1, 0, 2)
    ya_s, h_s = _rglru_step(zs, buf_t, state_rglru_h[0], *layer_w)

    q8 = jnp.pad(zs[:, OFF_Q:OFF_KVC].reshape(bs, N_KV, HPG, HEAD_DIM),
                 ((0, 0), (0, 0), (0, HROWS - HPG), (0, 0)))
    gates8 = jnp.pad(zs[:, Z_GN:Z_GN + 3 * N_HEADS].reshape(bs, 3, N_KV, HPG),
                     ((0, 0), (0, 0), (0, 0), (0, HROWS - HPG)))[..., None]
    kvc_new = zs[:, OFF_KVC:OFF_KVS].reshape(bs, KV_ROWS, HEAD_DIM)
    kvs_new = zs[:, OFF_KVS:OFF_KVW].reshape(bs, KV_ROWS, HEAD_DIM)
    kvw_new = zs[:, OFF_KVW:MAIN_W].reshape(bs, KV_ROWS, HEAD_DIM)
    cache_s = cache_sel_kv.reshape(n_pool * per_page, CMP_BLOCK, KV_ROWS, HEAD_DIM)
    win = cache_win_kv.reshape(bs, wc, KV_ROWS, HEAD_DIM)

    oc_s, idx = _cmp_sample(rel_bias, q8, kvc, kvc_new, w8, gates8, past_len)
    os_s = _sel_sample(page_table, idx, rel_bias, q8, cache_s, kvs_new, gates8, past_len)
    yb8, win_next = _win_sample(rel_bias, q8, win, kvw_new, gates8, oc_s, os_s)
    yb_s = yb8[:, :, :HPG].reshape(bs, Q_W).astype(BF16)

    y_prompt, y_sample = _out_and_ffn(xp, ya, yb, z, xs, ya_s, yb_s, zs, *tail_w, tm=1024, tm_norm=256)

    row_shape = (1, bs, 1, 2, N_KV, HEAD_DIM)
    cmp_s = kvc_new.reshape(row_shape)
    sel_s = kvs_new.reshape(row_shape)
    win_s = win_next.reshape(1, bs, wc, 2, N_KV, HEAD_DIM)
    conv_s = jnp.concatenate([state_conv[0], zs[:, None, OFF_XR:OFF_GR]], axis=1)[:, 1:][None]

    return (y_prompt.reshape(bp, tp, D_MODEL), y_sample.reshape(bs, ts, D_MODEL),
            cmp_p, sel_p, win_p, h_p.reshape(1, bp, D_RNN), conv_p[None],
            cmp_s, sel_s, win_s, h_s[None], conv_s)
```

```python
import functools
import math

import jax
import jax.numpy as jnp
from jax import lax
from jax.experimental import pallas as pl
from jax.experimental.pallas import tpu as pltpu

F32 = jnp.float32
BF16 = jnp.bfloat16
I32 = jnp.int32

D_MODEL = 4096
D_RNN = 2048
N_RNN_BLOCKS = 16
RNN_BLOCK = D_RNN // N_RNN_BLOCKS
CONV_W = 4
LRU_C = 8.0
N_HEADS = 16
HEAD_DIM = 128
N_KV = 4
HPG = N_HEADS // N_KV
CMP_BLOCK = 64
N_SEL = 16
WINDOW = 512
N_BUCKETS = 32
MAX_DISTANCE = 128
PAGE_SIZE = 128
Q_W = N_HEADS * HEAD_DIM
KV_W = 2 * N_KV * HEAD_DIM
EPS = 1e-6
NEG_INF = -1e30
FORCE = 1e9
SCALE = HEAD_DIM ** -0.5
LOG2E = math.log2(math.e)
DENOM_ROWS = 16

OFF_XR = 0
OFF_GR = D_RNN
OFF_Q = 2 * D_RNN
OFF_KVC = OFF_Q + Q_W
OFF_KVS = OFF_KVC + KV_W
OFF_KVW = OFF_KVS + KV_W
OFF_GN = OFF_KVW + KV_W
OFF_GA = OFF_GN + 3 * N_HEADS
MAIN_W = OFF_GN
IN_TN = 512
Z_GN = MAIN_W
Z_GA = MAIN_W + IN_TN
Z_GB = Z_GA + D_MODEL

ATT_T = 256
HROWS = 8
KV_ROWS = 2 * N_KV
PAGES_PER_STEP = 8
SEL_BLOCKS_PER_STEP = 4
VMEM_LIMIT = 56 * 1024 * 1024


def _cparams(n_axes):
    return pltpu.CompilerParams(dimension_semantics=("arbitrary",) * n_axes,
                                vmem_limit_bytes=VMEM_LIMIT)


def _sigmoid(x):
    return 1.0 / (1.0 + jnp.exp(-x))


def _gelu_tanh(x):
    return 0.5 * x * (1.0 + jnp.tanh(math.sqrt(2.0 / math.pi) * (x + 0.044715 * (x * x * x))))


def _dot(a, b):
    return jnp.dot(a, b, preferred_element_type=F32)


def _dot_nt(a, b):
    return lax.dot_general(a, b, (((1,), (1,)), ((), ())), preferred_element_type=F32)


def _bucket(dist):
    n = jnp.maximum(dist, 0)
    exact = N_BUCKETS // 2
    nf = jnp.maximum(n, 1).astype(F32)
    large = exact + (jnp.log(nf / exact) / math.log(MAX_DISTANCE / exact)
                     * (N_BUCKETS - exact)).astype(I32)
    return jnp.where(n < exact, n, jnp.minimum(large, N_BUCKETS - 1))


def _bias_lookup(bkt, tab_ref, cols):
    outs = [jnp.zeros(bkt.shape, F32) for _ in cols]
    for b in range(N_BUCKETS):
        eq = bkt == b
        outs = [jnp.where(eq, tab_ref[b, c], o) for c, o in zip(cols, outs)]
    return outs


def _block_scores(imp, blk, cur):
    score = jnp.where(blk == 0, FORCE, jnp.where(blk == cur, FORCE,
                                                 jnp.where(blk == cur - 1, FORCE, imp)))
    return jnp.where(blk <= cur, score, -FORCE)


def _ahead(col, score, later):
    return jnp.where(later, jnp.where(col >= score, 1.0, 0.0), jnp.where(col > score, 1.0, 0.0))


def _masked_softmax(s, valid):
    s = jnp.where(valid, s, NEG_INF)
    m = jnp.max(s, axis=-1, keepdims=True)
    e = jnp.where(valid, jnp.exp(s - m), 0.0)
    return e / jnp.maximum(jnp.sum(e, axis=-1, keepdims=True), 1e-30)


def _rmsnorm_kernel(x_ref, g_ref, o_ref):
    x = x_ref[...]
    y = x * lax.rsqrt(jnp.mean(x * x, axis=-1, keepdims=True) + EPS)
    o_ref[...] = (y * g_ref[...]).astype(o_ref.dtype)


def _rmsnorm(x, g, out_dtype, tm):
    m, d = x.shape
    return pl.pallas_call(
        _rmsnorm_kernel,
        out_shape=jax.ShapeDtypeStruct((m, d), out_dtype),
        grid=(m // tm,),
        in_specs=[pl.BlockSpec((tm, d), lambda i: (i, 0)),
                  pl.BlockSpec((1, d), lambda i: (0, 0))],
        out_specs=pl.BlockSpec((tm, d), lambda i: (i, 0)),
        compiler_params=_cparams(1),
        name="rmsnorm",
    )(x, g.reshape(1, d))


def _rider_store(os_ref, compute):
    last = pl.program_id(0) == pl.num_programs(0) - 1

    @pl.when(last)
    def _():
        os_ref[0, 0] = compute().astype(os_ref.dtype)

    @pl.when(jnp.logical_not(last))
    def _():
        os_ref[0, 0] = jnp.zeros(os_ref.shape[2:], os_ref.dtype)


def _rider_spec(rows, tn):
    return pl.BlockSpec((1, 1, rows, tn), lambda i, j, *_: (i, j, 0, 0))


def _rider_rows(os):
    _, nj, rows, tn = os.shape
    return os[-1].transpose(1, 0, 2).reshape(rows, nj * tn)


def _mm_res_kernel(a_ref, w_ref, r_ref, as_ref, rs_ref, o_ref, os_ref):
    w = w_ref[...].astype(BF16)
    o_ref[...] = r_ref[...] + _dot(a_ref[...], w)
    _rider_store(os_ref, lambda: rs_ref[...] + _dot(as_ref[...], w))


def _mm_res(a, w, res, a_s, res_s, *, tm, tn, k_block=0, tk=None, name="mm"):
    m = a.shape[0]
    bs = a_s.shape[0]
    n_cols = w.shape[1]
    tk = a.shape[1] if tk is None else tk
    grid = (m // tm, n_cols // tn)
    out, out_s = pl.pallas_call(
        _mm_res_kernel,
        out_shape=(jax.ShapeDtypeStruct((m, n_cols), F32),
                   jax.ShapeDtypeStruct(grid + (bs, tn), F32)),
        grid=grid,
        in_specs=[pl.BlockSpec((tm, tk), lambda i, j: (i, k_block)),
                  pl.BlockSpec((tk, tn), lambda i, j: (k_block, j)),
                  pl.BlockSpec((tm, tn), lambda i, j: (i, j)),
                  pl.BlockSpec((bs, tk), lambda i, j: (0, k_block)),
                  pl.BlockSpec((bs, tn), lambda i, j: (0, j))],
        out_specs=(pl.BlockSpec((tm, tn), lambda i, j: (i, j)), _rider_spec(bs, tn)),
        compiler_params=_cparams(2),
        name=name,
    )(a, w, res, a_s, res_s)
    return out, _rider_rows(out_s)


def _compress_pages(page_refs, w_ref, o_ref):
    per_page = PAGE_SIZE // CMP_BLOCK
    w = w_ref[...]
    for k, page_ref in enumerate(page_refs):
        x = page_ref[0].reshape(per_page, CMP_BLOCK, KV_ROWS, HEAD_DIM)
        o_ref[0, per_page * k:per_page * (k + 1)] = jnp.sum(x * w[None], axis=1)


def _in_proj_pages_kernel(pt_ref, a_ref, wt_ref, as_ref, *refs):
    page_refs, w8_ref, o_ref, os_ref, kvc_ref = refs[:-4], refs[-4], refs[-3], refs[-2], refs[-1]
    w = wt_ref[...].astype(BF16)
    o_ref[...] = _dot_nt(a_ref[...], w)
    _rider_store(os_ref, lambda: _dot_nt(as_ref[...], w))
    _compress_pages(page_refs, w8_ref, kvc_ref)


def _in_proj_all(a, wt, a_s, pages, *, tm, tn):
    m, k = a.shape
    bs = a_s.shape[0]
    assert MAIN_W % tn == 0 and Z_GN == MAIN_W and Z_GA == MAIN_W + tn and 3 * N_HEADS <= tn
    n_gate_blocks = 2 * D_MODEL // tn
    nb = MAIN_W // tn + 1 + n_gate_blocks
    assert OFF_GA + n_gate_blocks * tn == wt.shape[0]

    def w_index(i, j, *_):
        row = jnp.where(j <= MAIN_W // tn, j * tn, OFF_GA + (j - MAIN_W // tn - 1) * tn)
        return (pl.multiple_of(row, 8), 0)

    grid = (m // tm, nb)
    in_specs = [pl.BlockSpec((tm, k), lambda i, j, *_: (i, 0)),
                pl.BlockSpec((pl.Element(tn), pl.Element(k)), w_index),
                pl.BlockSpec((bs, k), lambda i, j, *_: (0, 0))]
    out_spec = pl.BlockSpec((tm, tn), lambda i, j, *_: (i, j))
    out_shape = jax.ShapeDtypeStruct((m, nb * tn), F32)

    cache, page_table, w8 = pages
    b, n_pages = page_table.shape
    per_page = PAGE_SIZE // CMP_BLOCK
    pps = PAGES_PER_STEP
    chunks = n_pages // pps
    assert grid[0] * grid[1] >= b * chunks

    def chunk(i, j):
        t = jnp.minimum(i * nb + j, b * chunks - 1)
        return t // chunks, t % chunks

    def page_index(k):
        def index(i, j, pt):
            seq, c = chunk(i, j)
            return (pt[seq, c * pps + k], 0, 0, 0)
        return index

    def kvc_index(i, j, pt):
        seq, c = chunk(i, j)
        return (seq, c, 0, 0)

    grid_spec = pltpu.PrefetchScalarGridSpec(
        num_scalar_prefetch=1,
        grid=grid,
        in_specs=in_specs
        + [pl.BlockSpec((1, PAGE_SIZE, KV_ROWS, HEAD_DIM), page_index(k)) for k in range(pps)]
        + [pl.BlockSpec((CMP_BLOCK, KV_ROWS, HEAD_DIM), lambda i, j, pt: (0, 0, 0))],
        out_specs=(out_spec, _rider_spec(bs, tn),
                   pl.BlockSpec((1, pps * per_page, KV_ROWS, HEAD_DIM), kvc_index)),
    )
    z, z_s, kvc = pl.pallas_call(
        _in_proj_pages_kernel,
        out_shape=(out_shape,
                   jax.ShapeDtypeStruct(grid + (bs, tn), F32),
                   jax.ShapeDtypeStruct((b, n_pages * per_page, KV_ROWS, HEAD_DIM), F32)),
        grid_spec=grid_spec,
        compiler_params=_cparams(2),
        name="in_proj_pages",
    )(page_table, a, wt, a_s, *([cache] * pps), w8)
    return z, _rider_rows(z_s), kvc


def _swiglu_kernel(a_ref, wg_ref, wu_ref, as_ref, o_ref, os_ref):
    wg = wg_ref[...].astype(BF16)
    wu = wu_ref[...].astype(BF16)

    def swiglu(a):
        g = _dot(a, wg)
        return (g * _sigmoid(g)) * _dot(a, wu)

    o_ref[...] = swiglu(a_ref[...]).astype(o_ref.dtype)
    _rider_store(os_ref, lambda: swiglu(as_ref[...]))


def _swiglu(a, wg, wu, a_s, *, tm, tn):
    m, k = a.shape
    bs = a_s.shape[0]
    n = wg.shape[1]
    grid = (m // tm, n // tn)
    out, out_s = pl.pallas_call(
        _swiglu_kernel,
        out_shape=(jax.ShapeDtypeStruct((m, n), BF16), jax.ShapeDtypeStruct(grid + (bs, tn), BF16)),
        grid=grid,
        in_specs=[pl.BlockSpec((tm, k), lambda i, j: (i, 0), pipeline_mode=pl.Buffered(1)),
                  pl.BlockSpec((k, tn), lambda i, j: (0, j)),
                  pl.BlockSpec((k, tn), lambda i, j: (0, j)),
                  pl.BlockSpec((bs, k), lambda i, j: (0, 0))],
        out_specs=(pl.BlockSpec((tm, tn), lambda i, j: (i, j)), _rider_spec(bs, tn)),
        compiler_params=_cparams(2),
        name="ffn_gate_up",
    )(a, wg, wu, a_s)
    return out, _rider_rows(out_s)


def _mix_kernel(ya_ref, yb_ref, wa_ref, wb_ref, ga_ref, gb_ref, yas_ref, ybs_ref, gas_ref, gbs_ref,
                o_ref, os_ref):
    wa = wa_ref[...].astype(BF16)
    wb = wb_ref[...].astype(BF16)

    def merge(ya, yb, ga, gb):
        return _sigmoid(ga) * _dot(ya, wa) + _sigmoid(gb) * _dot(yb, wb)

    o_ref[...] = merge(ya_ref[...], yb_ref[...], ga_ref[...], gb_ref[...]).astype(o_ref.dtype)
    _rider_store(os_ref, lambda: merge(yas_ref[...], ybs_ref[...], gas_ref[...], gbs_ref[...]))


def _mix(ya, yb, wa, wb, z, ya_s, yb_s, z_s, *, tm, tn):
    m, k = ya.shape
    bs = ya_s.shape[0]
    n = wa.shape[1]
    assert Z_GA % tn == 0 and Z_GB % tn == 0
    grid = (m // tm, n // tn)
    gate = lambda off: pl.BlockSpec((tm, tn), lambda i, j: (i, j + off // tn))
    gate_s = lambda off: pl.BlockSpec((bs, tn), lambda i, j: (0, j + off // tn))
    out, out_s = pl.pallas_call(
        _mix_kernel,
        out_shape=(jax.ShapeDtypeStruct((m, n), BF16), jax.ShapeDtypeStruct(grid + (bs, tn), BF16)),
        grid=grid,
        in_specs=[pl.BlockSpec((tm, k), lambda i, j: (i, 0)),
                  pl.BlockSpec((tm, k), lambda i, j: (i, 0)),
                  pl.BlockSpec((k, tn), lambda i, j: (0, j)),
                  pl.BlockSpec((k, tn), lambda i, j: (0, j)),
                  gate(Z_GA), gate(Z_GB),
                  pl.BlockSpec((bs, k), lambda i, j: (0, 0)),
                  pl.BlockSpec((bs, k), lambda i, j: (0, 0)),
                  gate_s(Z_GA), gate_s(Z_GB)],
        out_specs=(pl.BlockSpec((tm, tn), lambda i, j: (i, j)), _rider_spec(bs, tn)),
        compiler_params=_cparams(2),
        name="branch_merge",
    )(ya, yb, wa, wb, z, z, ya_s, yb_s, z_s, z_s)
    return out, _rider_rows(out_s)


def _lru_gates(xc, wa_ref, ba_ref, wi_ref, bi_ref, lam_ref, a_ref, u_ref):
    for n in range(N_RNN_BLOCKS):
        sl = slice(n * RNN_BLOCK, (n + 1) * RNN_BLOCK)
        xn = xc[:, sl]
        xb = xn.astype(BF16)
        r = _sigmoid(_dot(xb, wa_ref[n].astype(BF16)) + ba_ref[:, sl])
        i = _sigmoid(_dot(xb, wi_ref[n].astype(BF16)) + bi_ref[:, sl])
        neg_lam = -lam_ref[:, sl]
        softplus = jnp.maximum(neg_lam, 0.0) + jnp.log1p(jnp.exp(-jnp.abs(neg_lam)))
        log_a = (-LRU_C * r) * softplus
        a = jnp.exp(log_a)
        a_ref[:, sl] = a
        u_ref[:, sl] = jnp.sqrt(-jnp.tanh(log_a) * (a * a + 1.0)) * (i * xn)


def _rglru_prompt_kernel(xr_ref, gr_ref, cw_ref, cb_ref, wa_ref, ba_ref, wi_ref, bi_ref, lam_ref,
                         ya_ref, conv_ref, hl_ref, xp_ref, a_ref, u_ref, h_ref, carry_ref):
    ti = pl.program_id(1)
    tt, c = xr_ref.shape

    @pl.when(ti == 0)
    def _():
        xp_ref[0:8, :] = jnp.zeros((8, c), F32)
        carry_ref[...] = jnp.zeros((1, c), F32)

    x = xr_ref[...]
    xp_ref[8:8 + tt, :] = x
    xc = cb_ref[...]
    for k in range(CONV_W):
        xc = xc + cw_ref[k:k + 1, :] * xp_ref[5 + k:5 + k + tt, :]
    xp_ref[0:8, :] = xp_ref[tt:tt + 8, :]

    _lru_gates(xc, wa_ref, ba_ref, wi_ref, bi_ref, lam_ref, a_ref, u_ref)

    row = lax.broadcasted_iota(I32, (8, c), 0)

    def chunk(ci, carry):
        off = pl.multiple_of(ci * 8, 8)
        a = a_ref[pl.ds(off, 8), :]
        b = u_ref[pl.ds(off, 8), :]
        for s in (1, 2, 4):
            keep = row >= s
            b = jnp.where(keep, a * pltpu.roll(b, s, 0) + b, b)
            a = jnp.where(keep, a * pltpu.roll(a, s, 0), a)
        h = a * carry + b
        h_ref[pl.ds(off, 8), :] = h
        return h[7:8, :]

    carry = lax.fori_loop(0, tt // 8, chunk, carry_ref[...])
    carry_ref[...] = carry
    ya_ref[...] = (h_ref[...] * _gelu_tanh(gr_ref[...])).astype(ya_ref.dtype)

    @pl.when(ti == pl.num_programs(1) - 1)
    def _():
        conv_ref[0] = x[tt - (CONV_W - 1):, :]
        hl_ref[0] = carry


def _rglru_prompt(z, batch, seq, conv_w, conv_b, lru_wa, lru_ba, lru_wi, lru_bi, lam, *, tt):
    nt = seq // tt
    c = D_RNN
    row = lambda b, t: (b * nt + t, 0)
    full2 = lambda b, t: (0, 0)
    full3 = lambda b, t: (0, 0, 0)
    return pl.pallas_call(
        _rglru_prompt_kernel,
        out_shape=(jax.ShapeDtypeStruct((batch * seq, c), BF16),
                   jax.ShapeDtypeStruct((batch, CONV_W - 1, c), F32),
                   jax.ShapeDtypeStruct((batch, 1, c), F32)),
        grid=(batch, nt),
        in_specs=[pl.BlockSpec((tt, c), row),
                  pl.BlockSpec((tt, c), lambda b, t: (b * nt + t, 1)),
                  pl.BlockSpec((CONV_W, c), full2),
                  pl.BlockSpec((1, c), full2),
                  pl.BlockSpec((N_RNN_BLOCKS, RNN_BLOCK, RNN_BLOCK), full3),
                  pl.BlockSpec((1, c), full2),
                  pl.BlockSpec((N_RNN_BLOCKS, RNN_BLOCK, RNN_BLOCK), full3),
                  pl.BlockSpec((1, c), full2),
                  pl.BlockSpec((1, c), full2)],
        out_specs=(pl.BlockSpec((tt, c), row),
                   pl.BlockSpec((1, CONV_W - 1, c), lambda b, t: (b, 0, 0)),
                   pl.BlockSpec((1, 1, c), lambda b, t: (b, 0, 0))),
        scratch_shapes=[pltpu.VMEM((tt + 8, c), F32), pltpu.VMEM((tt, c), F32),
                        pltpu.VMEM((tt, c), F32), pltpu.VMEM((tt, c), F32),
                        pltpu.VMEM((1, c), F32)],
        compiler_params=_cparams(2),
        name="rglru_prompt",
    )(z, z, conv_w, conv_b.reshape(1, c), lru_wa, lru_ba.reshape(1, c), lru_wi,
      lru_bi.reshape(1, c), lam.reshape(1, c))


def _rglru_step_kernel(xr_ref, gr_ref, buf_ref, h0_ref, cw_ref, cb_ref, wa_ref, ba_ref, wi_ref,
                       bi_ref, lam_ref, ya_ref, hn_ref, a_ref, u_ref):
    x = xr_ref[...]
    xc = cb_ref[...]
    for k in range(CONV_W - 1):
        xc = xc + cw_ref[k:k + 1, :] * buf_ref[k]
    xc = xc + cw_ref[CONV_W - 1:CONV_W, :] * x
    _lru_gates(xc, wa_ref, ba_ref, wi_ref, bi_ref, lam_ref, a_ref, u_ref)
    h = a_ref[...] * h0_ref[...] + u_ref[...]
    hn_ref[...] = h
    ya_ref[...] = (h * _gelu_tanh(gr_ref[...])).astype(ya_ref.dtype)


def _rglru_step(z, buf_t, h0, conv_w, conv_b, lru_wa, lru_ba, lru_wi, lru_bi, lam):
    b, c = h0.shape
    full2 = lambda i: (0, 0)
    full3 = lambda i: (0, 0, 0)
    return pl.pallas_call(
        _rglru_step_kernel,
        out_shape=(jax.ShapeDtypeStruct((b, c), BF16), jax.ShapeDtypeStruct((b, c), F32)),
        grid=(1,),
        in_specs=[pl.BlockSpec((b, c), full2),
                  pl.BlockSpec((b, c), lambda i: (0, 1)),
                  pl.BlockSpec((CONV_W - 1, b, c), full3),
                  pl.BlockSpec((b, c), full2),
                  pl.BlockSpec((CONV_W, c), full2),
                  pl.BlockSpec((1, c), full2),
                  pl.BlockSpec((N_RNN_BLOCKS, RNN_BLOCK, RNN_BLOCK), full3),
                  pl.BlockSpec((1, c), full2),
                  pl.BlockSpec((N_RNN_BLOCKS, RNN_BLOCK, RNN_BLOCK), full3),
                  pl.BlockSpec((1, c), full2),
                  pl.BlockSpec((1, c), full2)],
        out_specs=(pl.BlockSpec((b, c), full2), pl.BlockSpec((b, c), full2)),
        scratch_shapes=[pltpu.VMEM((b, c), F32), pltpu.VMEM((b, c), F32)],
        compiler_params=_cparams(1),
        name="rglru_step",
    )(z, z, buf_t, h0, conv_w, conv_b.reshape(1, c), lru_wa, lru_ba.reshape(1, c), lru_wi,
      lru_bi.reshape(1, c), lam.reshape(1, c))


def _bias_tiles_kernel(tab_ref, bd_ref, bo_ref):
    h = pl.program_id(0)
    t = bd_ref.shape[1]
    d0 = lax.broadcasted_iota(I32, (t, t), 1) - lax.broadcasted_iota(I32, (t, t), 0)
    bd_ref[0] = jnp.where(d0 >= 0, _bias_lookup(_bucket(d0), tab_ref, [h])[0] * LOG2E, NEG_INF)
    bo_ref[0] = _bias_lookup(_bucket(d0 + t), tab_ref, [h])[0] * LOG2E


def _bias_tiles(rel_bias, t):
    shp = jax.ShapeDtypeStruct((N_HEADS, t, t), F32)
    spec = pl.BlockSpec((1, t, t), lambda h: (h, 0, 0))
    return pl.pallas_call(
        _bias_tiles_kernel,
        out_shape=(shp, shp),
        grid=(N_HEADS,),
        in_specs=[pl.BlockSpec(memory_space=pltpu.SMEM)],
        out_specs=(spec, spec),
        compiler_params=_cparams(1),
        name="bias_tiles",
    )(rel_bias)


def _cmp_prompt_kernel(tab_ref, q_ref, kc_ref, vc_ref, wk_ref, wv_ref, gate_ref,
                       oc_ref, sel_ref, kcmp_ref, vcmp_ref, vcmp_t_ref):
    g = pl.program_id(1)
    qi = pl.program_id(2)
    tq = q_ref.shape[0]
    seq = kc_ref.shape[0]
    nblk = seq // CMP_BLOCK

    @pl.when(qi == 0)
    def _():
        k3 = kc_ref[...].reshape(nblk, CMP_BLOCK, HEAD_DIM)
        v3 = vc_ref[...].reshape(nblk, CMP_BLOCK, HEAD_DIM)
        kcmp_ref[...] = jnp.sum(k3 * wk_ref[0, 0][None], axis=1).astype(BF16)
        vcmp_ref[...] = jnp.zeros(vcmp_ref.shape, F32)
        vcmp_ref[0:nblk, :] = jnp.sum(v3 * wv_ref[0, 0][None], axis=1)
        vcmp_t_ref[...] = jnp.transpose(vcmp_ref[...]).astype(BF16)

    tpos = qi * tq + lax.broadcasted_iota(I32, (nblk, tq), 1)
    blk = lax.broadcasted_iota(I32, (nblk, tq), 0)
    dist = tpos - (blk * CMP_BLOCK + (CMP_BLOCK - 1))
    valid = dist >= 0
    biases = _bias_lookup(_bucket(dist), tab_ref, [g * HPG + h for h in range(HPG)])
    kb = kcmp_ref[...]
    vt = vcmp_t_ref[:, 0:nblk]
    gates = _sigmoid(gate_ref[0, 0])
    heads = range(HPG)
    cols = [slice(h * HEAD_DIM, (h + 1) * HEAD_DIM) for h in heads]
    scores = [_dot_nt(kb, q_ref[:, cols[h]].astype(BF16)) for h in heads]
    scores = [jnp.where(valid, s * SCALE + b, NEG_INF) for s, b in zip(scores, biases)]
    e = [jnp.where(valid, jnp.exp(s - jnp.max(s, axis=0, keepdims=True)), 0.0) for s in scores]
    probs = [x / jnp.maximum(jnp.sum(x, axis=0, keepdims=True), 1e-30) for x in e]
    imp = probs[0]
    for h in range(1, HPG):
        imp = imp + probs[h]
    outs = [_dot(vt, p.astype(BF16)) for p in probs]
    for h in heads:
        oc_ref[:, cols[h]] = jnp.transpose(outs[h] * gates[h:h + 1, :])

    cur = jnp.right_shift(tpos, 6)
    score = _block_scores(imp, blk, cur)
    rank = jnp.zeros((nblk, tq), F32)
    for j in range(nblk):
        rank = rank + _ahead(score[j:j + 1, :], score, blk > j)
    sel_ref[0, 0] = jnp.where(rank < float(min(N_SEL, nblk)), 0.0, NEG_INF)


def _cmp_prompt(z, rel_bias, wexp, gn_t, batch, seq):
    tq = ATT_T
    nq = seq // tq
    nblk = seq // CMP_BLOCK
    m = batch * seq
    qcol = OFF_Q // (HPG * HEAD_DIM)
    kcol = OFF_KVC // HEAD_DIM
    return pl.pallas_call(
        _cmp_prompt_kernel,
        out_shape=(jax.ShapeDtypeStruct((m, Q_W), F32),
                   jax.ShapeDtypeStruct((batch, N_KV, nblk, seq), F32)),
        grid=(batch, N_KV, nq),
        in_specs=[pl.BlockSpec(memory_space=pltpu.SMEM),
                  pl.BlockSpec((tq, HPG * HEAD_DIM), lambda b, g, i: (b * nq + i, qcol + g)),
                  pl.BlockSpec((seq, HEAD_DIM), lambda b, g, i: (b, kcol + g)),
                  pl.BlockSpec((seq, HEAD_DIM), lambda b, g, i: (b, kcol + N_KV + g)),
                  pl.BlockSpec((1, 1, CMP_BLOCK, HEAD_DIM), lambda b, g, i: (0, g, 0, 0)),
                  pl.BlockSpec((1, 1, CMP_BLOCK, HEAD_DIM), lambda b, g, i: (1, g, 0, 0)),
                  pl.BlockSpec((1, 1, HPG, tq), lambda b, g, i: (0, g, 0, b * nq + i))],
        out_specs=(pl.BlockSpec((tq, HPG * HEAD_DIM), lambda b, g, i: (b * nq + i, g)),
                   pl.BlockSpec((1, 1, nblk, tq), lambda b, g, i: (b, g, 0, i))),
        scratch_shapes=[pltpu.VMEM((nblk, HEAD_DIM), BF16), pltpu.VMEM((HEAD_DIM, HEAD_DIM), F32),
                        pltpu.VMEM((HEAD_DIM, HEAD_DIM), BF16)],
        compiler_params=_cparams(3),
        name="cmp_prompt",
    )(rel_bias, z, z, z, wexp, wexp, gn_t)


def _attn_init(m_ref, l_ref, acc_ref):
    m_ref[...] = jnp.full(m_ref.shape, NEG_INF, F32)
    l_ref[...] = jnp.zeros(l_ref.shape, F32)
    acc_ref[...] = jnp.zeros(acc_ref.shape, F32)


def _attn_update(h, qh, k, v, bias, valid, m_ref, l_ref, acc_ref):
    s = _dot_nt(qh, k) * SCALE + bias
    if valid is not None:
        s = jnp.where(valid, s, NEG_INF)
    m_old = m_ref[h]
    m_new = jnp.maximum(m_old, jnp.max(s, axis=-1, keepdims=True))
    alpha = jnp.exp(m_old - m_new)
    e = jnp.exp(s - m_new)
    if valid is not None:
        e = jnp.where(valid, e, 0.0)
    l_ref[h] = alpha * l_ref[h] + jnp.sum(e, axis=-1, keepdims=True)
    acc_ref[h] = alpha * acc_ref[h] + _dot(e.astype(BF16), v)
    m_ref[h] = m_new


def _attn_prologue(qi, q_ref, k_ref, v_ref, kb_ref, vt_ref, qb_ref, m_ref, acc_ref):
    n_tiles, t = kb_ref.shape[0], kb_ref.shape[1]

    @pl.when(qi == 0)
    def _():
        for kj in range(n_tiles):
            rows = slice(kj * t, (kj + 1) * t)
            kb_ref[kj] = k_ref[rows, :].astype(BF16)
            vt_ref[kj, 0:HEAD_DIM] = jnp.transpose(v_ref[rows, :]).astype(BF16)
            vt_ref[kj, HEAD_DIM:] = jnp.ones((vt_ref.shape[1] - HEAD_DIM, t), BF16)

    for h in range(HPG):
        qb_ref[h] = q_ref[:, h * HEAD_DIM:(h + 1) * HEAD_DIM].astype(BF16)
    m_ref[...] = jnp.full(m_ref.shape, NEG_INF, F32)
    acc_ref[...] = jnp.zeros(acc_ref.shape, F32)


def _attn_update_t(tiles, qb_ref, m_ref, acc_ref):
    heads = range(HPG)
    scores = [[_dot_nt(k, qb_ref[h]) for h in heads] for k, _, _, _ in tiles]
    scores = [[s * (SCALE * LOG2E) + b for s, b in zip(ss, tile[2])] for ss, tile in zip(scores, tiles)]
    scores = [ss if tile[3] is None else [jnp.where(tile[3], s, NEG_INF) for s in ss]
              for ss, tile in zip(scores, tiles)]
    m_old = [m_ref[h] for h in heads]
    m_new = list(m_old)
    for ss in scores:
        m_new = [jnp.maximum(mn, jnp.max(s, axis=0, keepdims=True)) for mn, s in zip(m_new, ss)]
    alpha = [jnp.exp2(mo - mn) for mo, mn in zip(m_old, m_new)]
    e = [[jnp.exp2(s - mn).astype(BF16) for s, mn in zip(ss, m_new)] for ss in scores]
    for h in heads:
        m_ref[h] = m_new[h]
    pv = [[_dot(tile[1], es[h]) for h in heads] for es, tile in zip(e, tiles)]
    for h in heads:
        total = pv[0][h]
        for t in range(1, len(tiles)):
            total = total + pv[t][h]
        acc_ref[h] = alpha[h] * acc_ref[h] + total


def _attn_output_t(h, gates, acc_ref):
    denom = jnp.maximum(acc_ref[h, HEAD_DIM:HEAD_DIM + 1, :], 1e-30)
    o_t = acc_ref[h, 0:HEAD_DIM, :] / denom * gates[h:h + 1, :]
    return jnp.transpose(o_t)


def _sel_prompt_kernel(tab_ref, q_ref, k_ref, v_ref, bd_ref, bo_ref, gate_ref, sel_ref,
                       os_ref, kb_ref, vt_ref, qb_ref, m_ref, acc_ref):
    g = pl.program_id(1)
    qi = pl.program_id(2)
    t = q_ref.shape[0]
    per_tile = t // CMP_BLOCK
    _attn_prologue(qi, q_ref, k_ref, v_ref, kb_ref, vt_ref, qb_ref, m_ref, acc_ref)

    def tile(kj, kind):
        rows = [jnp.broadcast_to(sel_ref[0, 0, pl.ds(kj * per_tile + i, 1), :], (CMP_BLOCK, t))
                for i in range(per_tile)]
        mask = jnp.concatenate(rows, axis=0)
        if kind == "far":
            biases = [mask + tab_ref[N_BUCKETS - 1, g * HPG + h] * LOG2E for h in range(HPG)]
        elif kind == "off":
            biases = [mask + bo_ref[h] for h in range(HPG)]
        else:
            biases = [mask + bd_ref[h] for h in range(HPG)]
        return (kb_ref[kj], vt_ref[kj], biases, None)

    def update(*tiles):
        _attn_update_t(tiles, qb_ref, m_ref, acc_ref)

    n_far = jnp.maximum(qi - 1, 0)

    def pair_body(p, carry):
        update(tile(2 * p, "far"), tile(2 * p + 1, "far"))
        return carry

    lax.fori_loop(0, n_far // 2, pair_body, 0)

    @pl.when(qi == 0)
    def _():
        update(tile(qi, "diag"))

    @pl.when(jnp.logical_and(qi >= 1, n_far % 2 == 0))
    def _():
        update(tile(qi - 1, "off"), tile(qi, "diag"))

    @pl.when(n_far % 2 == 1)
    def _():
        update(tile(qi - 2, "far"), tile(qi - 1, "off"), tile(qi, "diag"))

    gates = _sigmoid(gate_ref[0, 0])
    for h in range(HPG):
        os_ref[:, h * HEAD_DIM:(h + 1) * HEAD_DIM] = _attn_output_t(h, gates, acc_ref)


def _win_prompt_kernel(tab_ref, q_ref, k_ref, v_ref, bd_ref, bo_ref, gate_ref, oc_ref, os_ref,
                       yb_ref, kb_ref, vt_ref, qb_ref, m_ref, acc_ref):
    g = pl.program_id(1)
    qi = pl.program_id(2)
    t = q_ref.shape[0]
    _attn_prologue(qi, q_ref, k_ref, v_ref, kb_ref, vt_ref, qb_ref, m_ref, acc_ref)
    key = lax.broadcasted_iota(I32, (t, t), 0)
    qry = lax.broadcasted_iota(I32, (t, t), 1)

    def tile(kj, kind):
        if kind == "far":
            biases = [tab_ref[N_BUCKETS - 1, g * HPG + h] * LOG2E for h in range(HPG)]
            valid = qry < key
        elif kind == "off":
            biases, valid = [bo_ref[h] for h in range(HPG)], None
        else:
            biases, valid = [bd_ref[h] for h in range(HPG)], None
        return (kb_ref[kj], vt_ref[kj], biases, valid)

    def update(*tiles):
        _attn_update_t(tiles, qb_ref, m_ref, acc_ref)

    @pl.when(qi == 0)
    def _():
        update(tile(qi, "diag"))

    @pl.when(qi == 1)
    def _():
        update(tile(qi, "diag"), tile(qi - 1, "off"))

    @pl.when(qi >= 2)
    def _():
        update(tile(qi, "diag"), tile(qi - 1, "off"), tile(qi - 2, "far"))

    gates = _sigmoid(gate_ref[0, 0])
    for h in range(HPG):
        sl = slice(h * HEAD_DIM, (h + 1) * HEAD_DIM)
        o = _attn_output_t(h, gates, acc_ref)
        yb_ref[:, sl] = (oc_ref[:, sl] + os_ref[:, sl] + o).astype(yb_ref.dtype)


def _attn_prompt_specs(batch, seq, kv_off, gate_idx):
    t = ATT_T
    nq = seq // t
    qcol = OFF_Q // (HPG * HEAD_DIM)
    kcol = kv_off // HEAD_DIM
    rowblk = lambda b, g, i: (b * nq + i, g)
    specs = [pl.BlockSpec(memory_space=pltpu.SMEM),
             pl.BlockSpec((t, HPG * HEAD_DIM), lambda b, g, i: (b * nq + i, qcol + g)),
             pl.BlockSpec((seq, HEAD_DIM), lambda b, g, i: (b, kcol + g)),
             pl.BlockSpec((seq, HEAD_DIM), lambda b, g, i: (b, kcol + N_KV + g)),
             pl.BlockSpec((HPG, t, t), lambda b, g, i: (g, 0, 0)),
             pl.BlockSpec((HPG, t, t), lambda b, g, i: (g, 0, 0)),
             pl.BlockSpec((1, 1, HPG, t), lambda b, g, i: (gate_idx, g, 0, b * nq + i))]
    scratch = [pltpu.VMEM((seq // t, t, HEAD_DIM), BF16),
               pltpu.VMEM((seq // t, HEAD_DIM + DENOM_ROWS, t), BF16),
               pltpu.VMEM((HPG, t, HEAD_DIM), BF16),
               pltpu.VMEM((HPG, 1, t), F32),
               pltpu.VMEM((HPG, HEAD_DIM + DENOM_ROWS, t), F32)]
    return specs, scratch, pl.BlockSpec((t, HPG * HEAD_DIM), rowblk), (batch, N_KV, nq)


def _sel_prompt(z, rel_bias, bd, bo, gn_t, sel, batch, seq):
    specs, scratch, out_spec, grid = _attn_prompt_specs(batch, seq, OFF_KVS, 1)
    t = ATT_T
    nblk = seq // CMP_BLOCK
    specs += [pl.BlockSpec((1, 1, nblk, t), lambda b, g, i: (b, g, 0, i))]
    return pl.pallas_call(
        _sel_prompt_kernel,
        out_shape=jax.ShapeDtypeStruct((batch * seq, Q_W), F32),
        grid=grid, in_specs=specs, out_specs=out_spec, scratch_shapes=scratch,
        compiler_params=_cparams(3), name="sel_prompt",
    )(rel_bias, z, z, z, bd, bo, gn_t, sel)


def _win_prompt(z, rel_bias, bd, bo, gn_t, oc, osel, batch, seq):
    specs, scratch, out_spec, grid = _attn_prompt_specs(batch, seq, OFF_KVW, 2)
    specs += [out_spec, out_spec]
    return pl.pallas_call(
        _win_prompt_kernel,
        out_shape=jax.ShapeDtypeStruct((batch * seq, Q_W), BF16),
        grid=grid, in_specs=specs, out_specs=out_spec, scratch_shapes=scratch,
        compiler_params=_cparams(3), name="win_prompt",
    )(rel_bias, z, z, z, bd, bo, gn_t, oc, osel)


def _cmp_sample_kernel(tab_ref, q_ref, kvc_ref, new_ref, w_ref, gate_ref,
                       oc_ref, idx_ref, kall_ref, vall_ref, imp_ref, *, q_pos):
    nblk = kvc_ref.shape[1]
    nk = kall_ref.shape[0]
    blk = lax.broadcasted_iota(I32, (1, nk), 1)
    dist = q_pos - (blk * CMP_BLOCK + (CMP_BLOCK - 1))
    valid = dist >= 0
    bkt = _bucket(dist)
    new_c = new_ref[0] * w_ref[0]
    kall_ref[...] = jnp.zeros(kall_ref.shape, F32)
    vall_ref[...] = jnp.zeros(vall_ref.shape, F32)
    for g in range(N_KV):
        kall_ref[0:nblk, :] = kvc_ref[0, :, g, :]
        vall_ref[0:nblk, :] = kvc_ref[0, :, N_KV + g, :]
        kall_ref[nblk:nblk + 1, :] = new_c[g:g + 1, :]
        vall_ref[nblk:nblk + 1, :] = new_c[N_KV + g:N_KV + g + 1, :]
        kb = kall_ref[...].astype(BF16)
        vb = vall_ref[...].astype(BF16)
        biases = _bias_lookup(bkt, tab_ref, [g * HPG + h for h in range(HPG)])
        rows = lax.broadcasted_iota(I32, (HROWS, nk), 0)
        bias = jnp.zeros((HROWS, nk), F32)
        for h in range(HPG):
            bias = jnp.where(rows == h, biases[h], bias)
        s = _dot_nt(q_ref[0, g].astype(BF16), kb) * SCALE + bias
        p = _masked_softmax(s, valid)
        gate = _sigmoid(gate_ref[0, 0, g])
        oc_ref[0, g] = gate * _dot(p.astype(BF16), vb)
        head_row = jnp.where(rows < HPG, p, 0.0)
        imp_ref[g:g + 1, :] = jnp.sum(head_row, axis=0, keepdims=True)

    imp = imp_ref[0:N_KV, :]
    blk4 = lax.broadcasted_iota(I32, (N_KV, nk), 1)
    cur = q_pos // CMP_BLOCK
    score = _block_scores(imp, blk4, cur)
    rank = jnp.zeros((N_KV, nk), F32)
    for j in range(nblk + 1):
        col = score[:, j:j + 1]
        rank = rank + _ahead(col, score, blk4 > j)
    rank = jnp.where(blk4 <= nblk, rank, float(nk))
    blkf = blk4.astype(F32)
    slot = lax.broadcasted_iota(I32, (N_KV, N_SEL), 1)
    idx = jnp.zeros((N_KV, N_SEL), F32)
    for r in range(N_SEL):
        pick = jnp.sum(jnp.where(rank == float(r), blkf, 0.0), axis=-1, keepdims=True)
        idx = jnp.where(slot == r, pick, idx)
    idx_ref[0] = idx.astype(I32)


def _cmp_sample(rel_bias, q8, kvc, kvc_new, w8, gates8, q_pos):
    b, nblk = kvc.shape[:2]
    nk = 2 * HEAD_DIM
    assert nblk + 1 <= nk and q_pos // CMP_BLOCK == nblk
    return pl.pallas_call(
        functools.partial(_cmp_sample_kernel, q_pos=q_pos),
        out_shape=(jax.ShapeDtypeStruct((b, N_KV, HROWS, HEAD_DIM), F32),
                   jax.ShapeDtypeStruct((b, N_KV, N_SEL), I32)),
        grid=(b,),
        in_specs=[pl.BlockSpec(memory_space=pltpu.SMEM),
                  pl.BlockSpec((1, N_KV, HROWS, HEAD_DIM), lambda i: (i, 0, 0, 0)),
                  pl.BlockSpec((1, nblk, KV_ROWS, HEAD_DIM), lambda i: (i, 0, 0, 0)),
                  pl.BlockSpec((1, KV_ROWS, HEAD_DIM), lambda i: (i, 0, 0)),
                  pl.BlockSpec((CMP_BLOCK, KV_ROWS, HEAD_DIM), lambda i: (0, 0, 0)),
                  pl.BlockSpec((1, 1, N_KV, HROWS, 1), lambda i: (i, 0, 0, 0, 0))],
        out_specs=(pl.BlockSpec((1, N_KV, HROWS, HEAD_DIM), lambda i: (i, 0, 0, 0)),
                   pl.BlockSpec((1, N_KV, N_SEL), lambda i: (i, 0, 0))),
        scratch_shapes=[pltpu.VMEM((nk, HEAD_DIM), F32), pltpu.VMEM((nk, HEAD_DIM), F32),
                        pltpu.VMEM((HROWS, nk), F32)],
        compiler_params=_cparams(1),
        name="cmp_sample",
    )(rel_bias, q8, kvc, kvc_new, w8, gates8)


def _sel_bias_kernel(tab_ref, o_ref, *, q_pos):
    g = pl.program_id(0)
    shape = o_ref.shape[1:]
    blk = lax.broadcasted_iota(I32, shape, 0)
    rows = lax.broadcasted_iota(I32, shape, 1)
    offs = lax.broadcasted_iota(I32, shape, 2)
    dist = q_pos - (blk * CMP_BLOCK + offs)
    biases = _bias_lookup(_bucket(dist), tab_ref, [g * HPG + h for h in range(HPG)])
    bias = jnp.zeros(shape, F32)
    for h in range(HPG):
        bias = jnp.where(rows == h, biases[h], bias)
    o_ref[0] = jnp.where(dist >= 0, bias, NEG_INF)


def _sel_bias(rel_bias, q_pos, n_blocks):
    return pl.pallas_call(
        functools.partial(_sel_bias_kernel, q_pos=q_pos),
        out_shape=jax.ShapeDtypeStruct((N_KV, n_blocks, HROWS, CMP_BLOCK), F32),
        grid=(N_KV,),
        in_specs=[pl.BlockSpec(memory_space=pltpu.SMEM)],
        out_specs=pl.BlockSpec((1, n_blocks, HROWS, CMP_BLOCK), lambda g: (g, 0, 0, 0)),
        compiler_params=_cparams(1),
        name="sel_bias",
    )(rel_bias)


def _sel_sample_kernel(pt_ref, idx_ref, bias_ref, q_ref, *refs, n_past):
    nb = SEL_BLOCKS_PER_STEP
    blk_refs = refs[:N_KV * nb]
    new_ref, gate_ref, os_ref, kcat_ref, vcat_ref, bcat_ref, m_ref, l_ref, acc_ref = refs[N_KV * nb:]
    b = pl.program_id(0)
    step = pl.program_id(1)

    @pl.when(step == 0)
    def _():
        _attn_init(m_ref, l_ref, acc_ref)

    first = lax.broadcasted_iota(I32, (CMP_BLOCK, HEAD_DIM), 0) == 0
    for g in range(N_KV):
        kn = jnp.where(first, new_ref[0][g:g + 1, :], 0.0)
        vn = jnp.where(first, new_ref[0][N_KV + g:N_KV + g + 1, :], 0.0)
        for j in range(nb):
            n = idx_ref[b, g, step * nb + j]
            is_new = n >= n_past
            blk_ref = blk_refs[g * nb + j]
            keys = slice(j * CMP_BLOCK, (j + 1) * CMP_BLOCK)
            kcat_ref[keys, :] = jnp.where(is_new, kn, blk_ref[0, :, g, :]).astype(BF16)
            vcat_ref[keys, :] = jnp.where(is_new, vn, blk_ref[0, :, N_KV + g, :]).astype(BF16)
            bcat_ref[:, keys] = bias_ref[g, n]
        _attn_update(g, q_ref[0, g].astype(BF16), kcat_ref[...], vcat_ref[...], bcat_ref[...], None,
                     m_ref, l_ref, acc_ref)

    @pl.when(step == pl.num_programs(1) - 1)
    def _():
        for g in range(N_KV):
            o = acc_ref[g] / jnp.maximum(l_ref[g], 1e-30)
            os_ref[0, g] = _sigmoid(gate_ref[0, 0, g]) * o


def _sel_sample(page_table, idx, rel_bias, q8, cache_half, kvs_new, gates8, q_pos):
    b = q8.shape[0]
    nb = SEL_BLOCKS_PER_STEP
    n_past = q_pos // CMP_BLOCK
    per_page = PAGE_SIZE // CMP_BLOCK
    bias = _sel_bias(rel_bias, q_pos, n_past + 1)

    def blk_spec(g, j):
        def index(i, s, pt, ix):
            n = jnp.minimum(ix[i, g, s * nb + j], n_past - 1)
            return (pt[i, n // per_page] * per_page + n % per_page, 0, 0, 0)
        return pl.BlockSpec((1, CMP_BLOCK, KV_ROWS, HEAD_DIM), index)

    hspec = pl.BlockSpec((1, N_KV, HROWS, HEAD_DIM), lambda i, s, pt, ix: (i, 0, 0, 0))
    grid_spec = pltpu.PrefetchScalarGridSpec(
        num_scalar_prefetch=2,
        grid=(b, N_SEL // nb),
        in_specs=[pl.BlockSpec(bias.shape, lambda i, s, pt, ix: (0, 0, 0, 0)), hspec]
        + [blk_spec(g, j) for g in range(N_KV) for j in range(nb)]
        + [pl.BlockSpec((1, KV_ROWS, HEAD_DIM), lambda i, s, pt, ix: (i, 0, 0)),
           pl.BlockSpec((1, 1, N_KV, HROWS, 1), lambda i, s, pt, ix: (i, 1, 0, 0, 0))],
        out_specs=hspec,
        scratch_shapes=[pltpu.VMEM((nb * CMP_BLOCK, HEAD_DIM), BF16),
                        pltpu.VMEM((nb * CMP_BLOCK, HEAD_DIM), BF16),
                        pltpu.VMEM((HROWS, nb * CMP_BLOCK), F32),
                        pltpu.VMEM((N_KV, HROWS, 1), F32), pltpu.VMEM((N_KV, HROWS, 1), F32),
                        pltpu.VMEM((N_KV, HROWS, HEAD_DIM), F32)],
    )
    return pl.pallas_call(
        functools.partial(_sel_sample_kernel, n_past=n_past),
        out_shape=jax.ShapeDtypeStruct((b, N_KV, HROWS, HEAD_DIM), F32),
        grid_spec=grid_spec,
        compiler_params=_cparams(2),
        name="sel_sample",
    )(page_table, idx, bias, q8, *([cache_half] * (N_KV * nb)), kvs_new, gates8)


def _win_sample_kernel(tab_ref, q_ref, win_ref, new_ref, gate_ref, oc_ref, os_ref,
                       yb_ref, nwin_ref, kall_ref, vall_ref):
    wc = win_ref.shape[1]
    nk = kall_ref.shape[0]
    nwin_ref[0, 0:wc - 1] = win_ref[0, 1:wc]
    nwin_ref[0, wc - 1] = new_ref[0]
    kidx = lax.broadcasted_iota(I32, (1, nk), 1)
    dist = wc - kidx
    valid = jnp.where(dist >= 0, dist, WINDOW) < WINDOW
    bkt = _bucket(dist)
    kall_ref[...] = jnp.zeros(kall_ref.shape, F32)
    vall_ref[...] = jnp.zeros(vall_ref.shape, F32)
    for g in range(N_KV):
        kall_ref[0:wc, :] = win_ref[0, :, g, :]
        vall_ref[0:wc, :] = win_ref[0, :, N_KV + g, :]
        kall_ref[wc:wc + 1, :] = new_ref[0][g:g + 1, :]
        vall_ref[wc:wc + 1, :] = new_ref[0][N_KV + g:N_KV + g + 1, :]
        biases = _bias_lookup(bkt, tab_ref, [g * HPG + h for h in range(HPG)])
        rows = lax.broadcasted_iota(I32, (HROWS, nk), 0)
        bias = jnp.zeros((HROWS, nk), F32)
        for h in range(HPG):
            bias = jnp.where(rows == h, biases[h], bias)
        s = _dot_nt(q_ref[0, g].astype(BF16), kall_ref[...].astype(BF16)) * SCALE + bias
        p = _masked_softmax(s, valid)
        o = _dot(p.astype(BF16), vall_ref[...].astype(BF16))
        yb_ref[0, g] = oc_ref[0, g] + os_ref[0, g] + _sigmoid(gate_ref[0, 0, g]) * o


def _win_sample(rel_bias, q8, win, kvw_new, gates8, oc, osel):
    b, wc = win.shape[:2]
    nk = wc + HEAD_DIM
    hspec = pl.BlockSpec((1, N_KV, HROWS, HEAD_DIM), lambda i: (i, 0, 0, 0))
    wspec = pl.BlockSpec((1, wc, KV_ROWS, HEAD_DIM), lambda i: (i, 0, 0, 0))
    return pl.pallas_call(
        _win_sample_kernel,
        out_shape=(jax.ShapeDtypeStruct((b, N_KV, HROWS, HEAD_DIM), F32),
                   jax.ShapeDtypeStruct(win.shape, F32)),
        grid=(b,),
        in_specs=[pl.BlockSpec(memory_space=pltpu.SMEM),
                  hspec,
                  wspec,
                  pl.BlockSpec((1, KV_ROWS, HEAD_DIM), lambda i: (i, 0, 0)),
                  pl.BlockSpec((1, 1, N_KV, HROWS, 1), lambda i: (i, 2, 0, 0, 0)),
                  hspec, hspec],
        out_specs=(hspec, wspec),
        scratch_shapes=[pltpu.VMEM((nk, HEAD_DIM), F32), pltpu.VMEM((nk, HEAD_DIM), F32)],
        compiler_params=_cparams(1),
        name="win_sample",
    )(rel_bias, q8, win, kvw_new, gates8, oc, osel)


def _out_and_ffn(x, ya, yb, z, x_s, ya_s, yb_s, z_s, w_proj_a, w_proj_b, w_out, norm_ffn,
                 w_gate, w_up, w_down, norm_final, *, tm, tm_norm):
    bs = x_s.shape[0]
    half = w_gate.shape[1] // 2
    mix, mix_s = _mix(ya, yb, w_proj_a, w_proj_b, z, ya_s, yb_s, z_s, tm=tm, tn=512)
    h, h_s = _mm_res(mix, w_out, x, mix_s, x_s, tm=tm, tn=512, name="out_proj")
    hn = _rmsnorm(h, norm_ffn, BF16, tm_norm)
    hn_s = _rmsnorm(h_s, norm_ffn, BF16, bs)
    ff, ff_s = _swiglu(hn, w_gate, w_up, hn_s, tm=2 * tm, tn=256)
    y, y_s = _mm_res(ff, w_down, h, ff_s, h_s, tm=tm, tn=256, k_block=0, tk=half, name="ffn_down0")
    y, y_s = _mm_res(ff, w_down, y, ff_s, y_s, tm=tm, tn=256, k_block=1, tk=half, name="ffn_down1")
    return _rmsnorm(y, norm_final, F32, tm_norm), _rmsnorm(y_s, norm_final, F32, bs)


def kernel(x_prompt, x_sample, cache_cmp_kv, cache_sel_kv, cache_win_kv, state_rglru_h, state_conv,
           page_table, rel_bias, norm_mix, w_in, conv_w, conv_b, lru_wa, lru_ba, lru_wi, lru_bi,
           lru_lambda, nsa_w_cmp, w_proj_a, w_proj_b, w_out, norm_ffn, w_gate, w_up, w_down,
           norm_final):
    depth = w_in.shape[0]
    assert depth == 1, "single-layer trunk"
    bp, tp, _ = x_prompt.shape
    bs, ts, _ = x_sample.shape
    assert ts == 1
    n_pool = cache_cmp_kv.shape[1]
    past_len = page_table.shape[1] * PAGE_SIZE
    wc = cache_win_kv.shape[2]

    w_in0 = jnp.swapaxes(w_in[0], 0, 1)
    wexp = jnp.broadcast_to(nsa_w_cmp[0][..., None], (2, N_KV, CMP_BLOCK, HEAD_DIM))
    w8 = jnp.broadcast_to(nsa_w_cmp[0].reshape(KV_ROWS, CMP_BLOCK).T[..., None],
                          (CMP_BLOCK, KV_ROWS, HEAD_DIM))
    layer_w = (conv_w[0], conv_b[0], lru_wa[0], lru_ba[0], lru_wi[0], lru_bi[0], lru_lambda[0])
    tail_w = (w_proj_a[0], w_proj_b[0], w_out[0], norm_ffn[0], w_gate[0], w_up[0], w_down[0],
              norm_final)

    mp = bp * tp
    xp = x_prompt.reshape(mp, D_MODEL)
    xs = x_sample.reshape(bs, D_MODEL)
    per_page = PAGE_SIZE // CMP_BLOCK
    cache_c = cache_cmp_kv.reshape(n_pool, PAGE_SIZE, KV_ROWS, HEAD_DIM)
    xn = _rmsnorm(xp, norm_mix[0], BF16, 256)
    xn_s = _rmsnorm(xs, norm_mix[0], BF16, bs)
    z, zs, kvc = _in_proj_all(xn, w_in0, xn_s, (cache_c, page_table, w8), tm=1024, tn=IN_TN)

    ya, conv_p, h_p = _rglru_prompt(z, bp, tp, *layer_w, tt=128)
    gn_t = z[:, Z_GN:Z_GN + 3 * N_HEADS].reshape(mp, 3, N_KV, HPG).transpose(1, 2, 3, 0)
    bd, bo = _bias_tiles(rel_bias, ATT_T)
    oc, sel = _cmp_prompt(z, rel_bias, wexp, gn_t, bp, tp)
    osel = _sel_prompt(z, rel_bias, bd, bo, gn_t, sel, bp, tp)
    yb = _win_prompt(z, rel_bias, bd, bo, gn_t, oc, osel, bp, tp)

    kv_shape = (1, bp, tp // PAGE_SIZE, PAGE_SIZE, 2, N_KV, HEAD_DIM)
    cmp_p = z[:, OFF_KVC:OFF_KVS].reshape(kv_shape)
    sel_p = z[:, OFF_KVS:OFF_KVW].reshape(kv_shape)
    wlen = min(WINDOW, tp)
    win_p = z[:, OFF_KVW:MAIN_W].reshape(bp, tp, 2, N_KV, HEAD_DIM)[None, :, tp - wlen:]

    buf_t = state_conv[0].transpose(1, 0, 2)
    ya_s, h_s = _rglru_step(zs, buf_t, state_rglru_h[0], *layer_w)

    q8 = jnp.pad(zs[:, OFF_Q:OFF_KVC].reshape(bs, N_KV, HPG, HEAD_DIM),
                 ((0, 0), (0, 0), (0, HROWS - HPG), (0, 0)))
    gates8 = jnp.pad(zs[:, Z_GN:Z_GN + 3 * N_HEADS].reshape(bs, 3, N_KV, HPG),
                     ((0, 0), (0, 0), (0, 0), (0, HROWS - HPG)))[..., None]
    kvc_new = zs[:, OFF_KVC:OFF_KVS].reshape(bs, KV_ROWS, HEAD_DIM)
    kvs_new = zs[:, OFF_KVS:OFF_KVW].reshape(bs, KV_ROWS, HEAD_DIM)
    kvw_new = zs[:, OFF_KVW:MAIN_W].reshape(bs, KV_ROWS, HEAD_DIM)
    cache_s = cache_sel_kv.reshape(n_pool * per_page, CMP_BLOCK, KV_ROWS, HEAD_DIM)
    win = cache_win_kv.reshape(bs, wc, KV_ROWS, HEAD_DIM)

    oc_s, idx = _cmp_sample(rel_bias, q8, kvc, kvc_new, w8, gates8, past_len)
    os_s = _sel_sample(page_table, idx, rel_bias, q8, cache_s, kvs_new, gates8, past_len)
    yb8, win_next = _win_sample(rel_bias, q8, win, kvw_new, gates8, oc_s, os_s)
    yb_s = yb8[:, :, :HPG].reshape(bs, Q_W).astype(BF16)

    y_prompt, y_sample = _out_and_ffn(xp, ya, yb, z, xs, ya_s, yb_s, zs, *tail_w, tm=1024, tm_norm=256)

    row_shape = (1, bs, 1, 2, N_KV, HEAD_DIM)
    cmp_s = kvc_new.reshape(row_shape)
    sel_s = kvs_new.reshape(row_shape)
    win_s = win_next.reshape(1, bs, wc, 2, N_KV, HEAD_DIM)
    conv_s = jnp.concatenate([state_conv[0], zs[:, None, OFF_XR:OFF_GR]], axis=1)[:, 1:][None]

    return (y_prompt.reshape(bp, tp, D_MODEL), y_sample.reshape(bs, ts, D_MODEL),
            cmp_p, sel_p, win_p, h_p.reshape(1, bp, D_RNN), conv_p[None],
            cmp_s, sel_s, win_s, h_s[None], conv_s)
```
